```python
import jax, jax.numpy as jnp
from jax import lax
import numpy as np

D_MODEL = 1024
BATCH = 4
SEQ = 8192
DEPTH = 2

GRID_W = 64
CTX_LEN = 256
NORM_EPS = 1e-6
F32 = jnp.float32

D_CONV = 512
D_RWKV = 512
RWKV_HEAD = 64
RWKV_HEADS = D_RWKV // RWKV_HEAD
RWKV_DECAY_RANK = 64
RWKV_ICLR_RANK = 64
RWKV_GATE_RANK = 128
RWKV_LN_EPS = 64e-5
D_NAT = 512
NAT_HEAD = 64
NAT_HEADS = D_NAT // NAT_HEAD
NAT_ROWS = 8
NAT_COLS = 16
GLA_HEADS = 4
GLA_DK = 64
GLA_DV = 128
D_GLA_K = GLA_HEADS * GLA_DK
D_GLA_V = GLA_HEADS * GLA_DV
GLA_GATE_RANK = 16
GLA_GATE_TEMP = 16.0
GLA_CHUNK = 64
ROPE_BASE = 10000.0
N_EXPERTS = 16
D_EXPERT = 1536
EC_CAPACITY_FACTOR = 2

EVEN_SPLIT = [D_CONV, D_CONV, D_CONV, D_RWKV, D_RWKV, D_RWKV, RWKV_DECAY_RANK, RWKV_ICLR_RANK, RWKV_GATE_RANK]
ODD_SPLIT = [D_NAT, D_NAT, D_NAT, D_GLA_K, D_GLA_K, D_GLA_V, D_GLA_V, GLA_GATE_RANK]
EVEN_IN = sum(EVEN_SPLIT)
ODD_IN = sum(ODD_SPLIT)

kernel_name = 'hybrid_diffusion_conv_rwkv7_natten_gla_ecmoe'


def _split(p, sizes):
    return jnp.split(p, np.cumsum(sizes)[:-1].tolist(), axis=-1)


def rmsnorm(x, g, eps=NORM_EPS):
    xf = x.astype(F32)
    y = xf * lax.rsqrt(jnp.mean(xf * xf, axis=-1, keepdims=True) + eps)
    return (y * g.astype(F32)).astype(x.dtype)


def modulate(x, g, shift, scale):
    return rmsnorm(x, g) * (1 + scale) + shift


def conv3(u, w):
    up = jnp.pad(u, ((0, 0), (1, 1), (0, 0)))
    return up[:, :-2] * w[0] + up[:, 1:-1] * w[1] + up[:, 2:] * w[2]


def rope_2d(x):
    L, d = x.shape[1], x.shape[-1]
    half, nf = d // 2, d // 4
    t = jnp.arange(L)
    inv = ROPE_BASE ** (-jnp.arange(nf, dtype=F32) / nf)

    def rot(u, pos):
        ang = pos.astype(F32)[:, None] * inv[None, :]
        cos, sin = jnp.cos(ang)[None, :, None, :], jnp.sin(ang)[None, :, None, :]
        u1, u2 = u[..., :nf].astype(F32), u[..., nf:].astype(F32)
        return jnp.concatenate([u1 * cos - u2 * sin, u1 * sin + u2 * cos], axis=-1)

    return jnp.concatenate([rot(x[..., :half], t // GRID_W), rot(x[..., half:], t % GRID_W)], axis=-1).astype(x.dtype)


def ec_moe(h, router, w1, w3, w2):
    B, T, D = h.shape
    cap = EC_CAPACITY_FACTOR * T // N_EXPERTS
    aff = jax.nn.softmax((h @ router).astype(F32), axis=-1)
    gate, idx = lax.top_k(jnp.swapaxes(aff, 1, 2), cap)
    xin = jax.vmap(lambda hb, ib: hb[ib])(h, idx)
    hid = jax.nn.silu(jnp.einsum('becd,edf->becf', xin, w1)) * jnp.einsum('becd,edf->becf', xin, w3)
    y = jnp.einsum('becf,efd->becd', hid, w2) * gate[..., None].astype(h.dtype)
    out = jax.vmap(lambda ib, yb: jnp.zeros((T, D), yb.dtype).at[ib.reshape(-1)].add(yb.reshape(-1, D)))(idx, y)
    return out.astype(h.dtype)


def rwkv_scan(r, w, k, v, kk, a, S0, reverse, emit):
    xs = tuple(jnp.moveaxis(t, 1, 0) for t in (r, w, k, v, kk, a))

    def step(S, inp):
        rt, wt, kt, vt, kkt, at = inp
        sa = jnp.einsum('bhij,bhj->bhi', S, -kkt)
        S = S * wt[:, :, None, :] + sa[..., None] * (kkt * at)[:, :, None, :] + vt[..., :, None] * kt[:, :, None, :]
        if emit:
            return S, jnp.einsum('bhij,bhj->bhi', S, rt)
        return S, None

    S, ys = lax.scan(step, S0, xs, reverse=reverse)
    return (jnp.moveaxis(ys, 0, 1) if emit else None), S


def rwkv_readout(y, r, k, v, xg, r_k, g_up, ln_g, ln_b):
    B, L = r.shape[:2]
    hs = lambda t: t.reshape(B, L, RWKV_HEADS, RWKV_HEAD)
    yf = y.astype(F32)
    mu = jnp.mean(yf, axis=-1, keepdims=True)
    var = jnp.mean(jnp.square(yf - mu), axis=-1, keepdims=True)
    yn = ((yf - mu) * lax.rsqrt(var + RWKV_LN_EPS)).reshape(B, L, D_RWKV) * ln_g + ln_b
    bonus = (jnp.sum(hs(r) * hs(k) * r_k, axis=-1, keepdims=True) * hs(v)).reshape(B, L, D_RWKV)
    g = jax.nn.sigmoid(xg) @ g_up
    return ((yn + bonus) * g).astype(r.dtype)


def even_mixer(h_lat, h_ctx, w_in, w_out, conv_w, k_k, k_a, r_k, w0, w_up, a0, a_up, g_up, ln_g, ln_b, need_ctx):
    B = h_lat.shape[0]
    p_lat = _split(h_lat @ w_in, EVEN_SPLIT)
    p_ctx = _split(h_ctx @ w_in, EVEN_SPLIT)

    def conv_branch(u, gate_b, gate_c):
        return gate_b * conv3(gate_c * u, conv_w)

    def dir_inputs(r, k, v, xw, xa, d):
        Bq, L = r.shape[:2]
        hs = lambda t: t.reshape(Bq, L, RWKV_HEADS, RWKV_HEAD)
        w_raw = (w0[d] + jnp.tanh(xw) @ w_up[d]).astype(F32)
        decay = jnp.exp(-jnp.exp(-jax.nn.softplus(-w_raw) - 0.5))
        a = jax.nn.sigmoid(a0[d] + xa @ a_up[d])
        kkf = hs((k * k_k).astype(F32))
        kk = kkf * lax.rsqrt(jnp.sum(kkf * kkf, axis=-1, keepdims=True) + 1e-12)
        k_eff = k * (1 + (a - 1) * k_a)
        return hs(r), hs(decay), hs(k_eff), hs(v), kk, hs(a)

    S0 = jnp.zeros((B, RWKV_HEADS, RWKV_HEAD, RWKV_HEAD), F32)
    y_lat, y_ctx = 0.0, 0.0
    for d, rev in ((0, False), (1, True)):
        yc, Sc = rwkv_scan(*dir_inputs(*p_ctx[3:8], d), S0, rev, need_ctx)
        yl, _ = rwkv_scan(*dir_inputs(*p_lat[3:8], d), Sc, rev, True)
        y_lat = y_lat + yl
        if need_ctx:
            y_ctx = y_ctx + yc
    out_lat = jnp.concatenate([conv_branch(*p_lat[0:3]),
                               rwkv_readout(y_lat, *p_lat[3:6], p_lat[8], r_k, g_up, ln_g, ln_b)], axis=-1) @ w_out
    out_ctx = None
    if need_ctx:
        out_ctx = jnp.concatenate([conv_branch(*p_ctx[0:3]),
                                   rwkv_readout(y_ctx, *p_ctx[3:6], p_ctx[8], r_k, g_up, ln_g, ln_b)], axis=-1) @ w_out
    return out_lat, out_ctx


def nat_latent(q, k, v, kc, vc, rpb):
    B, H, L, dh = q.shape
    rows = L // GRID_W
    kr = min(NAT_ROWS, rows)
    kw = NAT_COLS
    qg = q.reshape(B, H, rows, GRID_W, dh)
    kg = k.reshape(B, H, rows, GRID_W, dh)
    vg = v.reshape(B, H, rows, GRID_W, dh)
    col = jnp.arange(GRID_W)
    col_idx = jnp.clip(col - kw // 2, 0, GRID_W - kw)[:, None] + jnp.arange(kw)[None, :]
    col_bias_idx = col_idx - col[:, None] + (NAT_COLS - 1)
    scale = dh ** -0.5

    def row_block(r):
        rs = jnp.clip(r - kr // 2, 0, rows - kr)
        q_r = lax.dynamic_index_in_dim(qg, r, axis=2, keepdims=False)
        k_r = lax.dynamic_slice_in_dim(kg, rs, kr, axis=2)
        v_r = lax.dynamic_slice_in_dim(vg, rs, kr, axis=2)
        k_w = k_r[:, :, :, col_idx]
        v_w = v_r[:, :, :, col_idx]
        row_bias_idx = rs + jnp.arange(kr) - r + (NAT_ROWS - 1)
        bias = rpb[:, row_bias_idx[None, :, None], col_bias_idx[:, None, :]]
        s_loc = jnp.einsum('bhqd,bhrqwd->bhqrw', q_r, k_w) * scale + bias
        s_ctx = jnp.einsum('bhqd,bhcd->bhqc', q_r, kc) * scale
        s = jnp.concatenate([s_loc.reshape(B, H, GRID_W, kr * kw), s_ctx], axis=-1).astype(F32)
        p = jax.nn.softmax(s, axis=-1).astype(v.dtype)
        p_loc = p[..., :kr * kw].reshape(B, H, GRID_W, kr, kw)
        p_ctx = p[..., kr * kw:]
        return jnp.einsum('bhqrw,bhrqwd->bhqd', p_loc, v_w) + jnp.einsum('bhqc,bhcd->bhqd', p_ctx, vc)

    out = lax.map(row_block, jnp.arange(rows))
    return jnp.moveaxis(out, 0, 2).reshape(B, H, L, dh)


def ctx_attention(q, k, v):
    s = jnp.einsum('bhqd,bhkd->bhqk', q, k).astype(F32) * (q.shape[-1] ** -0.5)
    p = jax.nn.softmax(s, axis=-1).astype(v.dtype)
    return jnp.einsum('bhqk,bhkd->bhqd', p, v)


def gla_chunked(q, k, v, logg, S0, emit):
    B, L, H, dk = q.shape
    dv = v.shape[-1]
    C = GLA_CHUNK
    n = L // C
    to_chunks = lambda t: jnp.moveaxis(t.reshape(B, n, C, H, t.shape[-1]), 1, 0)
    xs = tuple(to_chunks(t) for t in (q, k, v, logg))
    mask = jnp.tril(jnp.ones((C, C), bool))[None, :, :, None, None]

    def step(S, inp):
        qc, kc, vc, gc = inp
        b = jnp.cumsum(gc.astype(F32), axis=1)
        bC = b[:, -1]
        S_new = S * jnp.exp(bC)[..., None] + jnp.einsum('bjhd,bjhv->bhdv', kc * jnp.exp(bC[:, None] - b), vc)
        if not emit:
            return S_new, None
        decay = jnp.exp(jnp.where(mask, b[:, :, None] - b[:, None, :], -jnp.inf))
        A = jnp.einsum('bihd,bjhd,bijhd->bhij', qc, kc, decay)
        o = jnp.einsum('bhij,bjhv->bihv', A, vc) + jnp.einsum('bihd,bhdv->bihv', qc * jnp.exp(b), S)
        return S_new, o

    S, ys = lax.scan(step, S0, xs)
    o = jnp.moveaxis(ys, 0, 1).reshape(B, L, H, dv) if emit else None
    return o, S


def gla_log_gate(ga, a_up_d, a_b_d):
    B, L = ga.shape[:2]
    lg = jax.nn.log_sigmoid((ga @ a_up_d + a_b_d).astype(F32)) / GLA_GATE_TEMP
    return lg.reshape(B, L, GLA_HEADS, GLA_DK)


def gla_readout(o, gr, ln_g):
    B, L = gr.shape[:2]
    return (rmsnorm(o, ln_g).reshape(B, L, D_GLA_V) * jax.nn.silu(gr)).astype(gr.dtype)


def odd_mixer(h_lat, h_ctx, w_in, w_out, qn_g, kn_g, rpb, a_up, a_b, gla_ln_g, need_ctx):
    B, L = h_lat.shape[:2]
    Lc = h_ctx.shape[1]
    nq, nk, nv, gq, gk, gv, gr, ga = _split(h_lat @ w_in, ODD_SPLIT)
    cnq, cnk, cnv, cgq, cgk, cgv, cgr, cga = _split(h_ctx @ w_in, ODD_SPLIT)

    def nat_heads(t, g=None):
        t = t.reshape(t.shape[0], t.shape[1], NAT_HEADS, NAT_HEAD)
        if g is not None:
            t = rmsnorm(t, g)
        return jnp.swapaxes(t, 1, 2)

    kc, vc = nat_heads(cnk, kn_g), nat_heads(cnv)
    nat_lat = nat_latent(nat_heads(nq, qn_g), nat_heads(nk, kn_g), nat_heads(nv), kc, vc, rpb)
    nat_lat = jnp.swapaxes(nat_lat, 1, 2).reshape(B, L, D_NAT)

    gh = lambda t, d: t.reshape(t.shape[0], t.shape[1], GLA_HEADS, d)
    qscale = GLA_DK ** -0.5
    q = rope_2d(gh(gq, GLA_DK)) * qscale
    k = rope_2d(gh(gk, GLA_DK))
    v = gh(gv, GLA_DV)
    qc, kcg, vcg = gh(cgq, GLA_DK) * qscale, gh(cgk, GLA_DK), gh(cgv, GLA_DV)
    S0 = jnp.zeros((B, GLA_HEADS, GLA_DK, GLA_DV), F32)
    o_lat, o_ctx = 0.0, 0.0
    for d in range(2):
        f = (lambda t: t) if d == 0 else (lambda t: jnp.flip(t, axis=1))
        oc, Sc = gla_chunked(f(qc), f(kcg), f(vcg), f(gla_log_gate(cga, a_up[d], a_b[d])), S0, need_ctx)
        ol, _ = gla_chunked(f(q), f(k), f(v), f(gla_log_gate(ga, a_up[d], a_b[d])), Sc, True)
        o_lat = o_lat + f(ol)
        if need_ctx:
            o_ctx = o_ctx + f(oc)
    out_lat = jnp.concatenate([nat_lat, gla_readout(o_lat, gr, gla_ln_g)], axis=-1) @ w_out
    out_ctx = None
    if need_ctx:
        nat_ctx = jnp.swapaxes(ctx_attention(nat_heads(cnq, qn_g), kc, vc), 1, 2).reshape(B, Lc, D_NAT)
        out_ctx = jnp.concatenate([nat_ctx, gla_readout(o_ctx, cgr, gla_ln_g)], axis=-1) @ w_out
    return out_lat, out_ctx


def setup_inputs(seed: int = 0) -> dict:
    key = jax.random.key(seed)
    keys = iter(jax.random.split(key, 48))
    ne, no = (DEPTH + 1) // 2, DEPTH // 2

    def normal(shape, std):
        return std * jax.random.normal(next(keys), shape, F32)

    def lin(shape, fan_in, gain=1.0):
        return normal(shape, gain * fan_in ** -0.5)

    def gain(shape):
        return 1.0 + normal(shape, 0.02)

    return {
        'x': normal((BATCH, SEQ, D_MODEL), 1.0),
        'c': normal((BATCH, D_MODEL), 1.0),
        'ctx': normal((BATCH, CTX_LEN, D_MODEL), 1.0),
        'c_ctx': normal((D_MODEL,), 1.0),
        'ada_w': lin((DEPTH, D_MODEL, 6 * D_MODEL), D_MODEL, 0.5),
        'ada_b': normal((DEPTH, 6 * D_MODEL), 0.02),
        'norm1_g': gain((DEPTH, D_MODEL)),
        'norm2_g': gain((DEPTH, D_MODEL)),
        'ev_w_in': lin((ne, D_MODEL, EVEN_IN), D_MODEL),
        'ev_w_out': lin((ne, D_MODEL, D_MODEL), D_MODEL),
        'conv_w': lin((ne, 3, D_CONV), 3),
        'rw_k_k': 0.85 + normal((ne, D_RWKV), 0.02),
        'rw_k_a': gain((ne, D_RWKV)),
        'rw_r_k': normal((ne, RWKV_HEADS, RWKV_HEAD), 0.1),
        'rw_w0': normal((ne, 2, D_RWKV), 1.0),
        'rw_w_up': lin((ne, 2, RWKV_DECAY_RANK, D_RWKV), RWKV_DECAY_RANK, 0.5),
        'rw_a0': normal((ne, 2, D_RWKV), 0.5),
        'rw_a_up': lin((ne, 2, RWKV_ICLR_RANK, D_RWKV), RWKV_ICLR_RANK, 0.5),
        'rw_g_up': lin((ne, RWKV_GATE_RANK, D_RWKV), RWKV_GATE_RANK),
        'rw_ln_g': gain((ne, D_RWKV)),
        'rw_ln_b': normal((ne, D_RWKV), 0.02),
        'od_w_in': lin((no, D_MODEL, ODD_IN), D_MODEL),
        'od_w_out': lin((no, D_MODEL, D_MODEL), D_MODEL),
        'nat_qn_g': gain((no, NAT_HEAD)),
        'nat_kn_g': gain((no, NAT_HEAD)),
        'nat_rpb': normal((no, NAT_HEADS, 2 * NAT_ROWS - 1, 2 * NAT_COLS - 1), 0.1),
        'gla_a_up': lin((no, 2, GLA_GATE_RANK, D_GLA_K), GLA_GATE_RANK),
        'gla_a_b': normal((no, 2, D_GLA_K), 0.5),
        'gla_ln_g': gain((no, GLA_DV)),
        'moe_router': lin((DEPTH, D_MODEL, N_EXPERTS), D_MODEL),
        'moe_w1': lin((DEPTH, N_EXPERTS, D_MODEL, D_EXPERT), D_MODEL),
        'moe_w3': lin((DEPTH, N_EXPERTS, D_MODEL, D_EXPERT), D_MODEL),
        'moe_w2': lin((DEPTH, N_EXPERTS, D_EXPERT, D_MODEL), D_EXPERT),
    }


def reference(x, c, ctx, c_ctx, ada_w, ada_b, norm1_g, norm2_g,
              ev_w_in, ev_w_out, conv_w, rw_k_k, rw_k_a, rw_r_k, rw_w0, rw_w_up, rw_a0, rw_a_up, rw_g_up,
              rw_ln_g, rw_ln_b, od_w_in, od_w_out, nat_qn_g, nat_kn_g, nat_rpb, gla_a_up, gla_a_b, gla_ln_g,
              moe_router, moe_w1, moe_w3, moe_w2):
    ctx_s = ctx
    silu_c = jax.nn.silu(c)
    silu_cc = jax.nn.silu(c_ctx)
    for l in range(DEPTH):
        last = l == DEPTH - 1
        j = l // 2
        sh1, sc1, gt1, sh2, sc2, gt2 = _split((silu_c @ ada_w[l] + ada_b[l])[:, None, :], [D_MODEL] * 6)
        csh1, csc1, cgt1, csh2, csc2, cgt2 = _split(silu_cc @ ada_w[l] + ada_b[l], [D_MODEL] * 6)
        h_lat = modulate(x, norm1_g[l], sh1, sc1)
        h_ctx = modulate(ctx_s, norm1_g[l], csh1, csc1)
        if l % 2 == 0:
            y_lat, y_ctx = even_mixer(h_lat, h_ctx, ev_w_in[j], ev_w_out[j], conv_w[j], rw_k_k[j], rw_k_a[j], rw_r_k[j],
                                      rw_w0[j], rw_w_up[j], rw_a0[j], rw_a_up[j], rw_g_up[j], rw_ln_g[j], rw_ln_b[j],
                                      not last)
        else:
            y_lat, y_ctx = odd_mixer(h_lat, h_ctx, od_w_in[j], od_w_out[j], nat_qn_g[j], nat_kn_g[j], nat_rpb[j],
                                     gla_a_up[j], gla_a_b[j], gla_ln_g[j], not last)
        x = x + gt1 * y_lat
        x = x + gt2 * ec_moe(modulate(x, norm2_g[l], sh2, sc2), moe_router[l], moe_w1[l], moe_w3[l], moe_w2[l])
        if not last:
            ctx_s = ctx_s + cgt1 * y_ctx
            ctx_s = ctx_s + cgt2 * ec_moe(modulate(ctx_s, norm2_g[l], csh2, csc2),
                                          moe_router[l], moe_w1[l], moe_w3[l], moe_w2[l])
    return x
```

```python
import functools

import jax
import jax.numpy as jnp
import numpy as np
from jax import lax
from jax.experimental import pallas as pl
from jax.experimental.pallas import tpu as pltpu

D_MODEL = 1024
GRID_W = 64
NORM_EPS = 1e-6
F32 = jnp.float32
BF16 = jnp.bfloat16

D_CONV = 512
D_RWKV = 512
RWKV_HEAD = 64
RWKV_HEADS = D_RWKV // RWKV_HEAD
RWKV_DECAY_RANK = 64
RWKV_ICLR_RANK = 64
RWKV_GATE_RANK = 128
RWKV_LN_EPS = 64e-5
D_NAT = 512
NAT_HEAD = 64
NAT_HEADS = D_NAT // NAT_HEAD
NAT_ROWS = 8
NAT_COLS = 16
GLA_HEADS = 4
GLA_DK = 64
GLA_DV = 128
D_GLA_K = GLA_HEADS * GLA_DK
D_GLA_V = GLA_HEADS * GLA_DV
GLA_GATE_RANK = 16
GLA_GATE_TEMP = 16.0
GLA_CHUNK = 64
ROPE_BASE = 10000.0
N_EXPERTS = 16
EC_CAPACITY_FACTOR = 2

EVEN_SPLIT = [D_CONV, D_CONV, D_CONV, D_RWKV, D_RWKV, D_RWKV, RWKV_DECAY_RANK, RWKV_ICLR_RANK, RWKV_GATE_RANK]
ODD_SPLIT = [D_NAT, D_NAT, D_NAT, D_GLA_K, D_GLA_K, D_GLA_V, D_GLA_V, GLA_GATE_RANK]


def _out_proj_body(a_ref, w_ref, x_ref, g_ref, o_ref):
    acc = jnp.dot(a_ref[0].astype(BF16), w_ref[...], preferred_element_type=F32)
    o_ref[0] = x_ref[0] + g_ref[0] * acc


def out_proj_residual(a, w, x, gate, block_rows=512):
    B, L, K = a.shape
    N = w.shape[1]
    tm = min(block_rows, L)
    assert L % tm == 0
    gate = jnp.broadcast_to(gate.reshape(-1, 1, N), (B, 1, N))
    return pl.pallas_call(
        _out_proj_body,
        grid=(B, L // tm),
        in_specs=[
            pl.BlockSpec((1, tm, K), lambda b, i: (b, i, 0)),
            pl.BlockSpec((K, N), lambda b, i: (0, 0)),
            pl.BlockSpec((1, tm, N), lambda b, i: (b, i, 0)),
            pl.BlockSpec((1, 1, N), lambda b, i: (b, 0, 0)),
        ],
        out_specs=pl.BlockSpec((1, tm, N), lambda b, i: (b, i, 0)),
        out_shape=jax.ShapeDtypeStruct((B, L, N), F32),
        compiler_params=pltpu.CompilerParams(dimension_semantics=("parallel", "parallel")),
        name="out_proj_residual",
    )(a, w.astype(BF16), x, gate)


def _split(p, sizes):
    return jnp.split(p, np.cumsum(sizes)[:-1].tolist(), axis=-1)


def rmsnorm(x, g, eps=NORM_EPS):
    xf = x.astype(F32)
    y = xf * lax.rsqrt(jnp.mean(xf * xf, axis=-1, keepdims=True) + eps)
    return (y * g.astype(F32)).astype(x.dtype)


def modulate(x, g, shift, scale):
    return rmsnorm(x, g) * (1 + scale) + shift


def conv3(u, w):
    up = jnp.pad(u, ((0, 0), (1, 1), (0, 0)))
    return up[:, :-2] * w[0] + up[:, 1:-1] * w[1] + up[:, 2:] * w[2]


def rope_2d(x):
    L, d = x.shape[1], x.shape[-1]
    half, nf = d // 2, d // 4
    t = jnp.arange(L)
    inv = ROPE_BASE ** (-jnp.arange(nf, dtype=F32) / nf)

    def rot(u, pos):
        ang = pos.astype(F32)[:, None] * inv[None, :]
        cos, sin = jnp.cos(ang)[None, :, None, :], jnp.sin(ang)[None, :, None, :]
        u1, u2 = u[..., :nf].astype(F32), u[..., nf:].astype(F32)
        return jnp.concatenate([u1 * cos - u2 * sin, u1 * sin + u2 * cos], axis=-1)

    return jnp.concatenate([rot(x[..., :half], t // GRID_W), rot(x[..., half:], t % GRID_W)], axis=-1).astype(x.dtype)


def ec_moe(h, router, w1, w3, w2):
    B, T, D = h.shape
    cap = EC_CAPACITY_FACTOR * T // N_EXPERTS
    aff = jax.nn.softmax((h @ router).astype(F32), axis=-1)
    gate, idx = lax.top_k(jnp.swapaxes(aff, 1, 2), cap)
    xin = jax.vmap(lambda hb, ib: hb[ib])(h, idx)
    hid = jax.nn.silu(jnp.einsum('becd,edf->becf', xin, w1)) * jnp.einsum('becd,edf->becf', xin, w3)
    y = jnp.einsum('becf,efd->becd', hid, w2) * gate[..., None].astype(h.dtype)
    out = jax.vmap(lambda ib, yb: jnp.zeros((T, D), yb.dtype).at[ib.reshape(-1)].add(yb.reshape(-1, D)))(idx, y)
    return out.astype(h.dtype)


def rwkv_scan(r, w, k, v, kk, a, S0, reverse, emit):
    xs = tuple(jnp.moveaxis(t, 1, 0) for t in (r, w, k, v, kk, a))

    def step(S, inp):
        rt, wt, kt, vt, kkt, at = inp
        sa = jnp.einsum('bhij,bhj->bhi', S, -kkt)
        S = S * wt[:, :, None, :] + sa[..., None] * (kkt * at)[:, :, None, :] + vt[..., :, None] * kt[:, :, None, :]
        if emit:
            return S, jnp.einsum('bhij,bhj->bhi', S, rt)
        return S, None

    S, ys = lax.scan(step, S0, xs, reverse=reverse)
    return (jnp.moveaxis(ys, 0, 1) if emit else None), S


def rwkv_readout(y, r, k, v, xg, r_k, g_up, ln_g, ln_b):
    B, L = r.shape[:2]
    hs = lambda t: t.reshape(B, L, RWKV_HEADS, RWKV_HEAD)
    yf = y.astype(F32)
    mu = jnp.mean(yf, axis=-1, keepdims=True)
    var = jnp.mean(jnp.square(yf - mu), axis=-1, keepdims=True)
    yn = ((yf - mu) * lax.rsqrt(var + RWKV_LN_EPS)).reshape(B, L, D_RWKV) * ln_g + ln_b
    bonus = (jnp.sum(hs(r) * hs(k) * r_k, axis=-1, keepdims=True) * hs(v)).reshape(B, L, D_RWKV)
    g = jax.nn.sigmoid(xg) @ g_up
    return ((yn + bonus) * g).astype(r.dtype)


def even_mixer(h_lat, h_ctx, w_in, conv_w, k_k, k_a, r_k, w0, w_up, a0, a_up, g_up, ln_g, ln_b, need_ctx):
    B = h_lat.shape[0]
    p_lat = _split(h_lat @ w_in, EVEN_SPLIT)
    p_ctx = _split(h_ctx @ w_in, EVEN_SPLIT)

    def conv_branch(u, gate_b, gate_c):
        return gate_b * conv3(gate_c * u, conv_w)

    def dir_inputs(r, k, v, xw, xa, d):
        Bq, L = r.shape[:2]
        hs = lambda t: t.reshape(Bq, L, RWKV_HEADS, RWKV_HEAD)
        w_raw = (w0[d] + jnp.tanh(xw) @ w_up[d]).astype(F32)
        decay = jnp.exp(-jnp.exp(-jax.nn.softplus(-w_raw) - 0.5))
        a = jax.nn.sigmoid(a0[d] + xa @ a_up[d])
        kkf = hs((k * k_k).astype(F32))
        kk = kkf * lax.rsqrt(jnp.sum(kkf * kkf, axis=-1, keepdims=True) + 1e-12)
        k_eff = k * (1 + (a - 1) * k_a)
        return hs(r), hs(decay), hs(k_eff), hs(v), kk, hs(a)

    S0 = jnp.zeros((B, RWKV_HEADS, RWKV_HEAD, RWKV_HEAD), F32)
    y_lat, y_ctx = 0.0, 0.0
    for d, rev in ((0, False), (1, True)):
        yc, Sc = rwkv_scan(*dir_inputs(*p_ctx[3:8], d), S0, rev, need_ctx)
        yl, _ = rwkv_scan(*dir_inputs(*p_lat[3:8], d), Sc, rev, True)
        y_lat = y_lat + yl
        if need_ctx:
            y_ctx = y_ctx + yc
    cat_lat = jnp.concatenate([conv_branch(*p_lat[0:3]),
                               rwkv_readout(y_lat, *p_lat[3:6], p_lat[8], r_k, g_up, ln_g, ln_b)], axis=-1)
    cat_ctx = None
    if need_ctx:
        cat_ctx = jnp.concatenate([conv_branch(*p_ctx[0:3]),
                                   rwkv_readout(y_ctx, *p_ctx[3:6], p_ctx[8], r_k, g_up, ln_g, ln_b)], axis=-1)
    return cat_lat, cat_ctx


def nat_latent(q, k, v, kc, vc, rpb):
    B, H, L, dh = q.shape
    rows = L // GRID_W
    kr = min(NAT_ROWS, rows)
    kw = NAT_COLS
    qg = q.reshape(B, H, rows, GRID_W, dh)
    kg = k.reshape(B, H, rows, GRID_W, dh)
    vg = v.reshape(B, H, rows, GRID_W, dh)
    col = jnp.arange(GRID_W)
    col_idx = jnp.clip(col - kw // 2, 0, GRID_W - kw)[:, None] + jnp.arange(kw)[None, :]
    col_bias_idx = col_idx - col[:, None] + (NAT_COLS - 1)
    scale = dh ** -0.5

    def row_block(r):
        rs = jnp.clip(r - kr // 2, 0, rows - kr)
        q_r = lax.dynamic_index_in_dim(qg, r, axis=2, keepdims=False)
        k_r = lax.dynamic_slice_in_dim(kg, rs, kr, axis=2)
        v_r = lax.dynamic_slice_in_dim(vg, rs, kr, axis=2)
        k_w = k_r[:, :, :, col_idx]
        v_w = v_r[:, :, :, col_idx]
        row_bias_idx = rs + jnp.arange(kr) - r + (NAT_ROWS - 1)
        bias = rpb[:, row_bias_idx[None, :, None], col_bias_idx[:, None, :]]
        s_loc = jnp.einsum('bhqd,bhrqwd->bhqrw', q_r, k_w) * scale + bias
        s_ctx = jnp.einsum('bhqd,bhcd->bhqc', q_r, kc) * scale
        s = jnp.concatenate([s_loc.reshape(B, H, GRID_W, kr * kw), s_ctx], axis=-1).astype(F32)
        p = jax.nn.softmax(s, axis=-1).astype(v.dtype)
        p_loc = p[..., :kr * kw].reshape(B, H, GRID_W, kr, kw)
        p_ctx = p[..., kr * kw:]
        return jnp.einsum('bhqrw,bhrqwd->bhqd', p_loc, v_w) + jnp.einsum('bhqc,bhcd->bhqd', p_ctx, vc)

    out = lax.map(row_block, jnp.arange(rows))
    return jnp.moveaxis(out, 0, 2).reshape(B, H, L, dh)


def ctx_attention(q, k, v):
    s = jnp.einsum('bhqd,bhkd->bhqk', q, k).astype(F32) * (q.shape[-1] ** -0.5)
    p = jax.nn.softmax(s, axis=-1).astype(v.dtype)
    return jnp.einsum('bhqk,bhkd->bhqd', p, v)


def gla_chunked(q, k, v, logg, S0, emit):
    B, L, H, dk = q.shape
    dv = v.shape[-1]
    C = GLA_CHUNK
    n = L // C
    to_chunks = lambda t: jnp.moveaxis(t.reshape(B, n, C, H, t.shape[-1]), 1, 0)
    xs = tuple(to_chunks(t) for t in (q, k, v, logg))
    mask = jnp.tril(jnp.ones((C, C), bool))[None, :, :, None, None]

    def step(S, inp):
        qc, kc, vc, gc = inp
        b = jnp.cumsum(gc.astype(F32), axis=1)
        bC = b[:, -1]
        S_new = S * jnp.exp(bC)[..., None] + jnp.einsum('bjhd,bjhv->bhdv', kc * jnp.exp(bC[:, None] - b), vc)
        if not emit:
            return S_new, None
        decay = jnp.exp(jnp.where(mask, b[:, :, None] - b[:, None, :], -jnp.inf))
        A = jnp.einsum('bihd,bjhd,bijhd->bhij', qc, kc, decay)
        o = jnp.einsum('bhij,bjhv->bihv', A, vc) + jnp.einsum('bihd,bhdv->bihv', qc * jnp.exp(b), S)
        return S_new, o

    S, ys = lax.scan(step, S0, xs)
    o = jnp.moveaxis(ys, 0, 1).reshape(B, L, H, dv) if emit else None
    return o, S


def gla_log_gate(ga, a_up_d, a_b_d):
    B, L = ga.shape[:2]
    lg = jax.nn.log_sigmoid((ga @ a_up_d + a_b_d).astype(F32)) / GLA_GATE_TEMP
    return lg.reshape(B, L, GLA_HEADS, GLA_DK)


def gla_readout(o, gr, ln_g):
    B, L = gr.shape[:2]
    return (rmsnorm(o, ln_g).reshape(B, L, D_GLA_V) * jax.nn.silu(gr)).astype(gr.dtype)


def odd_mixer(h_lat, h_ctx, w_in, qn_g, kn_g, rpb, a_up, a_b, gla_ln_g, need_ctx):
    B, L = h_lat.shape[:2]
    Lc = h_ctx.shape[1]
    nq, nk, nv, gq, gk, gv, gr, ga = _split(h_lat @ w_in, ODD_SPLIT)
    cnq, cnk, cnv, cgq, cgk, cgv, cgr, cga = _split(h_ctx @ w_in, ODD_SPLIT)

    def nat_heads(t, g=None):
        t = t.reshape(t.shape[0], t.shape[1], NAT_HEADS, NAT_HEAD)
        if g is not None:
            t = rmsnorm(t, g)
        return jnp.swapaxes(t, 1, 2)

    kc, vc = nat_heads(cnk, kn_g), nat_heads(cnv)
    nat_lat = nat_latent(nat_heads(nq, qn_g), nat_heads(nk, kn_g), nat_heads(nv), kc, vc, rpb)
    nat_lat = jnp.swapaxes(nat_lat, 1, 2).reshape(B, L, D_NAT)

    gh = lambda t, d: t.reshape(t.shape[0], t.shape[1], GLA_HEADS, d)
    qscale = GLA_DK ** -0.5
    q = rope_2d(gh(gq, GLA_DK)) * qscale
    k = rope_2d(gh(gk, GLA_DK))
    v = gh(gv, GLA_DV)
    qc, kcg, vcg = gh(cgq, GLA_DK) * qscale, gh(cgk, GLA_DK), gh(cgv, GLA_DV)
    S0 = jnp.zeros((B, GLA_HEADS, GLA_DK, GLA_DV), F32)
    o_lat, o_ctx = 0.0, 0.0
    for d in range(2):
        f = (lambda t: t) if d == 0 else (lambda t: jnp.flip(t, axis=1))
        oc, Sc = gla_chunked(f(qc), f(kcg), f(vcg), f(gla_log_gate(cga, a_up[d], a_b[d])), S0, need_ctx)
        ol, _ = gla_chunked(f(q), f(k), f(v), f(gla_log_gate(ga, a_up[d], a_b[d])), Sc, True)
        o_lat = o_lat + f(ol)
        if need_ctx:
            o_ctx = o_ctx + f(oc)
    cat_lat = jnp.concatenate([nat_lat, gla_readout(o_lat, gr, gla_ln_g)], axis=-1)
    cat_ctx = None
    if need_ctx:
        nat_ctx = jnp.swapaxes(ctx_attention(nat_heads(cnq, qn_g), kc, vc), 1, 2).reshape(B, Lc, D_NAT)
        cat_ctx = jnp.concatenate([nat_ctx, gla_readout(o_ctx, cgr, gla_ln_g)], axis=-1)
    return cat_lat, cat_ctx


def kernel(x, c, ctx, c_ctx, ada_w, ada_b, norm1_g, norm2_g, ev_w_in, ev_w_out, conv_w, rw_k_k, rw_k_a, rw_r_k, rw_w0, rw_w_up, rw_a0, rw_a_up, rw_g_up, rw_ln_g, rw_ln_b, od_w_in, od_w_out, nat_qn_g, nat_kn_g, nat_rpb, gla_a_up, gla_a_b, gla_ln_g, moe_router, moe_w1, moe_w3, moe_w2):
    depth = ada_w.shape[0]
    ctx_s = ctx
    silu_c = jax.nn.silu(c)
    silu_cc = jax.nn.silu(c_ctx)
    for l in range(depth):
        last = l == depth - 1
        j = l // 2
        sh1, sc1, gt1, sh2, sc2, gt2 = _split((silu_c @ ada_w[l] + ada_b[l])[:, None, :], [D_MODEL] * 6)
        csh1, csc1, cgt1, csh2, csc2, cgt2 = _split(silu_cc @ ada_w[l] + ada_b[l], [D_MODEL] * 6)
        h_lat = modulate(x, norm1_g[l], sh1, sc1)
        h_ctx = modulate(ctx_s, norm1_g[l], csh1, csc1)
        if l % 2 == 0:
            cat_lat, cat_ctx = even_mixer(h_lat, h_ctx, ev_w_in[j], conv_w[j], rw_k_k[j], rw_k_a[j], rw_r_k[j],
                                          rw_w0[j], rw_w_up[j], rw_a0[j], rw_a_up[j], rw_g_up[j], rw_ln_g[j],
                                          rw_ln_b[j], not last)
            w_out = ev_w_out[j]
        else:
            cat_lat, cat_ctx = odd_mixer(h_lat, h_ctx, od_w_in[j], nat_qn_g[j], nat_kn_g[j], nat_rpb[j],
                                         gla_a_up[j], gla_a_b[j], gla_ln_g[j], not last)
            w_out = od_w_out[j]
        x = out_proj_residual(cat_lat, w_out, x, gt1)
        x = x + gt2 * ec_moe(modulate(x, norm2_g[l], sh2, sc2), moe_router[l], moe_w1[l], moe_w3[l], moe_w2[l])
        if not last:
            ctx_s = out_proj_residual(cat_ctx, w_out, ctx_s, cgt1)
            ctx_s = ctx_s + cgt2 * ec_moe(modulate(ctx_s, norm2_g[l], csh2, csc2),
                                          moe_router[l], moe_w1[l], moe_w3[l], moe_w2[l])
    return x
```

```python
import functools

import jax
import jax.numpy as jnp
import numpy as np
from jax import lax
from jax.experimental import pallas as pl
from jax.experimental.pallas import tpu as pltpu

D_MODEL = 1024
GRID_W = 64
NORM_EPS = 1e-6
F32 = jnp.float32
BF16 = jnp.bfloat16

D_CONV = 512
D_RWKV = 512
RWKV_HEAD = 64
RWKV_HEADS = D_RWKV // RWKV_HEAD
RWKV_DECAY_RANK = 64
RWKV_ICLR_RANK = 64
RWKV_GATE_RANK = 128
RWKV_LN_EPS = 64e-5
D_NAT = 512
NAT_HEAD = 64
NAT_HEADS = D_NAT // NAT_HEAD
NAT_ROWS = 8
NAT_COLS = 16
GLA_HEADS = 4
GLA_DK = 64
GLA_DV = 128
D_GLA_K = GLA_HEADS * GLA_DK
D_GLA_V = GLA_HEADS * GLA_DV
GLA_GATE_RANK = 16
GLA_GATE_TEMP = 16.0
GLA_CHUNK = 64
ROPE_BASE = 10000.0
N_EXPERTS = 16
EC_CAPACITY_FACTOR = 2

EVEN_SPLIT = [D_CONV, D_CONV, D_CONV, D_RWKV, D_RWKV, D_RWKV, RWKV_DECAY_RANK, RWKV_ICLR_RANK, RWKV_GATE_RANK]
ODD_SPLIT = [D_NAT, D_NAT, D_NAT, D_GLA_K, D_GLA_K, D_GLA_V, D_GLA_V, GLA_GATE_RANK]


def _out_proj_body(a_ref, w_ref, x_ref, g_ref, o_ref):
    acc = jnp.dot(a_ref[0].astype(BF16), w_ref[...], preferred_element_type=F32)
    o_ref[0] = x_ref[0] + g_ref[0] * acc


def out_proj_residual(a, w, x, gate, block_rows=512):
    B, L, K = a.shape
    N = w.shape[1]
    tm = min(block_rows, L)
    assert L % tm == 0
    gate = jnp.broadcast_to(gate.reshape(-1, 1, N), (B, 1, N))
    return pl.pallas_call(
        _out_proj_body,
        grid=(B, L // tm),
        in_specs=[
            pl.BlockSpec((1, tm, K), lambda b, i: (b, i, 0)),
            pl.BlockSpec((K, N), lambda b, i: (0, 0)),
            pl.BlockSpec((1, tm, N), lambda b, i: (b, i, 0)),
            pl.BlockSpec((1, 1, N), lambda b, i: (b, 0, 0)),
        ],
        out_specs=pl.BlockSpec((1, tm, N), lambda b, i: (b, i, 0)),
        out_shape=jax.ShapeDtypeStruct((B, L, N), F32),
        compiler_params=pltpu.CompilerParams(dimension_semantics=("parallel", "parallel")),
        name="out_proj_residual",
    )(a, w.astype(BF16), x, gate)


RWKV_CHUNK = 64
PAIR = 2 * RWKV_HEAD


def _mm(a, b):
    return jnp.dot(a.astype(BF16), b.astype(BF16), preferred_element_type=F32)


def _mm_nt(a, b):
    return lax.dot_general(a.astype(BF16), b.astype(BF16), (((1,), (1,)), ((), ())), preferred_element_type=F32)


def _mm_tn(a, b):
    return lax.dot_general(a.astype(BF16), b.astype(BF16), (((0,), (0,)), ((), ())), preferred_element_type=F32)


def _rwkv_chunk_body(r_ref, k_ref, v_ref, kk_ref, lw_ref, a_ref, ka_ref, s0_ref, y_ref, sT_ref, st_scr, *, reverse):
    C = RWKV_CHUNK
    c = pl.program_id(2)

    @pl.when(c == 0)
    def _():
        st_scr[...] = s0_ref[0, 0]

    r, k, v, kk, lw, a = r_ref[0], k_ref[0], v_ref[0], kk_ref[0], lw_ref[0], a_ref[0]
    keff = k * (1.0 + (a - 1.0) * ka_ref[...])
    b = kk * a

    row = lax.broadcasted_iota(jnp.int32, (PAIR, PAIR), 0)
    col = lax.broadcasted_iota(jnp.int32, (PAIR, PAIR), 1)
    same_head = (row // RWKV_HEAD) == (col // RWKV_HEAD)
    t_i, s_i = row % C, col % C
    before = (s_i > t_i) if reverse else (s_i < t_i)
    upto = before | (s_i == t_i)

    ct = lax.broadcasted_iota(jnp.int32, (C, C), 0)
    cs = lax.broadcasted_iota(jnp.int32, (C, C), 1)
    tri = ((cs >= ct) if reverse else (cs <= ct)).astype(F32)
    cum_in = jnp.dot(tri, lw, preferred_element_type=F32, precision=lax.Precision.HIGHEST)
    cum_ex = cum_in - lw
    tot = jnp.sum(lw, axis=0, keepdims=True)
    e_neg = jnp.exp(-cum_in)
    e_rem = jnp.exp(tot - cum_in)

    def expand(x):
        return jnp.where(same_head, jnp.concatenate([x, x], axis=0), 0.0)

    at2 = expand(-kk * jnp.exp(cum_ex))
    rt2 = expand(r * jnp.exp(cum_in))
    bh2 = expand(b * e_neg)
    kh2 = expand(keff * e_neg)
    bp2 = expand(b * e_rem)
    kp2 = expand(keff * e_rem)
    v2 = expand(v)

    pp = _mm_nt(jnp.concatenate([at2, rt2], axis=0), jnp.concatenate([bh2, kh2], axis=0))
    a_ab = jnp.where(before, pp[:PAIR, :PAIR], 0.0)
    a_ak = jnp.where(before, pp[:PAIR, PAIR:], 0.0)
    a_rb = jnp.where(upto, pp[PAIR:, :PAIR], 0.0)
    a_rk = jnp.where(upto, pp[PAIR:, PAIR:], 0.0)

    tinv = jnp.where(row == col, 1.0, 0.0) + a_ab
    npow = a_ab
    for _ in range(int(np.log2(C)) - 1):
        npow = _mm(npow, npow)
        tinv = tinv + _mm(tinv, npow)

    av = _mm(jnp.concatenate([a_ak, a_rk], axis=0), v2)
    x = _mm(tinv, jnp.concatenate([at2, av[:PAIR]], axis=1))
    z = _mm(a_rb, x)
    qe = rt2 + z[:, :PAIR]
    yloc = z[:, PAIR:] + av[PAIR:]
    bx = _mm_tn(bp2, x)
    mt = jnp.where(row == col, jnp.exp(tot), 0.0) + bx[:, :PAIR]
    gt = bx[:, PAIR:] + _mm_tn(kp2, v2)

    st = st_scr[...]
    y2 = _mm(qe, st) + yloc
    st_new = _mm(mt, st) + gt
    st_scr[...] = st_new
    y_ref[0] = y2[:C] + y2[C:]

    @pl.when(c == pl.num_programs(2) - 1)
    def _():
        sT_ref[0, 0] = st_new


def rwkv_state_pack(S):
    B, H = S.shape[:2]
    St = jnp.swapaxes(S, -1, -2).reshape(B, H // 2, 2, RWKV_HEAD, RWKV_HEAD)
    z = jnp.zeros_like(St[:, :, 0])
    top = jnp.concatenate([St[:, :, 0], z], axis=-1)
    bot = jnp.concatenate([z, St[:, :, 1]], axis=-1)
    return jnp.concatenate([top, bot], axis=-2)


def rwkv_state_unpack(St):
    B, P = St.shape[:2]
    h0 = St[:, :, :RWKV_HEAD, :RWKV_HEAD]
    h1 = St[:, :, RWKV_HEAD:, RWKV_HEAD:]
    return jnp.swapaxes(jnp.stack([h0, h1], axis=2).reshape(B, 2 * P, RWKV_HEAD, RWKV_HEAD), -1, -2)


def rwkv_chunked(r, k, v, kk, lw, a, k_a, st0, *, reverse, interpret=False):
    B, L, D = r.shape
    C = RWKV_CHUNK
    assert L % C == 0 and D % PAIR == 0
    n, P = L // C, D // PAIR
    cidx = (lambda c: n - 1 - c) if reverse else (lambda c: c)
    seq = pl.BlockSpec((1, C, PAIR), lambda b, p, c: (b, cidx(c), p))
    state = pl.BlockSpec((1, 1, PAIR, PAIR), lambda b, p, c: (b, p, 0, 0))
    return pl.pallas_call(
        functools.partial(_rwkv_chunk_body, reverse=reverse),
        grid=(B, P, n),
        in_specs=[seq] * 6 + [pl.BlockSpec((1, PAIR), lambda b, p, c: (0, p)), state],
        out_specs=[seq, state],
        out_shape=[jax.ShapeDtypeStruct((B, L, D), F32), jax.ShapeDtypeStruct((B, P, PAIR, PAIR), F32)],
        scratch_shapes=[pltpu.VMEM((PAIR, PAIR), F32)],
        compiler_params=pltpu.CompilerParams(dimension_semantics=("parallel", "parallel", "arbitrary")),
        name="rwkv_chunked_rev" if reverse else "rwkv_chunked_fwd",
        interpret=interpret,
    )(r, k, v, kk, lw, a, k_a.reshape(1, D), st0)


NAT_WIN = NAT_ROWS * GRID_W
NAT_ROWS_PER_BLOCK = 16
NAT_MASKED = -1e30


def _pair_rmsnorm(x, g, lane_lo):
    sq = x * x
    s_lo = jnp.sum(jnp.where(lane_lo, sq, 0.0), axis=-1, keepdims=True)
    s_hi = jnp.sum(sq, axis=-1, keepdims=True) - s_lo
    ms = jnp.where(lane_lo, s_lo, s_hi) * (1.0 / NAT_HEAD)
    return x * lax.rsqrt(ms + NORM_EPS) * g


def _nat_body(q_ref, k_ref, v_ref, kc_ref, vc_ref, bias_ref, qg_ref, kg_ref, o_ref, kn_scr, vb_scr, kcn_scr, vcb_scr,
              *, rows):
    rb = pl.program_id(2)
    L = k_ref.shape[1]
    norm_rows = 512
    lane_lo_n = lax.broadcasted_iota(jnp.int32, (norm_rows, PAIR), 1) < NAT_HEAD

    @pl.when(rb == 0)
    def _():
        def norm_block(i, carry):
            sl = pl.ds(pl.multiple_of(i * norm_rows, norm_rows), norm_rows)
            kn_scr[sl, :] = _pair_rmsnorm(k_ref[0, sl, :], kg_ref[...], lane_lo_n).astype(BF16)
            vb_scr[sl, :] = v_ref[0, sl, :].astype(BF16)
            return carry
        lax.fori_loop(0, L // norm_rows, norm_block, 0)
        lane_lo_c = lax.broadcasted_iota(jnp.int32, kc_ref.shape[1:], 1) < NAT_HEAD
        kcn_scr[...] = _pair_rmsnorm(kc_ref[0], kg_ref[...], lane_lo_c).astype(BF16)
        vcb_scr[...] = vc_ref[0].astype(BF16)

    lane_lo = lax.broadcasted_iota(jnp.int32, (GRID_W, PAIR), 1) < NAT_HEAD
    scale = NAT_HEAD ** -0.5

    def one_row(j, carry):
        r = rb * NAT_ROWS_PER_BLOCK + j
        rs = jnp.clip(r - NAT_ROWS // 2, 0, rows - NAT_ROWS)
        delta = r - rs
        qsl = pl.ds(pl.multiple_of(j * GRID_W, GRID_W), GRID_W)
        qn = _pair_rmsnorm(q_ref[0, qsl, :], qg_ref[...], lane_lo) * scale
        wsl = pl.ds(pl.multiple_of(rs * GRID_W, GRID_W), NAT_WIN)
        kwin = kn_scr[wsl, :]
        vwin = vb_scr[wsl, :]
        outs = []
        for h in range(2):
            qh = jnp.where(lane_lo if h == 0 else ~lane_lo, qn, 0.0).astype(BF16)
            s_loc = lax.dot_general(qh, kwin, (((1,), (1,)), ((), ())), preferred_element_type=F32) + bias_ref[h, delta]
            s_ctx = lax.dot_general(qh, kcn_scr[...], (((1,), (1,)), ((), ())), preferred_element_type=F32)
            m = jnp.maximum(jnp.max(s_loc, axis=-1, keepdims=True), jnp.max(s_ctx, axis=-1, keepdims=True))
            p_loc = jnp.exp(s_loc - m)
            p_ctx = jnp.exp(s_ctx - m)
            den = jnp.sum(p_loc, axis=-1, keepdims=True) + jnp.sum(p_ctx, axis=-1, keepdims=True)
            o = (jnp.dot(p_loc.astype(BF16), vwin, preferred_element_type=F32)
                 + jnp.dot(p_ctx.astype(BF16), vcb_scr[...], preferred_element_type=F32))
            outs.append(o / den)
        o_ref[0, qsl, :] = jnp.where(lane_lo, outs[0], outs[1])
        return carry

    lax.fori_loop(0, NAT_ROWS_PER_BLOCK, one_row, 0)


def _nat_bias_table(rpb):
    col = jnp.arange(GRID_W)
    cstart = jnp.clip(col - NAT_COLS // 2, 0, GRID_W - NAT_COLS)
    delta = jnp.arange(NAT_ROWS)
    wrow = jnp.arange(NAT_ROWS)
    ridx = wrow[None, :] - delta[:, None] + (NAT_ROWS - 1)
    cidx = col[None, :] - col[:, None] + (NAT_COLS - 1)
    inwin = (col[None, :] >= cstart[:, None]) & (col[None, :] < cstart[:, None] + NAT_COLS)
    cidx = jnp.clip(cidx, 0, 2 * NAT_COLS - 2)
    tab = rpb[:, ridx[:, None, :, None], cidx[None, :, None, :]]
    tab = jnp.where(inwin[None, None, :, None, :], tab, NAT_MASKED)
    return tab.reshape(rpb.shape[0], NAT_ROWS, GRID_W, NAT_WIN).astype(F32)


def nat_attention(q, k, v, kc, vc, rpb, qn_g, kn_g, interpret=False):
    B, L, D = q.shape
    Lc = kc.shape[1]
    rows = L // GRID_W
    P = D // PAIR
    rpb_blk = NAT_ROWS_PER_BLOCK
    assert rows >= NAT_ROWS and rows % rpb_blk == 0 and L % 512 == 0
    bias = _nat_bias_table(rpb)
    g2 = lambda g: jnp.tile(g, 2).reshape(1, PAIR)
    full = pl.BlockSpec((1, L, PAIR), lambda b, p, i: (b, 0, p))
    cfull = pl.BlockSpec((1, Lc, PAIR), lambda b, p, i: (b, 0, p))
    qblk = pl.BlockSpec((1, rpb_blk * GRID_W, PAIR), lambda b, p, i: (b, i, p))
    gspec = pl.BlockSpec((1, PAIR), lambda b, p, i: (0, 0))
    return pl.pallas_call(
        functools.partial(_nat_body, rows=rows),
        grid=(B, P, rows // rpb_blk),
        in_specs=[qblk, full, full, cfull, cfull,
                  pl.BlockSpec((2, NAT_ROWS, GRID_W, NAT_WIN), lambda b, p, i: (p, 0, 0, 0)), gspec, gspec],
        out_specs=qblk,
        out_shape=jax.ShapeDtypeStruct((B, L, D), F32),
        scratch_shapes=[pltpu.VMEM((L, PAIR), BF16), pltpu.VMEM((L, PAIR), BF16),
                        pltpu.VMEM((Lc, PAIR), BF16), pltpu.VMEM((Lc, PAIR), BF16)],
        compiler_params=pltpu.CompilerParams(dimension_semantics=("parallel", "parallel", "arbitrary"),
                                             vmem_limit_bytes=48 * 1024 * 1024),
        name="nat_attention",
        interpret=interpret,
    )(q, k, v, kc, vc, bias, g2(qn_g), g2(kn_g))


GLA_SUB = 16
GLA_BLOCK = 256
GLA_KPAIR = 2 * GLA_DK
GLA_VPAIR = 2 * GLA_DV


def _gla_body(q_ref, k_ref, v_ref, g_ref, s0_ref, o_ref, sT_ref, st_scr, *, reverse):
    C = GLA_SUB
    nsub = GLA_BLOCK // C
    blk = pl.program_id(2)

    @pl.when(blk == 0)
    def _():
        st_scr[...] = s0_ref[0, 0]

    ti = lax.broadcasted_iota(jnp.int32, (C, C), 0)
    si = lax.broadcasted_iota(jnp.int32, (C, C), 1)
    tri = ((si >= ti) if reverse else (si <= ti)).astype(F32)
    lane_lo = lax.broadcasted_iota(jnp.int32, (C, GLA_KPAIR), 1) < GLA_DK
    row_id = lax.broadcasted_iota(jnp.int32, (C, GLA_KPAIR), 0)
    vrow = lax.broadcasted_iota(jnp.int32, (2 * C, GLA_VPAIR), 0)
    vcol = lax.broadcasted_iota(jnp.int32, (2 * C, GLA_VPAIR), 1)
    v_same_head = (vrow // C) == (vcol // GLA_DV)
    srow = lax.broadcasted_iota(jnp.int32, (GLA_VPAIR, GLA_KPAIR), 0)
    scol = lax.broadcasted_iota(jnp.int32, (GLA_VPAIR, GLA_KPAIR), 1)
    s_same_head = (srow // GLA_DV) == (scol // GLA_DK)

    def sub_chunk(i, carry):
        ci = (nsub - 1 - i) if reverse else i
        sl = pl.ds(pl.multiple_of(ci * C, C), C)
        q, k, v, g = q_ref[0, sl, :], k_ref[0, sl, :], v_ref[0, sl, :], g_ref[0, sl, :]
        b = jnp.dot(tri, g, preferred_element_type=F32, precision=lax.Precision.HIGHEST)
        b_end = jnp.sum(g, axis=0, keepdims=True)
        a_lo = jnp.zeros((C, C), F32)
        a_hi = jnp.zeros((C, C), F32)
        for j in range(C):
            seen = (row_id <= j) if reverse else (row_id >= j)
            f = jnp.where(seen, q * k[j:j + 1, :] * jnp.exp(jnp.where(seen, b - b[j:j + 1, :], 0.0)), 0.0)
            r_lo = jnp.sum(jnp.where(lane_lo, f, 0.0), axis=-1, keepdims=True)
            r_hi = jnp.sum(jnp.where(lane_lo, 0.0, f), axis=-1, keepdims=True)
            a_lo = jnp.where(si == j, r_lo, a_lo)
            a_hi = jnp.where(si == j, r_hi, a_hi)
        v_bd = jnp.where(v_same_head, jnp.concatenate([v, v], axis=0), 0.0)
        st = st_scr[...]
        o = _mm(jnp.concatenate([a_lo, a_hi], axis=1), v_bd) + _mm_nt(q * jnp.exp(b), st)
        o_ref[0, sl, :] = o
        kv = _mm_tn(v, k * jnp.exp(b_end - b))
        st_scr[...] = st * jnp.exp(b_end) + jnp.where(s_same_head, kv, 0.0)
        return carry

    lax.fori_loop(0, nsub, sub_chunk, 0)

    @pl.when(blk == pl.num_programs(2) - 1)
    def _():
        sT_ref[0, 0] = st_scr[...]


def gla_scan(q, k, v, logg, st0, *, reverse, interpret=False):
    B, L, Dk = q.shape
    T = GLA_BLOCK
    assert L % T == 0 and Dk % GLA_KPAIR == 0
    n, P = L // T, Dk // GLA_KPAIR
    bidx = (lambda c: n - 1 - c) if reverse else (lambda c: c)
    kspec = pl.BlockSpec((1, T, GLA_KPAIR), lambda b, p, c: (b, bidx(c), p))
    vspec = pl.BlockSpec((1, T, GLA_VPAIR), lambda b, p, c: (b, bidx(c), p))
    sspec = pl.BlockSpec((1, 1, GLA_VPAIR, GLA_KPAIR), lambda b, p, c: (b, p, 0, 0))
    return pl.pallas_call(
        functools.partial(_gla_body, reverse=reverse),
        grid=(B, P, n),
        in_specs=[kspec, kspec, vspec, kspec, sspec],
        out_specs=[vspec, sspec],
        out_shape=[jax.ShapeDtypeStruct((B, L, P * GLA_VPAIR), F32),
                   jax.ShapeDtypeStruct((B, P, GLA_VPAIR, GLA_KPAIR), F32)],
        scratch_shapes=[pltpu.VMEM((GLA_VPAIR, GLA_KPAIR), F32)],
        compiler_params=pltpu.CompilerParams(dimension_semantics=("parallel", "parallel", "arbitrary")),
        name="gla_scan_rev" if reverse else "gla_scan_fwd",
        interpret=interpret,
    )(q, k, v, logg, st0)


def _split(p, sizes):
    return jnp.split(p, np.cumsum(sizes)[:-1].tolist(), axis=-1)


def rmsnorm(x, g, eps=NORM_EPS):
    xf = x.astype(F32)
    y = xf * lax.rsqrt(jnp.mean(xf * xf, axis=-1, keepdims=True) + eps)
    return (y * g.astype(F32)).astype(x.dtype)


def modulate(x, g, shift, scale):
    return rmsnorm(x, g) * (1 + scale) + shift


def conv3(u, w):
    up = jnp.pad(u, ((0, 0), (1, 1), (0, 0)))
    return up[:, :-2] * w[0] + up[:, 1:-1] * w[1] + up[:, 2:] * w[2]


def rope_2d(x):
    L, d = x.shape[1], x.shape[-1]
    half, nf = d // 2, d // 4
    t = jnp.arange(L)
    inv = ROPE_BASE ** (-jnp.arange(nf, dtype=F32) / nf)

    def rot(u, pos):
        ang = pos.astype(F32)[:, None] * inv[None, :]
        cos, sin = jnp.cos(ang)[None, :, None, :], jnp.sin(ang)[None, :, None, :]
        u1, u2 = u[..., :nf].astype(F32), u[..., nf:].astype(F32)
        return jnp.concatenate([u1 * cos - u2 * sin, u1 * sin + u2 * cos], axis=-1)

    return jnp.concatenate([rot(x[..., :half], t // GRID_W), rot(x[..., half:], t % GRID_W)], axis=-1).astype(x.dtype)


def ec_moe(h, router, w1, w3, w2):
    B, T, D = h.shape
    cap = EC_CAPACITY_FACTOR * T // N_EXPERTS
    aff = jax.nn.softmax((h @ router).astype(F32), axis=-1)
    gate, idx = lax.top_k(jnp.swapaxes(aff, 1, 2), cap)
    xin = jax.vmap(lambda hb, ib: hb[ib])(h, idx)
    hid = jax.nn.silu(jnp.einsum('becd,edf->becf', xin, w1)) * jnp.einsum('becd,edf->becf', xin, w3)
    y = jnp.einsum('becf,efd->becd', hid, w2) * gate[..., None].astype(h.dtype)
    out = jax.vmap(lambda ib, yb: jnp.zeros((T, D), yb.dtype).at[ib.reshape(-1)].add(yb.reshape(-1, D)))(idx, y)
    return out.astype(h.dtype)


def rwkv_readout(y, r, k, v, xg, r_k, g_up, ln_g, ln_b):
    B, L = r.shape[:2]
    hs = lambda t: t.reshape(B, L, RWKV_HEADS, RWKV_HEAD)
    yf = hs(y.astype(F32))
    mu = jnp.mean(yf, axis=-1, keepdims=True)
    var = jnp.mean(jnp.square(yf - mu), axis=-1, keepdims=True)
    yn = ((yf - mu) * lax.rsqrt(var + RWKV_LN_EPS)).reshape(B, L, D_RWKV) * ln_g + ln_b
    bonus = (jnp.sum(hs(r) * hs(k) * r_k, axis=-1, keepdims=True) * hs(v)).reshape(B, L, D_RWKV)
    g = jax.nn.sigmoid(xg) @ g_up
    return ((yn + bonus) * g).astype(r.dtype)


def even_mixer(h_lat, h_ctx, w_in, conv_w, k_k, k_a, r_k, w0, w_up, a0, a_up, g_up, ln_g, ln_b, need_ctx):
    B = h_lat.shape[0]
    p_lat = _split(h_lat @ w_in, EVEN_SPLIT)
    p_ctx = _split(h_ctx @ w_in, EVEN_SPLIT)

    def conv_branch(u, gate_b, gate_c):
        return gate_b * conv3(gate_c * u, conv_w)

    def unit_key(k):
        Bq, L = k.shape[:2]
        kkf = (k * k_k).astype(F32).reshape(Bq, L, RWKV_HEADS, RWKV_HEAD)
        return (kkf * lax.rsqrt(jnp.sum(kkf * kkf, axis=-1, keepdims=True) + 1e-12)).reshape(Bq, L, D_RWKV)

    def dir_gates(xw, xa, d):
        w_raw = (w0[d] + jnp.tanh(xw) @ w_up[d]).astype(F32)
        return -jnp.exp(-jax.nn.softplus(-w_raw) - 0.5), jax.nn.sigmoid(a0[d] + xa @ a_up[d])

    kk_ctx, kk_lat = unit_key(p_ctx[4]), unit_key(p_lat[4])
    st0 = jnp.zeros((B, RWKV_HEADS // 2, PAIR, PAIR), F32)
    y_lat, y_ctx = 0.0, 0.0
    for d, rev in ((0, False), (1, True)):
        yc, st_c = rwkv_chunked(*p_ctx[3:6], kk_ctx, *dir_gates(*p_ctx[6:8], d), k_a, st0, reverse=rev)
        yl, _ = rwkv_chunked(*p_lat[3:6], kk_lat, *dir_gates(*p_lat[6:8], d), k_a, st_c, reverse=rev)
        y_lat = y_lat + yl
        if need_ctx:
            y_ctx = y_ctx + yc
    cat_lat = jnp.concatenate([conv_branch(*p_lat[0:3]),
                               rwkv_readout(y_lat, *p_lat[3:6], p_lat[8], r_k, g_up, ln_g, ln_b)], axis=-1)
    cat_ctx = None
    if need_ctx:
        cat_ctx = jnp.concatenate([conv_branch(*p_ctx[0:3]),
                                   rwkv_readout(y_ctx, *p_ctx[3:6], p_ctx[8], r_k, g_up, ln_g, ln_b)], axis=-1)
    return cat_lat, cat_ctx


def ctx_attention(q, k, v):
    s = jnp.einsum('bhqd,bhkd->bhqk', q, k).astype(F32) * (q.shape[-1] ** -0.5)
    p = jax.nn.softmax(s, axis=-1).astype(v.dtype)
    return jnp.einsum('bhqk,bhkd->bhqd', p, v)


def gla_log_gate(ga, a_up_d, a_b_d):
    B, L = ga.shape[:2]
    lg = jax.nn.log_sigmoid((ga @ a_up_d + a_b_d).astype(F32)) / GLA_GATE_TEMP
    return lg.reshape(B, L, GLA_HEADS, GLA_DK)


def gla_readout(o, gr, ln_g):
    B, L = gr.shape[:2]
    return (rmsnorm(o, ln_g).reshape(B, L, D_GLA_V) * jax.nn.silu(gr)).astype(gr.dtype)


def odd_mixer(h_lat, h_ctx, w_in, qn_g, kn_g, rpb, a_up, a_b, gla_ln_g, need_ctx):
    B, L = h_lat.shape[:2]
    Lc = h_ctx.shape[1]
    nq, nk, nv, gq, gk, gv, gr, ga = _split(h_lat @ w_in, ODD_SPLIT)
    cnq, cnk, cnv, cgq, cgk, cgv, cgr, cga = _split(h_ctx @ w_in, ODD_SPLIT)

    def nat_heads(t, g=None):
        t = t.reshape(t.shape[0], t.shape[1], NAT_HEADS, NAT_HEAD)
        if g is not None:
            t = rmsnorm(t, g)
        return jnp.swapaxes(t, 1, 2)

    nat_lat = nat_attention(nq, nk, nv, cnk, cnv, rpb, qn_g, kn_g)

    gh = lambda t, d: t.reshape(t.shape[0], t.shape[1], GLA_HEADS, d)
    qscale = GLA_DK ** -0.5
    q = (rope_2d(gh(gq, GLA_DK)) * qscale).reshape(B, L, D_GLA_K)
    k = rope_2d(gh(gk, GLA_DK)).reshape(B, L, D_GLA_K)
    qc = cgq * qscale
    st0 = jnp.zeros((B, GLA_HEADS // 2, GLA_VPAIR, GLA_KPAIR), F32)
    o_lat, o_ctx = 0.0, 0.0
    for d in range(2):
        lg_c = gla_log_gate(cga, a_up[d], a_b[d]).reshape(B, Lc, D_GLA_K)
        lg_l = gla_log_gate(ga, a_up[d], a_b[d]).reshape(B, L, D_GLA_K)
        oc, st_c = gla_scan(qc, cgk, cgv, lg_c, st0, reverse=(d == 1))
        ol, _ = gla_scan(q, k, gv, lg_l, st_c, reverse=(d == 1))
        o_lat = o_lat + ol
        if need_ctx:
            o_ctx = o_ctx + oc
    o_lat = gh(o_lat, GLA_DV)
    if need_ctx:
        o_ctx = gh(o_ctx, GLA_DV)
    cat_lat = jnp.concatenate([nat_lat, gla_readout(o_lat, gr, gla_ln_g)], axis=-1)
    cat_ctx = None
    if need_ctx:
        kc, vc = nat_heads(cnk, kn_g), nat_heads(cnv)
        nat_ctx = jnp.swapaxes(ctx_attention(nat_heads(cnq, qn_g), kc, vc), 1, 2).reshape(B, Lc, D_NAT)
        cat_ctx = jnp.concatenate([nat_ctx, gla_readout(o_ctx, cgr, gla_ln_g)], axis=-1)
    return cat_lat, cat_ctx


def kernel(x, c, ctx, c_ctx, ada_w, ada_b, norm1_g, norm2_g, ev_w_in, ev_w_out, conv_w, rw_k_k, rw_k_a, rw_r_k, rw_w0, rw_w_up, rw_a0, rw_a_up, rw_g_up, rw_ln_g, rw_ln_b, od_w_in, od_w_out, nat_qn_g, nat_kn_g, nat_rpb, gla_a_up, gla_a_b, gla_ln_g, moe_router, moe_w1, moe_w3, moe_w2):
    depth = ada_w.shape[0]
    ctx_s = ctx
    silu_c = jax.nn.silu(c)
    silu_cc = jax.nn.silu(c_ctx)
    for l in range(depth):
        last = l == depth - 1
        j = l // 2
        sh1, sc1, gt1, sh2, sc2, gt2 = _split((silu_c @ ada_w[l] + ada_b[l])[:, None, :], [D_MODEL] * 6)
        csh1, csc1, cgt1, csh2, csc2, cgt2 = _split(silu_cc @ ada_w[l] + ada_b[l], [D_MODEL] * 6)
        h_lat = modulate(x, norm1_g[l], sh1, sc1)
        h_ctx = modulate(ctx_s, norm1_g[l], csh1, csc1)
        if l % 2 == 0:
            cat_lat, cat_ctx = even_mixer(h_lat, h_ctx, ev_w_in[j], conv_w[j], rw_k_k[j], rw_k_a[j], rw_r_k[j],
                                          rw_w0[j], rw_w_up[j], rw_a0[j], rw_a_up[j], rw_g_up[j], rw_ln_g[j],
                                          rw_ln_b[j], not last)
            w_out = ev_w_out[j]
        else:
            cat_lat, cat_ctx = odd_mixer(h_lat, h_ctx, od_w_in[j], nat_qn_g[j], nat_kn_g[j], nat_rpb[j],
                                         gla_a_up[j], gla_a_b[j], gla_ln_g[j], not last)
            w_out = od_w_out[j]
        x = out_proj_residual(cat_lat, w_out, x, gt1)
        x = x + gt2 * ec_moe(modulate(x, norm2_g[l], sh2, sc2), moe_router[l], moe_w1[l], moe_w3[l], moe_w2[l])
        if not last:
            ctx_s = out_proj_residual(cat_ctx, w_out, ctx_s, cgt1)
            ctx_s = ctx_s + cgt2 * ec_moe(modulate(ctx_s, norm2_g[l], csh2, csc2),
                                          moe_router[l], moe_w1[l], moe_w3[l], moe_w2[l])
    return x
```

```python
import functools

import jax
import jax.numpy as jnp
import numpy as np
from jax import lax
from jax.experimental import pallas as pl
from jax.experimental.pallas import tpu as pltpu

D_MODEL = 1024
GRID_W = 64
NORM_EPS = 1e-6
F32 = jnp.float32
BF16 = jnp.bfloat16

D_CONV = 512
D_RWKV = 512
RWKV_HEAD = 64
RWKV_HEADS = D_RWKV // RWKV_HEAD
RWKV_DECAY_RANK = 64
RWKV_ICLR_RANK = 64
RWKV_GATE_RANK = 128
RWKV_LN_EPS = 64e-5
D_NAT = 512
NAT_HEAD = 64
NAT_HEADS = D_NAT // NAT_HEAD
NAT_ROWS = 8
NAT_COLS = 16
GLA_HEADS = 4
GLA_DK = 64
GLA_DV = 128
D_GLA_K = GLA_HEADS * GLA_DK
D_GLA_V = GLA_HEADS * GLA_DV
GLA_GATE_RANK = 16
GLA_GATE_TEMP = 16.0
GLA_CHUNK = 64
ROPE_BASE = 10000.0
N_EXPERTS = 16
EC_CAPACITY_FACTOR = 2

EVEN_SPLIT = [D_CONV, D_CONV, D_CONV, D_RWKV, D_RWKV, D_RWKV, RWKV_DECAY_RANK, RWKV_ICLR_RANK, RWKV_GATE_RANK]
ODD_SPLIT = [D_NAT, D_NAT, D_NAT, D_GLA_K, D_GLA_K, D_GLA_V, D_GLA_V, GLA_GATE_RANK]


def _out_proj_body(a_ref, w_ref, x_ref, g_ref, o_ref):
    acc = jnp.dot(a_ref[0].astype(BF16), w_ref[...], preferred_element_type=F32)
    o_ref[0] = x_ref[0] + g_ref[0] * acc


def out_proj_residual(a, w, x, gate, block_rows=512):
    B, L, K = a.shape
    N = w.shape[1]
    tm = min(block_rows, L)
    assert L % tm == 0
    gate = jnp.broadcast_to(gate.reshape(-1, 1, N), (B, 1, N))
    return pl.pallas_call(
        _out_proj_body,
        grid=(B, L // tm),
        in_specs=[
            pl.BlockSpec((1, tm, K), lambda b, i: (b, i, 0)),
            pl.BlockSpec((K, N), lambda b, i: (0, 0)),
            pl.BlockSpec((1, tm, N), lambda b, i: (b, i, 0)),
            pl.BlockSpec((1, 1, N), lambda b, i: (b, 0, 0)),
        ],
        out_specs=pl.BlockSpec((1, tm, N), lambda b, i: (b, i, 0)),
        out_shape=jax.ShapeDtypeStruct((B, L, N), F32),
        compiler_params=pltpu.CompilerParams(dimension_semantics=("parallel", "parallel")),
        name="out_proj_residual",
    )(a, w.astype(BF16), x, gate)


RWKV_CHUNK = 64
PAIR = 2 * RWKV_HEAD


def _mm(a, b):
    return jnp.dot(a.astype(BF16), b.astype(BF16), preferred_element_type=F32)


def _mm_nt(a, b):
    return lax.dot_general(a.astype(BF16), b.astype(BF16), (((1,), (1,)), ((), ())), preferred_element_type=F32)


def _mm_tn(a, b):
    return lax.dot_general(a.astype(BF16), b.astype(BF16), (((0,), (0,)), ((), ())), preferred_element_type=F32)


def _rwkv_chunk_pairs(ins, sts, *, reverse):
    C = RWKV_CHUNK
    row = lax.broadcasted_iota(jnp.int32, (PAIR, PAIR), 0)
    col = lax.broadcasted_iota(jnp.int32, (PAIR, PAIR), 1)
    same_head = (row // RWKV_HEAD) == (col // RWKV_HEAD)
    t_i, s_i = row % C, col % C
    before = (s_i > t_i) if reverse else (s_i < t_i)
    upto = before | (s_i == t_i)

    ct = lax.broadcasted_iota(jnp.int32, (C, C), 0)
    cs = lax.broadcasted_iota(jnp.int32, (C, C), 1)
    tri = ((cs >= ct) if reverse else (cs <= ct)).astype(F32)
    eye = jnp.where(row == col, 1.0, 0.0)

    def expand(x):
        return jnp.where(same_head, jnp.concatenate([x, x], axis=0), 0.0)

    def prepare(r, k, v, kk, lw, a, ka):
        keff = k * (1.0 + (a - 1.0) * ka)
        b = kk * a
        cum_in = jnp.dot(tri, lw, preferred_element_type=F32, precision=lax.Precision.HIGHEST)
        cum_ex = cum_in - lw
        tot = jnp.sum(lw, axis=0, keepdims=True)
        e_neg = jnp.exp(-cum_in)
        e_rem = jnp.exp(tot - cum_in)
        return dict(at2=expand(-kk * jnp.exp(cum_ex)), rt2=expand(r * jnp.exp(cum_in)),
                    bh2=expand(b * e_neg), kh2=expand(keff * e_neg), bp2=expand(b * e_rem),
                    kp2=expand(keff * e_rem), v2=expand(v), dtot=eye * jnp.exp(tot))

    ps = [prepare(*args) for args in ins]
    pps = [_mm_nt(jnp.concatenate([p["at2"], p["rt2"]], axis=0), jnp.concatenate([p["bh2"], p["kh2"]], axis=0))
           for p in ps]
    a_ab = [jnp.where(before, pp[:PAIR, :PAIR], 0.0) for pp in pps]
    a_ak = [jnp.where(before, pp[:PAIR, PAIR:], 0.0) for pp in pps]
    a_rb = [jnp.where(upto, pp[PAIR:, :PAIR], 0.0) for pp in pps]
    a_rk = [jnp.where(upto, pp[PAIR:, PAIR:], 0.0) for pp in pps]

    tinv = [eye + n for n in a_ab]
    npow = a_ab
    for _ in range(int(np.log2(C)) - 1):
        npow = [_mm(n, n) for n in npow]
        tinv = [t + _mm(t, n) for t, n in zip(tinv, npow)]

    av = [_mm(jnp.concatenate([ak, rk], axis=0), p["v2"]) for ak, rk, p in zip(a_ak, a_rk, ps)]
    x = [_mm(t, jnp.concatenate([p["at2"], w[:PAIR]], axis=1)) for t, p, w in zip(tinv, ps, av)]
    z = [_mm(rb, xx) for rb, xx in zip(a_rb, x)]
    bx = [_mm_tn(p["bp2"], xx) for p, xx in zip(ps, x)]
    kv = [_mm_tn(p["kp2"], p["v2"]) for p in ps]
    y2 = [_mm(p["rt2"] + zz[:, :PAIR], st) + zz[:, PAIR:] + w[PAIR:] for p, zz, st, w in zip(ps, z, sts, av)]
    st_new = [_mm(p["dtot"] + b[:, :PAIR], st) + b[:, PAIR:] + g for p, b, st, g in zip(ps, bx, sts, kv)]
    return [y[:C] + y[C:] for y in y2], st_new


def _rwkv_chunk_body(r_ref, k_ref, v_ref, kk_ref, lw_ref, a_ref, ka_ref, s0_ref, y_ref, sT_ref, st_scr, *, reverse):
    c = pl.program_id(1)

    @pl.when(c == 0)
    def _():
        st_scr[...] = s0_ref[0]

    npair = st_scr.shape[0]
    lanes = [slice(p * PAIR, (p + 1) * PAIR) for p in range(npair)]
    ins = [(r_ref[0, :, sl], k_ref[0, :, sl], v_ref[0, :, sl], kk_ref[0, :, sl], lw_ref[0, :, sl], a_ref[0, :, sl],
            ka_ref[:, sl]) for sl in lanes]
    ys, sts = _rwkv_chunk_pairs(ins, [st_scr[p] for p in range(npair)], reverse=reverse)
    for p, sl in enumerate(lanes):
        y_ref[0, :, sl] = ys[p]
        st_scr[p] = sts[p]

    @pl.when(c == pl.num_programs(1) - 1)
    def _():
        sT_ref[0] = st_scr[...]


def rwkv_state_pack(S):
    B, H = S.shape[:2]
    St = jnp.swapaxes(S, -1, -2).reshape(B, H // 2, 2, RWKV_HEAD, RWKV_HEAD)
    z = jnp.zeros_like(St[:, :, 0])
    top = jnp.concatenate([St[:, :, 0], z], axis=-1)
    bot = jnp.concatenate([z, St[:, :, 1]], axis=-1)
    return jnp.concatenate([top, bot], axis=-2)


def rwkv_state_unpack(St):
    B, P = St.shape[:2]
    h0 = St[:, :, :RWKV_HEAD, :RWKV_HEAD]
    h1 = St[:, :, RWKV_HEAD:, RWKV_HEAD:]
    return jnp.swapaxes(jnp.stack([h0, h1], axis=2).reshape(B, 2 * P, RWKV_HEAD, RWKV_HEAD), -1, -2)


def rwkv_chunked(r, k, v, kk, lw, a, k_a, st0, *, reverse, interpret=False):
    B, L, D = r.shape
    C = RWKV_CHUNK
    assert L % C == 0 and D % PAIR == 0
    n, P = L // C, D // PAIR
    cidx = (lambda c: n - 1 - c) if reverse else (lambda c: c)
    seq = pl.BlockSpec((1, C, D), lambda b, c: (b, cidx(c), 0))
    state = pl.BlockSpec((1, P, PAIR, PAIR), lambda b, c: (b, 0, 0, 0))
    return pl.pallas_call(
        functools.partial(_rwkv_chunk_body, reverse=reverse),
        grid=(B, n),
        in_specs=[seq] * 6 + [pl.BlockSpec((1, D), lambda b, c: (0, 0)), state],
        out_specs=[seq, state],
        out_shape=[jax.ShapeDtypeStruct((B, L, D), F32), jax.ShapeDtypeStruct((B, P, PAIR, PAIR), F32)],
        scratch_shapes=[pltpu.VMEM((P, PAIR, PAIR), F32)],
        compiler_params=pltpu.CompilerParams(dimension_semantics=("parallel", "arbitrary")),
        name="rwkv_chunked_rev" if reverse else "rwkv_chunked_fwd",
        interpret=interpret,
    )(r, k, v, kk, lw, a, k_a.reshape(1, D), st0)


NAT_WIN = NAT_ROWS * GRID_W
NAT_ROWS_PER_BLOCK = 16
NAT_MASKED = -1e30


def _pair_rmsnorm(x, g, lane_lo):
    sq = x * x
    s_lo = jnp.sum(jnp.where(lane_lo, sq, 0.0), axis=-1, keepdims=True)
    s_hi = jnp.sum(sq, axis=-1, keepdims=True) - s_lo
    ms = jnp.where(lane_lo, s_lo, s_hi) * (1.0 / NAT_HEAD)
    return x * lax.rsqrt(ms + NORM_EPS) * g


def _nat_body(q_ref, k_ref, v_ref, kc_ref, vc_ref, bias_ref, qg_ref, kg_ref, o_ref, kn_scr, vb_scr, kcn_scr, vcb_scr,
              *, rows):
    rb = pl.program_id(2)
    L = k_ref.shape[1]
    norm_rows = 512
    lane_lo_n = lax.broadcasted_iota(jnp.int32, (norm_rows, PAIR), 1) < NAT_HEAD

    @pl.when(rb == 0)
    def _():
        def norm_block(i, carry):
            sl = pl.ds(pl.multiple_of(i * norm_rows, norm_rows), norm_rows)
            kn_scr[sl, :] = _pair_rmsnorm(k_ref[0, sl, :], kg_ref[...], lane_lo_n).astype(BF16)
            vb_scr[sl, :] = v_ref[0, sl, :].astype(BF16)
            return carry
        lax.fori_loop(0, L // norm_rows, norm_block, 0)
        lane_lo_c = lax.broadcasted_iota(jnp.int32, kc_ref.shape[1:], 1) < NAT_HEAD
        kcn_scr[...] = _pair_rmsnorm(kc_ref[0], kg_ref[...], lane_lo_c).astype(BF16)
        vcb_scr[...] = vc_ref[0].astype(BF16)

    lane_lo = lax.broadcasted_iota(jnp.int32, (GRID_W, PAIR), 1) < NAT_HEAD
    scale = NAT_HEAD ** -0.5

    rows_per_iter = 2
    nt = (((1,), (1,)), ((), ()))

    def row_group(it, carry):
        qsl, kwin, vwin, qh, bias = [], [], [], [], []
        for u in range(rows_per_iter):
            j = it * rows_per_iter + u
            r = rb * NAT_ROWS_PER_BLOCK + j
            rs = jnp.clip(r - NAT_ROWS // 2, 0, rows - NAT_ROWS)
            sl = pl.ds(pl.multiple_of(j * GRID_W, GRID_W), GRID_W)
            wsl = pl.ds(pl.multiple_of(rs * GRID_W, GRID_W), NAT_WIN)
            qn = _pair_rmsnorm(q_ref[0, sl, :], qg_ref[...], lane_lo) * scale
            for h in range(2):
                qsl.append(sl)
                kwin.append(kn_scr[wsl, :])
                vwin.append(vb_scr[wsl, :])
                qh.append(jnp.where(lane_lo if h == 0 else ~lane_lo, qn, 0.0).astype(BF16))
                bias.append(bias_ref[h, r - rs])
        chains = range(2 * rows_per_iter)
        s_loc = [lax.dot_general(qh[i], kwin[i], nt, preferred_element_type=F32) + bias[i] for i in chains]
        s_ctx = [lax.dot_general(qh[i], kcn_scr[...], nt, preferred_element_type=F32) for i in chains]
        m = [jnp.maximum(jnp.max(s_loc[i], axis=-1, keepdims=True), jnp.max(s_ctx[i], axis=-1, keepdims=True))
             for i in chains]
        p_loc = [jnp.exp(s_loc[i] - m[i]) for i in chains]
        p_ctx = [jnp.exp(s_ctx[i] - m[i]) for i in chains]
        den = [jnp.sum(p_loc[i], axis=-1, keepdims=True) + jnp.sum(p_ctx[i], axis=-1, keepdims=True) for i in chains]
        o = [(jnp.dot(p_loc[i].astype(BF16), vwin[i], preferred_element_type=F32)
              + jnp.dot(p_ctx[i].astype(BF16), vcb_scr[...], preferred_element_type=F32)) / den[i] for i in chains]
        for u in range(rows_per_iter):
            o_ref[0, qsl[2 * u], :] = jnp.where(lane_lo, o[2 * u], o[2 * u + 1])
        return carry

    lax.fori_loop(0, NAT_ROWS_PER_BLOCK // rows_per_iter, row_group, 0)


def _nat_bias_table(rpb):
    col = jnp.arange(GRID_W)
    cstart = jnp.clip(col - NAT_COLS // 2, 0, GRID_W - NAT_COLS)
    delta = jnp.arange(NAT_ROWS)
    wrow = jnp.arange(NAT_ROWS)
    ridx = wrow[None, :] - delta[:, None] + (NAT_ROWS - 1)
    cidx = col[None, :] - col[:, None] + (NAT_COLS - 1)
    inwin = (col[None, :] >= cstart[:, None]) & (col[None, :] < cstart[:, None] + NAT_COLS)
    cidx = jnp.clip(cidx, 0, 2 * NAT_COLS - 2)
    tab = rpb[:, ridx[:, None, :, None], cidx[None, :, None, :]]
    tab = jnp.where(inwin[None, None, :, None, :], tab, NAT_MASKED)
    return tab.reshape(rpb.shape[0], NAT_ROWS, GRID_W, NAT_WIN).astype(F32)


def nat_attention(q, k, v, kc, vc, rpb, qn_g, kn_g, interpret=False):
    B, L, D = q.shape
    Lc = kc.shape[1]
    rows = L // GRID_W
    P = D // PAIR
    rpb_blk = NAT_ROWS_PER_BLOCK
    assert rows >= NAT_ROWS and rows % rpb_blk == 0 and L % 512 == 0
    bias = _nat_bias_table(rpb)
    g2 = lambda g: jnp.tile(g, 2).reshape(1, PAIR)
    full = pl.BlockSpec((1, L, PAIR), lambda b, p, i: (b, 0, p))
    cfull = pl.BlockSpec((1, Lc, PAIR), lambda b, p, i: (b, 0, p))
    qblk = pl.BlockSpec((1, rpb_blk * GRID_W, PAIR), lambda b, p, i: (b, i, p))
    gspec = pl.BlockSpec((1, PAIR), lambda b, p, i: (0, 0))
    return pl.pallas_call(
        functools.partial(_nat_body, rows=rows),
        grid=(B, P, rows // rpb_blk),
        in_specs=[qblk, full, full, cfull, cfull,
                  pl.BlockSpec((2, NAT_ROWS, GRID_W, NAT_WIN), lambda b, p, i: (p, 0, 0, 0)), gspec, gspec],
        out_specs=qblk,
        out_shape=jax.ShapeDtypeStruct((B, L, D), F32),
        scratch_shapes=[pltpu.VMEM((L, PAIR), BF16), pltpu.VMEM((L, PAIR), BF16),
                        pltpu.VMEM((Lc, PAIR), BF16), pltpu.VMEM((Lc, PAIR), BF16)],
        compiler_params=pltpu.CompilerParams(dimension_semantics=("parallel", "parallel", "arbitrary"),
                                             vmem_limit_bytes=48 * 1024 * 1024),
        name="nat_attention",
        interpret=interpret,
    )(q, k, v, kc, vc, bias, g2(qn_g), g2(kn_g))


GLA_SUB = 16
GLA_BLOCK = 256
GLA_KPAIR = 2 * GLA_DK
GLA_VPAIR = 2 * GLA_DV


def _gla_body(q_ref, k_ref, v_ref, g_ref, s0_ref, o_ref, sT_ref, st_scr, *, reverse):
    C = GLA_SUB
    nsub = GLA_BLOCK // C
    npair = st_scr.shape[0]
    blk = pl.program_id(1)

    @pl.when(blk == 0)
    def _():
        st_scr[...] = s0_ref[0]

    ti = lax.broadcasted_iota(jnp.int32, (C, C), 0)
    si = lax.broadcasted_iota(jnp.int32, (C, C), 1)
    tri = ((si >= ti) if reverse else (si <= ti)).astype(F32)
    lane_lo = lax.broadcasted_iota(jnp.int32, (C, GLA_KPAIR), 1) < GLA_DK
    row_id = lax.broadcasted_iota(jnp.int32, (C, GLA_KPAIR), 0)
    vrow = lax.broadcasted_iota(jnp.int32, (2 * C, GLA_VPAIR), 0)
    vcol = lax.broadcasted_iota(jnp.int32, (2 * C, GLA_VPAIR), 1)
    v_same_head = (vrow // C) == (vcol // GLA_DV)
    srow = lax.broadcasted_iota(jnp.int32, (GLA_VPAIR, GLA_KPAIR), 0)
    scol = lax.broadcasted_iota(jnp.int32, (GLA_VPAIR, GLA_KPAIR), 1)
    s_same_head = (srow // GLA_DV) == (scol // GLA_DK)

    pairs = range(npair)
    klanes = [slice(p * GLA_KPAIR, (p + 1) * GLA_KPAIR) for p in pairs]
    vlanes = [slice(p * GLA_VPAIR, (p + 1) * GLA_VPAIR) for p in pairs]

    def sub_chunk(i, carry):
        ci = (nsub - 1 - i) if reverse else i
        sl = pl.ds(pl.multiple_of(ci * C, C), C)
        q = [q_ref[0, sl, ks] for ks in klanes]
        k = [k_ref[0, sl, ks] for ks in klanes]
        g = [g_ref[0, sl, ks] for ks in klanes]
        v = [v_ref[0, sl, vs] for vs in vlanes]
        st = [st_scr[p] for p in pairs]
        b = [jnp.dot(tri, gg, preferred_element_type=F32, precision=lax.Precision.HIGHEST) for gg in g]
        b_end = [jnp.sum(gg, axis=0, keepdims=True) for gg in g]
        a_lo = [jnp.zeros((C, C), F32) for _ in pairs]
        a_hi = [jnp.zeros((C, C), F32) for _ in pairs]
        for j in range(C):
            seen = (row_id <= j) if reverse else (row_id >= j)
            for p in pairs:
                decay = jnp.exp(jnp.where(seen, b[p] - b[p][j:j + 1, :], 0.0))
                f = jnp.where(seen, q[p] * k[p][j:j + 1, :] * decay, 0.0)
                r_lo = jnp.sum(jnp.where(lane_lo, f, 0.0), axis=-1, keepdims=True)
                r_hi = jnp.sum(jnp.where(lane_lo, 0.0, f), axis=-1, keepdims=True)
                a_lo[p] = jnp.where(si == j, r_lo, a_lo[p])
                a_hi[p] = jnp.where(si == j, r_hi, a_hi[p])
        v_bd = [jnp.where(v_same_head, jnp.concatenate([vv, vv], axis=0), 0.0) for vv in v]
        o_in = [_mm(jnp.concatenate([a_lo[p], a_hi[p]], axis=1), v_bd[p]) for p in pairs]
        o_st = [_mm_nt(q[p] * jnp.exp(b[p]), st[p]) for p in pairs]
        kv = [_mm_tn(v[p], k[p] * jnp.exp(b_end[p] - b[p])) for p in pairs]
        for p in pairs:
            o_ref[0, sl, vlanes[p]] = o_in[p] + o_st[p]
            st_scr[p] = st[p] * jnp.exp(b_end[p]) + jnp.where(s_same_head, kv[p], 0.0)
        return carry

    lax.fori_loop(0, nsub, sub_chunk, 0)

    @pl.when(blk == pl.num_programs(1) - 1)
    def _():
        sT_ref[0] = st_scr[...]


def gla_scan(q, k, v, logg, st0, *, reverse, interpret=False):
    B, L, Dk = q.shape
    Dv = v.shape[-1]
    T = GLA_BLOCK
    assert L % T == 0 and Dk % GLA_KPAIR == 0
    n, P = L // T, Dk // GLA_KPAIR
    bidx = (lambda c: n - 1 - c) if reverse else (lambda c: c)
    kspec = pl.BlockSpec((1, T, Dk), lambda b, c: (b, bidx(c), 0))
    vspec = pl.BlockSpec((1, T, Dv), lambda b, c: (b, bidx(c), 0))
    sspec = pl.BlockSpec((1, P, GLA_VPAIR, GLA_KPAIR), lambda b, c: (b, 0, 0, 0))
    return pl.pallas_call(
        functools.partial(_gla_body, reverse=reverse),
        grid=(B, n),
        in_specs=[kspec, kspec, vspec, kspec, sspec],
        out_specs=[vspec, sspec],
        out_shape=[jax.ShapeDtypeStruct((B, L, Dv), F32),
                   jax.ShapeDtypeStruct((B, P, GLA_VPAIR, GLA_KPAIR), F32)],
        scratch_shapes=[pltpu.VMEM((P, GLA_VPAIR, GLA_KPAIR), F32)],
        compiler_params=pltpu.CompilerParams(dimension_semantics=("parallel", "arbitrary")),
        name="gla_scan_rev" if reverse else "gla_scan_fwd",
        interpret=interpret,
    )(q, k, v, logg, st0)


def _split(p, sizes):
    return jnp.split(p, np.cumsum(sizes)[:-1].tolist(), axis=-1)


def rmsnorm(x, g, eps=NORM_EPS):
    xf = x.astype(F32)
    y = xf * lax.rsqrt(jnp.mean(xf * xf, axis=-1, keepdims=True) + eps)
    return (y * g.astype(F32)).astype(x.dtype)


def modulate(x, g, shift, scale):
    return rmsnorm(x, g) * (1 + scale) + shift


def conv3(u, w):
    up = jnp.pad(u, ((0, 0), (1, 1), (0, 0)))
    return up[:, :-2] * w[0] + up[:, 1:-1] * w[1] + up[:, 2:] * w[2]


def rope_2d(x):
    L, d = x.shape[1], x.shape[-1]
    half, nf = d // 2, d // 4
    t = jnp.arange(L)
    inv = ROPE_BASE ** (-jnp.arange(nf, dtype=F32) / nf)

    def rot(u, pos):
        ang = pos.astype(F32)[:, None] * inv[None, :]
        cos, sin = jnp.cos(ang)[None, :, None, :], jnp.sin(ang)[None, :, None, :]
        u1, u2 = u[..., :nf].astype(F32), u[..., nf:].astype(F32)
        return jnp.concatenate([u1 * cos - u2 * sin, u1 * sin + u2 * cos], axis=-1)

    return jnp.concatenate([rot(x[..., :half], t // GRID_W), rot(x[..., half:], t % GRID_W)], axis=-1).astype(x.dtype)


def ec_moe(h, router, w1, w3, w2):
    B, T, D = h.shape
    cap = EC_CAPACITY_FACTOR * T // N_EXPERTS
    aff = jax.nn.softmax((h @ router).astype(F32), axis=-1)
    gate, idx = lax.top_k(jnp.swapaxes(aff, 1, 2), cap)
    xin = jax.vmap(lambda hb, ib: hb[ib])(h, idx)
    hid = jax.nn.silu(jnp.einsum('becd,edf->becf', xin, w1)) * jnp.einsum('becd,edf->becf', xin, w3)
    y = jnp.einsum('becf,efd->becd', hid, w2) * gate[..., None].astype(h.dtype)
    out = jax.vmap(lambda ib, yb: jnp.zeros((T, D), yb.dtype).at[ib.reshape(-1)].add(yb.reshape(-1, D)))(idx, y)
    return out.astype(h.dtype)


def rwkv_readout(y, r, k, v, xg, r_k, g_up, ln_g, ln_b):
    B, L = r.shape[:2]
    hs = lambda t: t.reshape(B, L, RWKV_HEADS, RWKV_HEAD)
    yf = hs(y.astype(F32))
    mu = jnp.mean(yf, axis=-1, keepdims=True)
    var = jnp.mean(jnp.square(yf - mu), axis=-1, keepdims=True)
    yn = ((yf - mu) * lax.rsqrt(var + RWKV_LN_EPS)).reshape(B, L, D_RWKV) * ln_g + ln_b
    bonus = (jnp.sum(hs(r) * hs(k) * r_k, axis=-1, keepdims=True) * hs(v)).reshape(B, L, D_RWKV)
    g = jax.nn.sigmoid(xg) @ g_up
    return ((yn + bonus) * g).astype(r.dtype)


def even_mixer(h_lat, h_ctx, w_in, conv_w, k_k, k_a, r_k, w0, w_up, a0, a_up, g_up, ln_g, ln_b, need_ctx):
    B = h_lat.shape[0]
    p_lat = _split(h_lat @ w_in, EVEN_SPLIT)
    p_ctx = _split(h_ctx @ w_in, EVEN_SPLIT)

    def conv_branch(u, gate_b, gate_c):
        return gate_b * conv3(gate_c * u, conv_w)

    def unit_key(k):
        Bq, L = k.shape[:2]
        kkf = (k * k_k).astype(F32).reshape(Bq, L, RWKV_HEADS, RWKV_HEAD)
        return (kkf * lax.rsqrt(jnp.sum(kkf * kkf, axis=-1, keepdims=True) + 1e-12)).reshape(Bq, L, D_RWKV)

    def dir_gates(xw, xa, d):
        w_raw = (w0[d] + jnp.tanh(xw) @ w_up[d]).astype(F32)
        return -jnp.exp(-jax.nn.softplus(-w_raw) - 0.5), jax.nn.sigmoid(a0[d] + xa @ a_up[d])

    kk_ctx, kk_lat = unit_key(p_ctx[4]), unit_key(p_lat[4])
    st0 = jnp.zeros((B, RWKV_HEADS // 2, PAIR, PAIR), F32)
    y_lat, y_ctx = 0.0, 0.0
    for d, rev in ((0, False), (1, True)):
        yc, st_c = rwkv_chunked(*p_ctx[3:6], kk_ctx, *dir_gates(*p_ctx[6:8], d), k_a, st0, reverse=rev)
        yl, _ = rwkv_chunked(*p_lat[3:6], kk_lat, *dir_gates(*p_lat[6:8], d), k_a, st_c, reverse=rev)
        y_lat = y_lat + yl
        if need_ctx:
            y_ctx = y_ctx + yc
    cat_lat = jnp.concatenate([conv_branch(*p_lat[0:3]),
                               rwkv_readout(y_lat, *p_lat[3:6], p_lat[8], r_k, g_up, ln_g, ln_b)], axis=-1)
    cat_ctx = None
    if need_ctx:
        cat_ctx = jnp.concatenate([conv_branch(*p_ctx[0:3]),
                                   rwkv_readout(y_ctx, *p_ctx[3:6], p_ctx[8], r_k, g_up, ln_g, ln_b)], axis=-1)
    return cat_lat, cat_ctx


def ctx_attention(q, k, v):
    s = jnp.einsum('bhqd,bhkd->bhqk', q, k).astype(F32) * (q.shape[-1] ** -0.5)
    p = jax.nn.softmax(s, axis=-1).astype(v.dtype)
    return jnp.einsum('bhqk,bhkd->bhqd', p, v)


def gla_log_gate(ga, a_up_d, a_b_d):
    B, L = ga.shape[:2]
    lg = jax.nn.log_sigmoid((ga @ a_up_d + a_b_d).astype(F32)) / GLA_GATE_TEMP
    return lg.reshape(B, L, GLA_HEADS, GLA_DK)


def gla_readout(o, gr, ln_g):
    B, L = gr.shape[:2]
    return (rmsnorm(o, ln_g).reshape(B, L, D_GLA_V) * jax.nn.silu(gr)).astype(gr.dtype)


def odd_mixer(h_lat, h_ctx, w_in, qn_g, kn_g, rpb, a_up, a_b, gla_ln_g, need_ctx):
    B, L = h_lat.shape[:2]
    Lc = h_ctx.shape[1]
    nq, nk, nv, gq, gk, gv, gr, ga = _split(h_lat @ w_in, ODD_SPLIT)
    cnq, cnk, cnv, cgq, cgk, cgv, cgr, cga = _split(h_ctx @ w_in, ODD_SPLIT)

    def nat_heads(t, g=None):
        t = t.reshape(t.shape[0], t.shape[1], NAT_HEADS, NAT_HEAD)
        if g is not None:
            t = rmsnorm(t, g)
        return jnp.swapaxes(t, 1, 2)

    nat_lat = nat_attention(nq, nk, nv, cnk, cnv, rpb, qn_g, kn_g)

    gh = lambda t, d: t.reshape(t.shape[0], t.shape[1], GLA_HEADS, d)
    qscale = GLA_DK ** -0.5
    q = (rope_2d(gh(gq, GLA_DK)) * qscale).reshape(B, L, D_GLA_K)
    k = rope_2d(gh(gk, GLA_DK)).reshape(B, L, D_GLA_K)
    qc = cgq * qscale
    st0 = jnp.zeros((B, GLA_HEADS // 2, GLA_VPAIR, GLA_KPAIR), F32)
    o_lat, o_ctx = 0.0, 0.0
    for d in range(2):
        lg_c = gla_log_gate(cga, a_up[d], a_b[d]).reshape(B, Lc, D_GLA_K)
        lg_l = gla_log_gate(ga, a_up[d], a_b[d]).reshape(B, L, D_GLA_K)
        oc, st_c = gla_scan(qc, cgk, cgv, lg_c, st0, reverse=(d == 1))
        ol, _ = gla_scan(q, k, gv, lg_l, st_c, reverse=(d == 1))
        o_lat = o_lat + ol
        if need_ctx:
            o_ctx = o_ctx + oc
    o_lat = gh(o_lat, GLA_DV)
    if need_ctx:
        o_ctx = gh(o_ctx, GLA_DV)
    cat_lat = jnp.concatenate([nat_lat, gla_readout(o_lat, gr, gla_ln_g)], axis=-1)
    cat_ctx = None
    if need_ctx:
        kc, vc = nat_heads(cnk, kn_g), nat_heads(cnv)
        nat_ctx = jnp.swapaxes(ctx_attention(nat_heads(cnq, qn_g), kc, vc), 1, 2).reshape(B, Lc, D_NAT)
        cat_ctx = jnp.concatenate([nat_ctx, gla_readout(o_ctx, cgr, gla_ln_g)], axis=-1)
    return cat_lat, cat_ctx


def kernel(x, c, ctx, c_ctx, ada_w, ada_b, norm1_g, norm2_g, ev_w_in, ev_w_out, conv_w, rw_k_k, rw_k_a, rw_r_k, rw_w0, rw_w_up, rw_a0, rw_a_up, rw_g_up, rw_ln_g, rw_ln_b, od_w_in, od_w_out, nat_qn_g, nat_kn_g, nat_rpb, gla_a_up, gla_a_b, gla_ln_g, moe_router, moe_w1, moe_w3, moe_w2):
    depth = ada_w.shape[0]
    ctx_s = ctx
    silu_c = jax.nn.silu(c)
    silu_cc = jax.nn.silu(c_ctx)
    for l in range(depth):
        last = l == depth - 1
        j = l // 2
        sh1, sc1, gt1, sh2, sc2, gt2 = _split((silu_c @ ada_w[l] + ada_b[l])[:, None, :], [D_MODEL] * 6)
        csh1, csc1, cgt1, csh2, csc2, cgt2 = _split(silu_cc @ ada_w[l] + ada_b[l], [D_MODEL] * 6)
        h_lat = modulate(x, norm1_g[l], sh1, sc1)
        h_ctx = modulate(ctx_s, norm1_g[l], csh1, csc1)
        if l % 2 == 0:
            cat_lat, cat_ctx = even_mixer(h_lat, h_ctx, ev_w_in[j], conv_w[j], rw_k_k[j], rw_k_a[j], rw_r_k[j],
                                          rw_w0[j], rw_w_up[j], rw_a0[j], rw_a_up[j], rw_g_up[j], rw_ln_g[j],
                                          rw_ln_b[j], not last)
            w_out = ev_w_out[j]
        else:
            cat_lat, cat_ctx = odd_mixer(h_lat, h_ctx, od_w_in[j], nat_qn_g[j], nat_kn_g[j], nat_rpb[j],
                                         gla_a_up[j], gla_a_b[j], gla_ln_g[j], not last)
            w_out = od_w_out[j]
        x = out_proj_residual(cat_lat, w_out, x, gt1)
        x = x + gt2 * ec_moe(modulate(x, norm2_g[l], sh2, sc2), moe_router[l], moe_w1[l], moe_w3[l], moe_w2[l])
        if not last:
            ctx_s = out_proj_residual(cat_ctx, w_out, ctx_s, cgt1)
            ctx_s = ctx_s + cgt2 * ec_moe(modulate(ctx_s, norm2_g[l], csh2, csc2),
                                          moe_router[l], moe_w1[l], moe_w3[l], moe_w2[l])
    return x
```

```python
import functools

import jax
import jax.numpy as jnp
import numpy as np
from jax import lax
from jax.experimental import pallas as pl
from jax.experimental.pallas import tpu as pltpu

D_MODEL = 1024
GRID_W = 64
NORM_EPS = 1e-6
F32 = jnp.float32
BF16 = jnp.bfloat16

D_CONV = 512
D_RWKV = 512
RWKV_HEAD = 64
RWKV_HEADS = D_RWKV // RWKV_HEAD
RWKV_DECAY_RANK = 64
RWKV_ICLR_RANK = 64
RWKV_GATE_RANK = 128
RWKV_LN_EPS = 64e-5
D_NAT = 512
NAT_HEAD = 64
NAT_HEADS = D_NAT // NAT_HEAD
NAT_ROWS = 8
NAT_COLS = 16
GLA_HEADS = 4
GLA_DK = 64
GLA_DV = 128
D_GLA_K = GLA_HEADS * GLA_DK
D_GLA_V = GLA_HEADS * GLA_DV
GLA_GATE_RANK = 16
GLA_GATE_TEMP = 16.0
GLA_CHUNK = 64
ROPE_BASE = 10000.0
N_EXPERTS = 16
EC_CAPACITY_FACTOR = 2

EVEN_SPLIT = [D_CONV, D_CONV, D_CONV, D_RWKV, D_RWKV, D_RWKV, RWKV_DECAY_RANK, RWKV_ICLR_RANK, RWKV_GATE_RANK]
ODD_SPLIT = [D_NAT, D_NAT, D_NAT, D_GLA_K, D_GLA_K, D_GLA_V, D_GLA_V, GLA_GATE_RANK]


def _out_proj_body(a_ref, w_ref, x_ref, g_ref, o_ref):
    acc = jnp.dot(a_ref[0].astype(BF16), w_ref[...], preferred_element_type=F32)
    o_ref[0] = x_ref[0] + g_ref[0] * acc


def out_proj_residual(a, w, x, gate, block_rows=512):
    B, L, K = a.shape
    N = w.shape[1]
    tm = min(block_rows, L)
    assert L % tm == 0
    gate = jnp.broadcast_to(gate.reshape(-1, 1, N), (B, 1, N))
    return pl.pallas_call(
        _out_proj_body,
        grid=(B, L // tm),
        in_specs=[
            pl.BlockSpec((1, tm, K), lambda b, i: (b, i, 0)),
            pl.BlockSpec((K, N), lambda b, i: (0, 0)),
            pl.BlockSpec((1, tm, N), lambda b, i: (b, i, 0)),
            pl.BlockSpec((1, 1, N), lambda b, i: (b, 0, 0)),
        ],
        out_specs=pl.BlockSpec((1, tm, N), lambda b, i: (b, i, 0)),
        out_shape=jax.ShapeDtypeStruct((B, L, N), F32),
        compiler_params=pltpu.CompilerParams(dimension_semantics=("parallel", "parallel")),
        name="out_proj_residual",
    )(a, w.astype(BF16), x, gate)


RWKV_CHUNK = 64
PAIR = 2 * RWKV_HEAD


def _mm(a, b):
    return jnp.dot(a.astype(BF16), b.astype(BF16), preferred_element_type=F32)


def _mm_nt(a, b):
    return lax.dot_general(a.astype(BF16), b.astype(BF16), (((1,), (1,)), ((), ())), preferred_element_type=F32)


def _mm_tn(a, b):
    return lax.dot_general(a.astype(BF16), b.astype(BF16), (((0,), (0,)), ((), ())), preferred_element_type=F32)


def _rwkv_chunk_pairs(ins, sts, *, reverse):
    C = RWKV_CHUNK
    row = lax.broadcasted_iota(jnp.int32, (PAIR, PAIR), 0)
    col = lax.broadcasted_iota(jnp.int32, (PAIR, PAIR), 1)
    same_head = (row // RWKV_HEAD) == (col // RWKV_HEAD)
    t_i, s_i = row % C, col % C
    before = (s_i > t_i) if reverse else (s_i < t_i)
    upto = before | (s_i == t_i)

    ct = lax.broadcasted_iota(jnp.int32, (C, C), 0)
    cs = lax.broadcasted_iota(jnp.int32, (C, C), 1)
    tri = ((cs >= ct) if reverse else (cs <= ct)).astype(F32)
    eye = jnp.where(row == col, 1.0, 0.0)

    def expand(x):
        return jnp.where(same_head, jnp.concatenate([x, x], axis=0), 0.0)

    def prepare(r, k, v, kk, lw, a, ka):
        keff = k * (1.0 + (a - 1.0) * ka)
        b = kk * a
        cum_in = jnp.dot(tri, lw, preferred_element_type=F32, precision=lax.Precision.HIGHEST)
        cum_ex = cum_in - lw
        tot = jnp.sum(lw, axis=0, keepdims=True)
        e_neg = jnp.exp(-cum_in)
        e_rem = jnp.exp(tot - cum_in)
        return dict(at2=expand(-kk * jnp.exp(cum_ex)), rt2=expand(r * jnp.exp(cum_in)),
                    bh2=expand(b * e_neg), kh2=expand(keff * e_neg), bp2=expand(b * e_rem),
                    kp2=expand(keff * e_rem), v2=expand(v), dtot=eye * jnp.exp(tot))

    ps = [prepare(*args) for args in ins]
    pps = [_mm_nt(jnp.concatenate([p["at2"], p["rt2"]], axis=0), jnp.concatenate([p["bh2"], p["kh2"]], axis=0))
           for p in ps]
    a_ab = [jnp.where(before, pp[:PAIR, :PAIR], 0.0) for pp in pps]
    a_ak = [jnp.where(before, pp[:PAIR, PAIR:], 0.0) for pp in pps]
    a_rb = [jnp.where(upto, pp[PAIR:, :PAIR], 0.0) for pp in pps]
    a_rk = [jnp.where(upto, pp[PAIR:, PAIR:], 0.0) for pp in pps]

    tinv = [eye + n for n in a_ab]
    npow = a_ab
    for _ in range(int(np.log2(C)) - 1):
        npow = [_mm(n, n) for n in npow]
        tinv = [t + _mm(t, n) for t, n in zip(tinv, npow)]

    av = [_mm(jnp.concatenate([ak, rk], axis=0), p["v2"]) for ak, rk, p in zip(a_ak, a_rk, ps)]
    x = [_mm(t, jnp.concatenate([p["at2"], w[:PAIR]], axis=1)) for t, p, w in zip(tinv, ps, av)]
    z = [_mm(rb, xx) for rb, xx in zip(a_rb, x)]
    bx = [_mm_tn(p["bp2"], xx) for p, xx in zip(ps, x)]
    kv = [_mm_tn(p["kp2"], p["v2"]) for p in ps]
    y2 = [_mm(p["rt2"] + zz[:, :PAIR], st) + zz[:, PAIR:] + w[PAIR:] for p, zz, st, w in zip(ps, z, sts, av)]
    st_new = [_mm(p["dtot"] + b[:, :PAIR], st) + b[:, PAIR:] + g for p, b, st, g in zip(ps, bx, sts, kv)]
    return [y[:C] + y[C:] for y in y2], st_new


def _rwkv_chunk_body(r_ref, k_ref, v_ref, kk_ref, lw_ref, a_ref, ka_ref, s0_ref, y_ref, sT_ref, st_scr, *, reverse):
    c = pl.program_id(1)

    @pl.when(c == 0)
    def _():
        st_scr[...] = s0_ref[0]

    npair = st_scr.shape[0]
    lanes = [slice(p * PAIR, (p + 1) * PAIR) for p in range(npair)]
    ins = [(r_ref[0, :, sl], k_ref[0, :, sl], v_ref[0, :, sl], kk_ref[0, :, sl], lw_ref[0, :, sl], a_ref[0, :, sl],
            ka_ref[:, sl]) for sl in lanes]
    ys, sts = _rwkv_chunk_pairs(ins, [st_scr[p] for p in range(npair)], reverse=reverse)
    for p, sl in enumerate(lanes):
        y_ref[0, :, sl] = ys[p]
        st_scr[p] = sts[p]

    @pl.when(c == pl.num_programs(1) - 1)
    def _():
        sT_ref[0] = st_scr[...]


def rwkv_state_pack(S):
    B, H = S.shape[:2]
    St = jnp.swapaxes(S, -1, -2).reshape(B, H // 2, 2, RWKV_HEAD, RWKV_HEAD)
    z = jnp.zeros_like(St[:, :, 0])
    top = jnp.concatenate([St[:, :, 0], z], axis=-1)
    bot = jnp.concatenate([z, St[:, :, 1]], axis=-1)
    return jnp.concatenate([top, bot], axis=-2)


def rwkv_state_unpack(St):
    B, P = St.shape[:2]
    h0 = St[:, :, :RWKV_HEAD, :RWKV_HEAD]
    h1 = St[:, :, RWKV_HEAD:, RWKV_HEAD:]
    return jnp.swapaxes(jnp.stack([h0, h1], axis=2).reshape(B, 2 * P, RWKV_HEAD, RWKV_HEAD), -1, -2)


def rwkv_chunked(r, k, v, kk, lw, a, k_a, st0, *, reverse, interpret=False):
    B, L, D = r.shape
    C = RWKV_CHUNK
    assert L % C == 0 and D % PAIR == 0
    n, P = L // C, D // PAIR
    cidx = (lambda c: n - 1 - c) if reverse else (lambda c: c)
    seq = pl.BlockSpec((1, C, D), lambda b, c: (b, cidx(c), 0))
    state = pl.BlockSpec((1, P, PAIR, PAIR), lambda b, c: (b, 0, 0, 0))
    return pl.pallas_call(
        functools.partial(_rwkv_chunk_body, reverse=reverse),
        grid=(B, n),
        in_specs=[seq] * 6 + [pl.BlockSpec((1, D), lambda b, c: (0, 0)), state],
        out_specs=[seq, state],
        out_shape=[jax.ShapeDtypeStruct((B, L, D), F32), jax.ShapeDtypeStruct((B, P, PAIR, PAIR), F32)],
        scratch_shapes=[pltpu.VMEM((P, PAIR, PAIR), F32)],
        compiler_params=pltpu.CompilerParams(dimension_semantics=("parallel", "arbitrary")),
        name="rwkv_chunked_rev" if reverse else "rwkv_chunked_fwd",
        interpret=interpret,
    )(r, k, v, kk, lw, a, k_a.reshape(1, D), st0)


NAT_WIN = NAT_ROWS * GRID_W
NAT_ROWS_PER_BLOCK = 16
NAT_MASKED = -1e30


def _pair_rmsnorm(x, g, lane_lo):
    sq = x * x
    s_lo = jnp.sum(jnp.where(lane_lo, sq, 0.0), axis=-1, keepdims=True)
    s_hi = jnp.sum(sq, axis=-1, keepdims=True) - s_lo
    ms = jnp.where(lane_lo, s_lo, s_hi) * (1.0 / NAT_HEAD)
    return x * lax.rsqrt(ms + NORM_EPS) * g


def _nat_body(q_ref, k_ref, v_ref, kc_ref, vc_ref, bias_ref, qg_ref, kg_ref, o_ref, kn_scr, vb_scr, kcn_scr, vcb_scr,
              *, rows):
    rb = pl.program_id(2)
    L = k_ref.shape[1]
    norm_rows = 512
    lane_lo_n = lax.broadcasted_iota(jnp.int32, (norm_rows, PAIR), 1) < NAT_HEAD

    @pl.when(rb == 0)
    def _():
        def norm_block(i, carry):
            sl = pl.ds(pl.multiple_of(i * norm_rows, norm_rows), norm_rows)
            kn_scr[sl, :] = _pair_rmsnorm(k_ref[0, sl, :], kg_ref[...], lane_lo_n).astype(BF16)
            vb_scr[sl, :] = v_ref[0, sl, :].astype(BF16)
            return carry
        lax.fori_loop(0, L // norm_rows, norm_block, 0)
        lane_lo_c = lax.broadcasted_iota(jnp.int32, kc_ref.shape[1:], 1) < NAT_HEAD
        kcn_scr[...] = _pair_rmsnorm(kc_ref[0], kg_ref[...], lane_lo_c).astype(BF16)
        vcb_scr[...] = vc_ref[0].astype(BF16)

    lane_lo = lax.broadcasted_iota(jnp.int32, (GRID_W, PAIR), 1) < NAT_HEAD
    scale = NAT_HEAD ** -0.5

    rows_per_iter = 2
    nt = (((1,), (1,)), ((), ()))

    def row_group(it, carry):
        qsl, kwin, vwin, qh, bias = [], [], [], [], []
        for u in range(rows_per_iter):
            j = it * rows_per_iter + u
            r = rb * NAT_ROWS_PER_BLOCK + j
            rs = jnp.clip(r - NAT_ROWS // 2, 0, rows - NAT_ROWS)
            sl = pl.ds(pl.multiple_of(j * GRID_W, GRID_W), GRID_W)
            wsl = pl.ds(pl.multiple_of(rs * GRID_W, GRID_W), NAT_WIN)
            qn = _pair_rmsnorm(q_ref[0, sl, :], qg_ref[...], lane_lo) * scale
            for h in range(2):
                qsl.append(sl)
                kwin.append(kn_scr[wsl, :])
                vwin.append(vb_scr[wsl, :])
                qh.append(jnp.where(lane_lo if h == 0 else ~lane_lo, qn, 0.0).astype(BF16))
                bias.append(bias_ref[h, r - rs])
        chains = range(2 * rows_per_iter)
        s_loc = [lax.dot_general(qh[i], kwin[i], nt, preferred_element_type=F32) + bias[i] for i in chains]
        s_ctx = [lax.dot_general(qh[i], kcn_scr[...], nt, preferred_element_type=F32) for i in chains]
        m = [jnp.maximum(jnp.max(s_loc[i], axis=-1, keepdims=True), jnp.max(s_ctx[i], axis=-1, keepdims=True))
             for i in chains]
        p_loc = [jnp.exp(s_loc[i] - m[i]) for i in chains]
        p_ctx = [jnp.exp(s_ctx[i] - m[i]) for i in chains]
        den = [jnp.sum(p_loc[i], axis=-1, keepdims=True) + jnp.sum(p_ctx[i], axis=-1, keepdims=True) for i in chains]
        o = [(jnp.dot(p_loc[i].astype(BF16), vwin[i], preferred_element_type=F32)
              + jnp.dot(p_ctx[i].astype(BF16), vcb_scr[...], preferred_element_type=F32)) / den[i] for i in chains]
        for u in range(rows_per_iter):
            o_ref[0, qsl[2 * u], :] = jnp.where(lane_lo, o[2 * u], o[2 * u + 1])
        return carry

    lax.fori_loop(0, NAT_ROWS_PER_BLOCK // rows_per_iter, row_group, 0)


def _nat_bias_table(rpb):
    col = jnp.arange(GRID_W)
    cstart = jnp.clip(col - NAT_COLS // 2, 0, GRID_W - NAT_COLS)
    delta = jnp.arange(NAT_ROWS)
    wrow = jnp.arange(NAT_ROWS)
    ridx = wrow[None, :] - delta[:, None] + (NAT_ROWS - 1)
    cidx = col[None, :] - col[:, None] + (NAT_COLS - 1)
    inwin = (col[None, :] >= cstart[:, None]) & (col[None, :] < cstart[:, None] + NAT_COLS)
    rsel = (ridx[:, :, None] == jnp.arange(2 * NAT_ROWS - 1)).astype(F32)
    csel = ((cidx[:, :, None] == jnp.arange(2 * NAT_COLS - 1)) & inwin[:, :, None]).astype(F32)
    tab = jnp.einsum('hab,dia,ckb->hdcik', rpb.astype(F32), rsel, csel, precision=lax.Precision.HIGHEST)
    tab = jnp.where(inwin[None, None, :, None, :], tab, NAT_MASKED)
    return tab.reshape(rpb.shape[0], NAT_ROWS, GRID_W, NAT_WIN)


def nat_attention(q, k, v, kc, vc, rpb, qn_g, kn_g, interpret=False):
    B, L, D = q.shape
    Lc = kc.shape[1]
    rows = L // GRID_W
    P = D // PAIR
    rpb_blk = NAT_ROWS_PER_BLOCK
    assert rows >= NAT_ROWS and rows % rpb_blk == 0 and L % 512 == 0
    bias = _nat_bias_table(rpb)
    g2 = lambda g: jnp.tile(g, 2).reshape(1, PAIR)
    full = pl.BlockSpec((1, L, PAIR), lambda b, p, i: (b, 0, p))
    cfull = pl.BlockSpec((1, Lc, PAIR), lambda b, p, i: (b, 0, p))
    qblk = pl.BlockSpec((1, rpb_blk * GRID_W, PAIR), lambda b, p, i: (b, i, p))
    gspec = pl.BlockSpec((1, PAIR), lambda b, p, i: (0, 0))
    return pl.pallas_call(
        functools.partial(_nat_body, rows=rows),
        grid=(B, P, rows // rpb_blk),
        in_specs=[qblk, full, full, cfull, cfull,
                  pl.BlockSpec((2, NAT_ROWS, GRID_W, NAT_WIN), lambda b, p, i: (p, 0, 0, 0)), gspec, gspec],
        out_specs=qblk,
        out_shape=jax.ShapeDtypeStruct((B, L, D), F32),
        scratch_shapes=[pltpu.VMEM((L, PAIR), BF16), pltpu.VMEM((L, PAIR), BF16),
                        pltpu.VMEM((Lc, PAIR), BF16), pltpu.VMEM((Lc, PAIR), BF16)],
        compiler_params=pltpu.CompilerParams(dimension_semantics=("parallel", "parallel", "arbitrary"),
                                             vmem_limit_bytes=48 * 1024 * 1024),
        name="nat_attention",
        interpret=interpret,
    )(q, k, v, kc, vc, bias, g2(qn_g), g2(kn_g))


GLA_SUB = 16
GLA_BLOCK = 256
GLA_KPAIR = 2 * GLA_DK
GLA_VPAIR = 2 * GLA_DV


def _gla_body(q_ref, k_ref, v_ref, g_ref, s0_ref, o_ref, sT_ref, st_scr, *, reverse):
    C = GLA_SUB
    nsub = GLA_BLOCK // C
    npair = st_scr.shape[0]
    blk = pl.program_id(1)

    @pl.when(blk == 0)
    def _():
        st_scr[...] = s0_ref[0]

    ti = lax.broadcasted_iota(jnp.int32, (C, C), 0)
    si = lax.broadcasted_iota(jnp.int32, (C, C), 1)
    tri = ((si >= ti) if reverse else (si <= ti)).astype(F32)
    lane_lo = lax.broadcasted_iota(jnp.int32, (C, GLA_KPAIR), 1) < GLA_DK
    row_id = lax.broadcasted_iota(jnp.int32, (C, GLA_KPAIR), 0)
    vrow = lax.broadcasted_iota(jnp.int32, (2 * C, GLA_VPAIR), 0)
    vcol = lax.broadcasted_iota(jnp.int32, (2 * C, GLA_VPAIR), 1)
    v_same_head = (vrow // C) == (vcol // GLA_DV)
    srow = lax.broadcasted_iota(jnp.int32, (GLA_VPAIR, GLA_KPAIR), 0)
    scol = lax.broadcasted_iota(jnp.int32, (GLA_VPAIR, GLA_KPAIR), 1)
    s_same_head = (srow // GLA_DV) == (scol // GLA_DK)

    pairs = range(npair)
    klanes = [slice(p * GLA_KPAIR, (p + 1) * GLA_KPAIR) for p in pairs]
    vlanes = [slice(p * GLA_VPAIR, (p + 1) * GLA_VPAIR) for p in pairs]

    def sub_chunk(i, carry):
        ci = (nsub - 1 - i) if reverse else i
        sl = pl.ds(pl.multiple_of(ci * C, C), C)
        q = [q_ref[0, sl, ks] for ks in klanes]
        k = [k_ref[0, sl, ks] for ks in klanes]
        g = [g_ref[0, sl, ks] for ks in klanes]
        v = [v_ref[0, sl, vs] for vs in vlanes]
        st = [st_scr[p] for p in pairs]
        b = [jnp.dot(tri, gg, preferred_element_type=F32, precision=lax.Precision.HIGHEST) for gg in g]
        b_end = [jnp.sum(gg, axis=0, keepdims=True) for gg in g]
        a_lo = [jnp.zeros((C, C), F32) for _ in pairs]
        a_hi = [jnp.zeros((C, C), F32) for _ in pairs]
        for j in range(C):
            seen = (row_id <= j) if reverse else (row_id >= j)
            for p in pairs:
                decay = jnp.exp(jnp.where(seen, b[p] - b[p][j:j + 1, :], 0.0))
                f = jnp.where(seen, q[p] * k[p][j:j + 1, :] * decay, 0.0)
                r_lo = jnp.sum(jnp.where(lane_lo, f, 0.0), axis=-1, keepdims=True)
                r_hi = jnp.sum(jnp.where(lane_lo, 0.0, f), axis=-1, keepdims=True)
                a_lo[p] = jnp.where(si == j, r_lo, a_lo[p])
                a_hi[p] = jnp.where(si == j, r_hi, a_hi[p])
        v_bd = [jnp.where(v_same_head, jnp.concatenate([vv, vv], axis=0), 0.0) for vv in v]
        o_in = [_mm(jnp.concatenate([a_lo[p], a_hi[p]], axis=1), v_bd[p]) for p in pairs]
        o_st = [_mm_nt(q[p] * jnp.exp(b[p]), st[p]) for p in pairs]
        kv = [_mm_tn(v[p], k[p] * jnp.exp(b_end[p] - b[p])) for p in pairs]
        for p in pairs:
            o_ref[0, sl, vlanes[p]] = o_in[p] + o_st[p]
            st_scr[p] = st[p] * jnp.exp(b_end[p]) + jnp.where(s_same_head, kv[p], 0.0)
        return carry

    lax.fori_loop(0, nsub, sub_chunk, 0)

    @pl.when(blk == pl.num_programs(1) - 1)
    def _():
        sT_ref[0] = st_scr[...]


def gla_scan(q, k, v, logg, st0, *, reverse, interpret=False):
    B, L, Dk = q.shape
    Dv = v.shape[-1]
    T = GLA_BLOCK
    assert L % T == 0 and Dk % GLA_KPAIR == 0
    n, P = L // T, Dk // GLA_KPAIR
    bidx = (lambda c: n - 1 - c) if reverse else (lambda c: c)
    kspec = pl.BlockSpec((1, T, Dk), lambda b, c: (b, bidx(c), 0))
    vspec = pl.BlockSpec((1, T, Dv), lambda b, c: (b, bidx(c), 0))
    sspec = pl.BlockSpec((1, P, GLA_VPAIR, GLA_KPAIR), lambda b, c: (b, 0, 0, 0))
    return pl.pallas_call(
        functools.partial(_gla_body, reverse=reverse),
        grid=(B, n),
        in_specs=[kspec, kspec, vspec, kspec, sspec],
        out_specs=[vspec, sspec],
        out_shape=[jax.ShapeDtypeStruct((B, L, Dv), F32),
                   jax.ShapeDtypeStruct((B, P, GLA_VPAIR, GLA_KPAIR), F32)],
        scratch_shapes=[pltpu.VMEM((P, GLA_VPAIR, GLA_KPAIR), F32)],
        compiler_params=pltpu.CompilerParams(dimension_semantics=("parallel", "arbitrary")),
        name="gla_scan_rev" if reverse else "gla_scan_fwd",
        interpret=interpret,
    )(q, k, v, logg, st0)


MOE_F_TILE = 512


def _expert_ffn_body(x_ref, g_ref, w1_ref, w3_ref, w2_ref, o_ref, acc_ref):
    f = pl.program_id(3)

    @pl.when(f == 0)
    def _():
        acc_ref[...] = jnp.zeros_like(acc_ref)

    x = x_ref[0, 0].astype(BF16)
    h1 = jnp.dot(x, w1_ref[0], preferred_element_type=F32)
    h3 = jnp.dot(x, w3_ref[0], preferred_element_type=F32)
    hid = (h1 * jax.nn.sigmoid(h1) * h3).astype(BF16)
    acc_ref[...] += jnp.dot(hid, w2_ref[0], preferred_element_type=F32)

    @pl.when(f == pl.num_programs(3) - 1)
    def _():
        o_ref[0, 0] = acc_ref[...] * g_ref[0, 0]


def expert_ffn(xin, gate, w1, w3, w2, interpret=False):
    B, E, cap, D = xin.shape
    F = w1.shape[-1]
    tm = min(cap, 1024)
    tf = MOE_F_TILE
    assert cap % tm == 0 and F % tf == 0
    return pl.pallas_call(
        _expert_ffn_body,
        grid=(E, B, cap // tm, F // tf),
        in_specs=[
            pl.BlockSpec((1, 1, tm, D), lambda e, b, i, f: (b, e, i, 0)),
            pl.BlockSpec((1, 1, tm, 1), lambda e, b, i, f: (b, e, i, 0)),
            pl.BlockSpec((1, D, tf), lambda e, b, i, f: (e, 0, f)),
            pl.BlockSpec((1, D, tf), lambda e, b, i, f: (e, 0, f)),
            pl.BlockSpec((1, tf, D), lambda e, b, i, f: (e, f, 0)),
        ],
        out_specs=pl.BlockSpec((1, 1, tm, D), lambda e, b, i, f: (b, e, i, 0)),
        out_shape=jax.ShapeDtypeStruct((B, E, cap, D), F32),
        scratch_shapes=[pltpu.VMEM((tm, D), F32)],
        compiler_params=pltpu.CompilerParams(
            dimension_semantics=("parallel", "parallel", "parallel", "arbitrary"),
            vmem_limit_bytes=48 * 1024 * 1024),
        name="expert_ffn",
        interpret=interpret,
    )(xin, gate[..., None], w1.astype(BF16), w3.astype(BF16), w2.astype(BF16))


def _split(p, sizes):
    return jnp.split(p, np.cumsum(sizes)[:-1].tolist(), axis=-1)


def rmsnorm(x, g, eps=NORM_EPS):
    xf = x.astype(F32)
    y = xf * lax.rsqrt(jnp.mean(xf * xf, axis=-1, keepdims=True) + eps)
    return (y * g.astype(F32)).astype(x.dtype)


def modulate(x, g, shift, scale):
    return rmsnorm(x, g) * (1 + scale) + shift


def conv3(u, w):
    up = jnp.pad(u, ((0, 0), (1, 1), (0, 0)))
    return up[:, :-2] * w[0] + up[:, 1:-1] * w[1] + up[:, 2:] * w[2]


def rope_2d(x):
    L, d = x.shape[1], x.shape[-1]
    half, nf = d // 2, d // 4
    t = jnp.arange(L)
    inv = ROPE_BASE ** (-jnp.arange(nf, dtype=F32) / nf)

    def rot(u, pos):
        ang = pos.astype(F32)[:, None] * inv[None, :]
        cos, sin = jnp.cos(ang)[None, :, None, :], jnp.sin(ang)[None, :, None, :]
        u1, u2 = u[..., :nf].astype(F32), u[..., nf:].astype(F32)
        return jnp.concatenate([u1 * cos - u2 * sin, u1 * sin + u2 * cos], axis=-1)

    return jnp.concatenate([rot(x[..., :half], t // GRID_W), rot(x[..., half:], t % GRID_W)], axis=-1).astype(x.dtype)


def ec_moe(h, router, w1, w3, w2):
    B, T, D = h.shape
    cap = EC_CAPACITY_FACTOR * T // N_EXPERTS
    aff = jax.nn.softmax((h @ router).astype(F32), axis=-1)
    gate, idx = lax.top_k(jnp.swapaxes(aff, 1, 2), cap)
    xin = jax.vmap(lambda hb, ib: hb[ib])(h, idx)
    y = expert_ffn(xin, gate, w1, w3, w2)
    out = jax.vmap(lambda ib, yb: jnp.zeros((T, D), yb.dtype).at[ib.reshape(-1)].add(yb.reshape(-1, D)))(idx, y)
    return out.astype(h.dtype)


def rwkv_readout(y, r, k, v, xg, r_k, g_up, ln_g, ln_b):
    B, L = r.shape[:2]
    hs = lambda t: t.reshape(B, L, RWKV_HEADS, RWKV_HEAD)
    yf = hs(y.astype(F32))
    mu = jnp.mean(yf, axis=-1, keepdims=True)
    var = jnp.mean(jnp.square(yf - mu), axis=-1, keepdims=True)
    yn = ((yf - mu) * lax.rsqrt(var + RWKV_LN_EPS)).reshape(B, L, D_RWKV) * ln_g + ln_b
    bonus = (jnp.sum(hs(r) * hs(k) * r_k, axis=-1, keepdims=True) * hs(v)).reshape(B, L, D_RWKV)
    g = jax.nn.sigmoid(xg) @ g_up
    return ((yn + bonus) * g).astype(r.dtype)


def even_mixer(h_lat, h_ctx, w_in, conv_w, k_k, k_a, r_k, w0, w_up, a0, a_up, g_up, ln_g, ln_b, need_ctx):
    B = h_lat.shape[0]
    p_lat = _split(h_lat @ w_in, EVEN_SPLIT)
    p_ctx = _split(h_ctx @ w_in, EVEN_SPLIT)

    def conv_branch(u, gate_b, gate_c):
        return gate_b * conv3(gate_c * u, conv_w)

    def unit_key(k):
        Bq, L = k.shape[:2]
        kkf = (k * k_k).astype(F32).reshape(Bq, L, RWKV_HEADS, RWKV_HEAD)
        return (kkf * lax.rsqrt(jnp.sum(kkf * kkf, axis=-1, keepdims=True) + 1e-12)).reshape(Bq, L, D_RWKV)

    def dir_gates(xw, xa, d):
        w_raw = (w0[d] + jnp.tanh(xw) @ w_up[d]).astype(F32)
        return -jnp.exp(-jax.nn.softplus(-w_raw) - 0.5), jax.nn.sigmoid(a0[d] + xa @ a_up[d])

    kk_ctx, kk_lat = unit_key(p_ctx[4]), unit_key(p_lat[4])
    st0 = jnp.zeros((B, RWKV_HEADS // 2, PAIR, PAIR), F32)
    y_lat, y_ctx = 0.0, 0.0
    for d, rev in ((0, False), (1, True)):
        yc, st_c = rwkv_chunked(*p_ctx[3:6], kk_ctx, *dir_gates(*p_ctx[6:8], d), k_a, st0, reverse=rev)
        yl, _ = rwkv_chunked(*p_lat[3:6], kk_lat, *dir_gates(*p_lat[6:8], d), k_a, st_c, reverse=rev)
        y_lat = y_lat + yl
        if need_ctx:
            y_ctx = y_ctx + yc
    cat_lat = jnp.concatenate([conv_branch(*p_lat[0:3]),
                               rwkv_readout(y_lat, *p_lat[3:6], p_lat[8], r_k, g_up, ln_g, ln_b)], axis=-1)
    cat_ctx = None
    if need_ctx:
        cat_ctx = jnp.concatenate([conv_branch(*p_ctx[0:3]),
                                   rwkv_readout(y_ctx, *p_ctx[3:6], p_ctx[8], r_k, g_up, ln_g, ln_b)], axis=-1)
    return cat_lat, cat_ctx


def ctx_attention(q, k, v):
    s = jnp.einsum('bhqd,bhkd->bhqk', q, k).astype(F32) * (q.shape[-1] ** -0.5)
    p = jax.nn.softmax(s, axis=-1).astype(v.dtype)
    return jnp.einsum('bhqk,bhkd->bhqd', p, v)


def gla_log_gate(ga, a_up_d, a_b_d):
    B, L = ga.shape[:2]
    lg = jax.nn.log_sigmoid((ga @ a_up_d + a_b_d).astype(F32)) / GLA_GATE_TEMP
    return lg.reshape(B, L, GLA_HEADS, GLA_DK)


def gla_readout(o, gr, ln_g):
    B, L = gr.shape[:2]
    return (rmsnorm(o, ln_g).reshape(B, L, D_GLA_V) * jax.nn.silu(gr)).astype(gr.dtype)


def odd_mixer(h_lat, h_ctx, w_in, qn_g, kn_g, rpb, a_up, a_b, gla_ln_g, need_ctx):
    B, L = h_lat.shape[:2]
    Lc = h_ctx.shape[1]
    nq, nk, nv, gq, gk, gv, gr, ga = _split(h_lat @ w_in, ODD_SPLIT)
    cnq, cnk, cnv, cgq, cgk, cgv, cgr, cga = _split(h_ctx @ w_in, ODD_SPLIT)

    def nat_heads(t, g=None):
        t = t.reshape(t.shape[0], t.shape[1], NAT_HEADS, NAT_HEAD)
        if g is not None:
            t = rmsnorm(t, g)
        return jnp.swapaxes(t, 1, 2)

    nat_lat = nat_attention(nq, nk, nv, cnk, cnv, rpb, qn_g, kn_g)

    gh = lambda t, d: t.reshape(t.shape[0], t.shape[1], GLA_HEADS, d)
    qscale = GLA_DK ** -0.5
    q = (rope_2d(gh(gq, GLA_DK)) * qscale).reshape(B, L, D_GLA_K)
    k = rope_2d(gh(gk, GLA_DK)).reshape(B, L, D_GLA_K)
    qc = cgq * qscale
    st0 = jnp.zeros((B, GLA_HEADS // 2, GLA_VPAIR, GLA_KPAIR), F32)
    o_lat, o_ctx = 0.0, 0.0
    for d in range(2):
        lg_c = gla_log_gate(cga, a_up[d], a_b[d]).reshape(B, Lc, D_GLA_K)
        lg_l = gla_log_gate(ga, a_up[d], a_b[d]).reshape(B, L, D_GLA_K)
        oc, st_c = gla_scan(qc, cgk, cgv, lg_c, st0, reverse=(d == 1))
        ol, _ = gla_scan(q, k, gv, lg_l, st_c, reverse=(d == 1))
        o_lat = o_lat + ol
        if need_ctx:
            o_ctx = o_ctx + oc
    o_lat = gh(o_lat, GLA_DV)
    if need_ctx:
        o_ctx = gh(o_ctx, GLA_DV)
    cat_lat = jnp.concatenate([nat_lat, gla_readout(o_lat, gr, gla_ln_g)], axis=-1)
    cat_ctx = None
    if need_ctx:
        kc, vc = nat_heads(cnk, kn_g), nat_heads(cnv)
        nat_ctx = jnp.swapaxes(ctx_attention(nat_heads(cnq, qn_g), kc, vc), 1, 2).reshape(B, Lc, D_NAT)
        cat_ctx = jnp.concatenate([nat_ctx, gla_readout(o_ctx, cgr, gla_ln_g)], axis=-1)
    return cat_lat, cat_ctx


def kernel(x, c, ctx, c_ctx, ada_w, ada_b, norm1_g, norm2_g, ev_w_in, ev_w_out, conv_w, rw_k_k, rw_k_a, rw_r_k, rw_w0, rw_w_up, rw_a0, rw_a_up, rw_g_up, rw_ln_g, rw_ln_b, od_w_in, od_w_out, nat_qn_g, nat_kn_g, nat_rpb, gla_a_up, gla_a_b, gla_ln_g, moe_router, moe_w1, moe_w3, moe_w2):
    depth = ada_w.shape[0]
    ctx_s = ctx
    silu_c = jax.nn.silu(c)
    silu_cc = jax.nn.silu(c_ctx)
    for l in range(depth):
        last = l == depth - 1
        j = l // 2
        sh1, sc1, gt1, sh2, sc2, gt2 = _split((silu_c @ ada_w[l] + ada_b[l])[:, None, :], [D_MODEL] * 6)
        csh1, csc1, cgt1, csh2, csc2, cgt2 = _split(silu_cc @ ada_w[l] + ada_b[l], [D_MODEL] * 6)
        h_lat = modulate(x, norm1_g[l], sh1, sc1)
        h_ctx = modulate(ctx_s, norm1_g[l], csh1, csc1)
        if l % 2 == 0:
            cat_lat, cat_ctx = even_mixer(h_lat, h_ctx, ev_w_in[j], conv_w[j], rw_k_k[j], rw_k_a[j], rw_r_k[j],
                                          rw_w0[j], rw_w_up[j], rw_a0[j], rw_a_up[j], rw_g_up[j], rw_ln_g[j],
                                          rw_ln_b[j], not last)
            w_out = ev_w_out[j]
        else:
            cat_lat, cat_ctx = odd_mixer(h_lat, h_ctx, od_w_in[j], nat_qn_g[j], nat_kn_g[j], nat_rpb[j],
                                         gla_a_up[j], gla_a_b[j], gla_ln_g[j], not last)
            w_out = od_w_out[j]
        experts = (moe_w1[l].astype(BF16), moe_w3[l].astype(BF16), moe_w2[l].astype(BF16))
        x = out_proj_residual(cat_lat, w_out, x, gt1)
        x = x + gt2 * ec_moe(modulate(x, norm2_g[l], sh2, sc2), moe_router[l], *experts)
        if not last:
            ctx_s = out_proj_residual(cat_ctx, w_out, ctx_s, cgt1)
            ctx_s = ctx_s + cgt2 * ec_moe(modulate(ctx_s, norm2_g[l], csh2, csc2), moe_router[l], *experts)
    return x
```

```python
import functools

import jax
import jax.numpy as jnp
import numpy as np
from jax import lax
from jax.experimental import pallas as pl
from jax.experimental.pallas import tpu as pltpu

D_MODEL = 1024
GRID_W = 64
NORM_EPS = 1e-6
F32 = jnp.float32
BF16 = jnp.bfloat16

D_CONV = 512
D_RWKV = 512
RWKV_HEAD = 64
RWKV_HEADS = D_RWKV // RWKV_HEAD
RWKV_DECAY_RANK = 64
RWKV_ICLR_RANK = 64
RWKV_GATE_RANK = 128
RWKV_LN_EPS = 64e-5
D_NAT = 512
NAT_HEAD = 64
NAT_HEADS = D_NAT // NAT_HEAD
NAT_ROWS = 8
NAT_COLS = 16
GLA_HEADS = 4
GLA_DK = 64
GLA_DV = 128
D_GLA_K = GLA_HEADS * GLA_DK
D_GLA_V = GLA_HEADS * GLA_DV
GLA_GATE_RANK = 16
GLA_GATE_TEMP = 16.0
GLA_CHUNK = 64
ROPE_BASE = 10000.0
N_EXPERTS = 16
EC_CAPACITY_FACTOR = 2

EVEN_SPLIT = [D_CONV, D_CONV, D_CONV, D_RWKV, D_RWKV, D_RWKV, RWKV_DECAY_RANK, RWKV_ICLR_RANK, RWKV_GATE_RANK]
ODD_SPLIT = [D_NAT, D_NAT, D_NAT, D_GLA_K, D_GLA_K, D_GLA_V, D_GLA_V, GLA_GATE_RANK]


def _out_proj_body(a_ref, w_ref, x_ref, g_ref, o_ref):
    acc = jnp.dot(a_ref[0].astype(BF16), w_ref[...], preferred_element_type=F32)
    o_ref[0] = x_ref[0] + g_ref[0] * acc


def out_proj_residual(a, w, x, gate, block_rows=512):
    B, L, K = a.shape
    N = w.shape[1]
    tm = min(block_rows, L)
    assert L % tm == 0
    gate = jnp.broadcast_to(gate.reshape(-1, 1, N), (B, 1, N))
    return pl.pallas_call(
        _out_proj_body,
        grid=(B, L // tm),
        in_specs=[
            pl.BlockSpec((1, tm, K), lambda b, i: (b, i, 0)),
            pl.BlockSpec((K, N), lambda b, i: (0, 0)),
            pl.BlockSpec((1, tm, N), lambda b, i: (b, i, 0)),
            pl.BlockSpec((1, 1, N), lambda b, i: (b, 0, 0)),
        ],
        out_specs=pl.BlockSpec((1, tm, N), lambda b, i: (b, i, 0)),
        out_shape=jax.ShapeDtypeStruct((B, L, N), F32),
        compiler_params=pltpu.CompilerParams(dimension_semantics=("parallel", "parallel")),
        name="out_proj_residual",
    )(a, w.astype(BF16), x, gate)


RWKV_CHUNK = 64
PAIR = 2 * RWKV_HEAD


def _mm(a, b):
    return jnp.dot(a.astype(BF16), b.astype(BF16), preferred_element_type=F32)


def _mm_nt(a, b):
    return lax.dot_general(a.astype(BF16), b.astype(BF16), (((1,), (1,)), ((), ())), preferred_element_type=F32)


def _mm_tn(a, b):
    return lax.dot_general(a.astype(BF16), b.astype(BF16), (((0,), (0,)), ((), ())), preferred_element_type=F32)


def _rwkv_chunk_pairs(ins, sts, *, reverse):
    C = RWKV_CHUNK
    row = lax.broadcasted_iota(jnp.int32, (PAIR, PAIR), 0)
    col = lax.broadcasted_iota(jnp.int32, (PAIR, PAIR), 1)
    same_head = (row // RWKV_HEAD) == (col // RWKV_HEAD)
    t_i, s_i = row % C, col % C
    before = (s_i > t_i) if reverse else (s_i < t_i)
    upto = before | (s_i == t_i)

    ct = lax.broadcasted_iota(jnp.int32, (C, C), 0)
    cs = lax.broadcasted_iota(jnp.int32, (C, C), 1)
    tri = ((cs >= ct) if reverse else (cs <= ct)).astype(F32)
    eye = jnp.where(row == col, 1.0, 0.0)

    def expand(x):
        return jnp.where(same_head, jnp.concatenate([x, x], axis=0), 0.0)

    def prepare(r, k, v, kk, lw, a, ka):
        keff = k * (1.0 + (a - 1.0) * ka)
        b = kk * a
        cum_in = jnp.dot(tri, lw, preferred_element_type=F32, precision=lax.Precision.HIGHEST)
        cum_ex = cum_in - lw
        tot = jnp.sum(lw, axis=0, keepdims=True)
        e_neg = jnp.exp(-cum_in)
        e_rem = jnp.exp(tot - cum_in)
        return dict(at2=expand(-kk * jnp.exp(cum_ex)), rt2=expand(r * jnp.exp(cum_in)),
                    bh2=expand(b * e_neg), kh2=expand(keff * e_neg), bp2=expand(b * e_rem),
                    kp2=expand(keff * e_rem), v2=expand(v), dtot=eye * jnp.exp(tot))

    ps = [prepare(*args) for args in ins]
    pps = [_mm_nt(jnp.concatenate([p["at2"], p["rt2"]], axis=0), jnp.concatenate([p["bh2"], p["kh2"]], axis=0))
           for p in ps]
    a_ab = [jnp.where(before, pp[:PAIR, :PAIR], 0.0) for pp in pps]
    a_ak = [jnp.where(before, pp[:PAIR, PAIR:], 0.0) for pp in pps]
    a_rb = [jnp.where(upto, pp[PAIR:, :PAIR], 0.0) for pp in pps]
    a_rk = [jnp.where(upto, pp[PAIR:, PAIR:], 0.0) for pp in pps]

    tinv = [eye + n for n in a_ab]
    npow = a_ab
    for _ in range(int(np.log2(C)) - 1):
        npow = [_mm(n, n) for n in npow]
        tinv = [t + _mm(t, n) for t, n in zip(tinv, npow)]

    av = [_mm(jnp.concatenate([ak, rk], axis=0), p["v2"]) for ak, rk, p in zip(a_ak, a_rk, ps)]
    x = [_mm(t, jnp.concatenate([p["at2"], w[:PAIR]], axis=1)) for t, p, w in zip(tinv, ps, av)]
    z = [_mm(rb, xx) for rb, xx in zip(a_rb, x)]
    bx = [_mm_tn(p["bp2"], xx) for p, xx in zip(ps, x)]
    kv = [_mm_tn(p["kp2"], p["v2"]) for p in ps]
    y2 = [_mm(p["rt2"] + zz[:, :PAIR], st) + zz[:, PAIR:] + w[PAIR:] for p, zz, st, w in zip(ps, z, sts, av)]
    st_new = [_mm(p["dtot"] + b[:, :PAIR], st) + b[:, PAIR:] + g for p, b, st, g in zip(ps, bx, sts, kv)]
    return [y[:C] + y[C:] for y in y2], st_new


def _rwkv_chunk_body(r_ref, k_ref, v_ref, kk_ref, lw_ref, a_ref, ka_ref, s0_ref, y_ref, sT_ref, st_scr, *, reverse):
    c = pl.program_id(1)

    @pl.when(c == 0)
    def _():
        st_scr[...] = s0_ref[0]

    npair = st_scr.shape[0]
    lanes = [slice(p * PAIR, (p + 1) * PAIR) for p in range(npair)]
    ins = [(r_ref[0, :, sl], k_ref[0, :, sl], v_ref[0, :, sl], kk_ref[0, :, sl], lw_ref[0, :, sl], a_ref[0, :, sl],
            ka_ref[:, sl]) for sl in lanes]
    ys, sts = _rwkv_chunk_pairs(ins, [st_scr[p] for p in range(npair)], reverse=reverse)
    for p, sl in enumerate(lanes):
        y_ref[0, :, sl] = ys[p]
        st_scr[p] = sts[p]

    @pl.when(c == pl.num_programs(1) - 1)
    def _():
        sT_ref[0] = st_scr[...]


def rwkv_state_pack(S):
    B, H = S.shape[:2]
    St = jnp.swapaxes(S, -1, -2).reshape(B, H // 2, 2, RWKV_HEAD, RWKV_HEAD)
    z = jnp.zeros_like(St[:, :, 0])
    top = jnp.concatenate([St[:, :, 0], z], axis=-1)
    bot = jnp.concatenate([z, St[:, :, 1]], axis=-1)
    return jnp.concatenate([top, bot], axis=-2)


def rwkv_state_unpack(St):
    B, P = St.shape[:2]
    h0 = St[:, :, :RWKV_HEAD, :RWKV_HEAD]
    h1 = St[:, :, RWKV_HEAD:, RWKV_HEAD:]
    return jnp.swapaxes(jnp.stack([h0, h1], axis=2).reshape(B, 2 * P, RWKV_HEAD, RWKV_HEAD), -1, -2)


def rwkv_chunked(r, k, v, kk, lw, a, k_a, st0, *, reverse, interpret=False):
    B, L, D = r.shape
    C = RWKV_CHUNK
    assert L % C == 0 and D % PAIR == 0
    n, P = L // C, D // PAIR
    cidx = (lambda c: n - 1 - c) if reverse else (lambda c: c)
    seq = pl.BlockSpec((1, C, D), lambda b, c: (b, cidx(c), 0))
    state = pl.BlockSpec((1, P, PAIR, PAIR), lambda b, c: (b, 0, 0, 0))
    return pl.pallas_call(
        functools.partial(_rwkv_chunk_body, reverse=reverse),
        grid=(B, n),
        in_specs=[seq] * 6 + [pl.BlockSpec((1, D), lambda b, c: (0, 0)), state],
        out_specs=[seq, state],
        out_shape=[jax.ShapeDtypeStruct((B, L, D), F32), jax.ShapeDtypeStruct((B, P, PAIR, PAIR), F32)],
        scratch_shapes=[pltpu.VMEM((P, PAIR, PAIR), F32)],
        compiler_params=pltpu.CompilerParams(dimension_semantics=("parallel", "arbitrary")),
        name="rwkv_chunked_rev" if reverse else "rwkv_chunked_fwd",
        interpret=interpret,
    )(r, k, v, kk, lw, a, k_a.reshape(1, D), st0)


NAT_WIN = NAT_ROWS * GRID_W
NAT_ROWS_PER_BLOCK = 16
NAT_MASKED = -1e30


def _pair_rmsnorm(x, g, lane_lo):
    sq = x * x
    s_lo = jnp.sum(jnp.where(lane_lo, sq, 0.0), axis=-1, keepdims=True)
    s_hi = jnp.sum(sq, axis=-1, keepdims=True) - s_lo
    ms = jnp.where(lane_lo, s_lo, s_hi) * (1.0 / NAT_HEAD)
    return x * lax.rsqrt(ms + NORM_EPS) * g


def _nat_body(q_ref, k_ref, v_ref, kc_ref, vc_ref, bias_ref, qg_ref, kg_ref, o_ref, kn_scr, vb_scr, kcn_scr, vcb_scr,
              *, rows):
    rb = pl.program_id(2)
    L = k_ref.shape[1]
    norm_rows = 512
    lane_lo_n = lax.broadcasted_iota(jnp.int32, (norm_rows, PAIR), 1) < NAT_HEAD

    @pl.when(rb == 0)
    def _():
        def norm_block(i, carry):
            sl = pl.ds(pl.multiple_of(i * norm_rows, norm_rows), norm_rows)
            kn_scr[sl, :] = _pair_rmsnorm(k_ref[0, sl, :], kg_ref[...], lane_lo_n).astype(BF16)
            vb_scr[sl, :] = v_ref[0, sl, :].astype(BF16)
            return carry
        lax.fori_loop(0, L // norm_rows, norm_block, 0)
        lane_lo_c = lax.broadcasted_iota(jnp.int32, kc_ref.shape[1:], 1) < NAT_HEAD
        kcn_scr[...] = _pair_rmsnorm(kc_ref[0], kg_ref[...], lane_lo_c).astype(BF16)
        vcb_scr[...] = vc_ref[0].astype(BF16)

    lane_lo = lax.broadcasted_iota(jnp.int32, (GRID_W, PAIR), 1) < NAT_HEAD
    scale = NAT_HEAD ** -0.5

    rows_per_iter = 2
    nt = (((1,), (1,)), ((), ()))

    def row_group(it, carry):
        qsl, kwin, vwin, qh, bias = [], [], [], [], []
        for u in range(rows_per_iter):
            j = it * rows_per_iter + u
            r = rb * NAT_ROWS_PER_BLOCK + j
            rs = jnp.clip(r - NAT_ROWS // 2, 0, rows - NAT_ROWS)
            sl = pl.ds(pl.multiple_of(j * GRID_W, GRID_W), GRID_W)
            wsl = pl.ds(pl.multiple_of(rs * GRID_W, GRID_W), NAT_WIN)
            qn = _pair_rmsnorm(q_ref[0, sl, :], qg_ref[...], lane_lo) * scale
            for h in range(2):
                qsl.append(sl)
                kwin.append(kn_scr[wsl, :])
                vwin.append(vb_scr[wsl, :])
                qh.append(jnp.where(lane_lo if h == 0 else ~lane_lo, qn, 0.0).astype(BF16))
                bias.append(bias_ref[h, r - rs])
        chains = range(2 * rows_per_iter)
        s_loc = [lax.dot_general(qh[i], kwin[i], nt, preferred_element_type=F32) + bias[i] for i in chains]
        s_ctx = [lax.dot_general(qh[i], kcn_scr[...], nt, preferred_element_type=F32) for i in chains]
        m = [jnp.maximum(jnp.max(s_loc[i], axis=-1, keepdims=True), jnp.max(s_ctx[i], axis=-1, keepdims=True))
             for i in chains]
        p_loc = [jnp.exp(s_loc[i] - m[i]) for i in chains]
        p_ctx = [jnp.exp(s_ctx[i] - m[i]) for i in chains]
        den = [jnp.sum(p_loc[i], axis=-1, keepdims=True) + jnp.sum(p_ctx[i], axis=-1, keepdims=True) for i in chains]
        o = [(jnp.dot(p_loc[i].astype(BF16), vwin[i], preferred_element_type=F32)
              + jnp.dot(p_ctx[i].astype(BF16), vcb_scr[...], preferred_element_type=F32)) / den[i] for i in chains]
        for u in range(rows_per_iter):
            o_ref[0, qsl[2 * u], :] = jnp.where(lane_lo, o[2 * u], o[2 * u + 1])
        return carry

    lax.fori_loop(0, NAT_ROWS_PER_BLOCK // rows_per_iter, row_group, 0)


def _nat_bias_table(rpb):
    col = jnp.arange(GRID_W)
    cstart = jnp.clip(col - NAT_COLS // 2, 0, GRID_W - NAT_COLS)
    delta = jnp.arange(NAT_ROWS)
    wrow = jnp.arange(NAT_ROWS)
    ridx = wrow[None, :] - delta[:, None] + (NAT_ROWS - 1)
    cidx = col[None, :] - col[:, None] + (NAT_COLS - 1)
    inwin = (col[None, :] >= cstart[:, None]) & (col[None, :] < cstart[:, None] + NAT_COLS)
    rsel = (ridx[:, :, None] == jnp.arange(2 * NAT_ROWS - 1)).astype(F32)
    csel = ((cidx[:, :, None] == jnp.arange(2 * NAT_COLS - 1)) & inwin[:, :, None]).astype(F32)
    tab = jnp.einsum('hab,dia,ckb->hdcik', rpb.astype(F32), rsel, csel, precision=lax.Precision.HIGHEST)
    tab = jnp.where(inwin[None, None, :, None, :], tab, NAT_MASKED)
    return tab.reshape(rpb.shape[0], NAT_ROWS, GRID_W, NAT_WIN)


def nat_attention(q, k, v, kc, vc, rpb, qn_g, kn_g, interpret=False):
    B, L, D = q.shape
    Lc = kc.shape[1]
    rows = L // GRID_W
    P = D // PAIR
    rpb_blk = NAT_ROWS_PER_BLOCK
    assert rows >= NAT_ROWS and rows % rpb_blk == 0 and L % 512 == 0
    bias = _nat_bias_table(rpb)
    g2 = lambda g: jnp.tile(g, 2).reshape(1, PAIR)
    full = pl.BlockSpec((1, L, PAIR), lambda b, p, i: (b, 0, p))
    cfull = pl.BlockSpec((1, Lc, PAIR), lambda b, p, i: (b, 0, p))
    qblk = pl.BlockSpec((1, rpb_blk * GRID_W, PAIR), lambda b, p, i: (b, i, p))
    gspec = pl.BlockSpec((1, PAIR), lambda b, p, i: (0, 0))
    return pl.pallas_call(
        functools.partial(_nat_body, rows=rows),
        grid=(B, P, rows // rpb_blk),
        in_specs=[qblk, full, full, cfull, cfull,
                  pl.BlockSpec((2, NAT_ROWS, GRID_W, NAT_WIN), lambda b, p, i: (p, 0, 0, 0)), gspec, gspec],
        out_specs=qblk,
        out_shape=jax.ShapeDtypeStruct((B, L, D), F32),
        scratch_shapes=[pltpu.VMEM((L, PAIR), BF16), pltpu.VMEM((L, PAIR), BF16),
                        pltpu.VMEM((Lc, PAIR), BF16), pltpu.VMEM((Lc, PAIR), BF16)],
        compiler_params=pltpu.CompilerParams(dimension_semantics=("parallel", "parallel", "arbitrary"),
                                             vmem_limit_bytes=48 * 1024 * 1024),
        name="nat_attention",
        interpret=interpret,
    )(q, k, v, kc, vc, bias, g2(qn_g), g2(kn_g))


GLA_SUB = 16
GLA_BLOCK = 256
GLA_KPAIR = 2 * GLA_DK
GLA_VPAIR = 2 * GLA_DV
GLA_SPLIT_LOG_RANGE = 60.0


def _gla_body(q_ref, k_ref, v_ref, g_ref, s0_ref, o_ref, sT_ref, st_scr, a_scr, *, reverse):
    C = GLA_SUB
    nsub = GLA_BLOCK // C
    npair = st_scr.shape[0]
    blk = pl.program_id(1)

    @pl.when(blk == 0)
    def _():
        st_scr[...] = s0_ref[0]

    ti = lax.broadcasted_iota(jnp.int32, (C, C), 0)
    si = lax.broadcasted_iota(jnp.int32, (C, C), 1)
    tri = ((si >= ti) if reverse else (si <= ti)).astype(F32)
    lane_lo = lax.broadcasted_iota(jnp.int32, (C, GLA_KPAIR), 1) < GLA_DK
    row_id = lax.broadcasted_iota(jnp.int32, (C, GLA_KPAIR), 0)
    vrow = lax.broadcasted_iota(jnp.int32, (2 * C, GLA_VPAIR), 0)
    vcol = lax.broadcasted_iota(jnp.int32, (2 * C, GLA_VPAIR), 1)
    v_same_head = (vrow // C) == (vcol // GLA_DV)
    srow = lax.broadcasted_iota(jnp.int32, (GLA_VPAIR, GLA_KPAIR), 0)
    scol = lax.broadcasted_iota(jnp.int32, (GLA_VPAIR, GLA_KPAIR), 1)
    s_same_head = (srow // GLA_DV) == (scol // GLA_DK)

    pairs = range(npair)
    klanes = [slice(p * GLA_KPAIR, (p + 1) * GLA_KPAIR) for p in pairs]
    vlanes = [slice(p * GLA_VPAIR, (p + 1) * GLA_VPAIR) for p in pairs]

    def sub_chunk(i, carry):
        ci = (nsub - 1 - i) if reverse else i
        sl = pl.ds(pl.multiple_of(ci * C, C), C)
        q = [q_ref[0, sl, ks] for ks in klanes]
        k = [k_ref[0, sl, ks] for ks in klanes]
        g = [g_ref[0, sl, ks] for ks in klanes]
        v = [v_ref[0, sl, vs] for vs in vlanes]
        st = [st_scr[p] for p in pairs]
        b = [jnp.dot(tri, gg, preferred_element_type=F32, precision=lax.Precision.HIGHEST) for gg in g]
        b_end = [jnp.sum(gg, axis=0, keepdims=True) for gg in g]
        kd = [k[p] * jnp.exp(b_end[p] - b[p]) for p in pairs]
        b_min = jnp.min(jnp.concatenate(b_end, axis=1))
        splittable = b_min >= -GLA_SPLIT_LOG_RANGE

        @pl.when(splittable)
        def _():
            for p in pairs:
                qd = q[p] * jnp.exp(b[p] - b_end[p])
                a_lo = _mm_nt(jnp.where(lane_lo, qd, 0.0), kd[p])
                a_hi = _mm_nt(jnp.where(lane_lo, 0.0, qd), kd[p])
                a_scr[p] = jnp.concatenate([jnp.where(tri > 0.0, a_lo, 0.0), jnp.where(tri > 0.0, a_hi, 0.0)], axis=1)

        @pl.when(jnp.logical_not(splittable))
        def _():
            a_lo = [jnp.zeros((C, C), F32) for _ in pairs]
            a_hi = [jnp.zeros((C, C), F32) for _ in pairs]
            for j in range(C):
                seen = (row_id <= j) if reverse else (row_id >= j)
                for p in pairs:
                    decay = jnp.exp(jnp.where(seen, b[p] - b[p][j:j + 1, :], 0.0))
                    f = jnp.where(seen, q[p] * k[p][j:j + 1, :] * decay, 0.0)
                    r_lo = jnp.sum(jnp.where(lane_lo, f, 0.0), axis=-1, keepdims=True)
                    r_hi = jnp.sum(jnp.where(lane_lo, 0.0, f), axis=-1, keepdims=True)
                    a_lo[p] = jnp.where(si == j, r_lo, a_lo[p])
                    a_hi[p] = jnp.where(si == j, r_hi, a_hi[p])
            for p in pairs:
                a_scr[p] = jnp.concatenate([a_lo[p], a_hi[p]], axis=1)

        v_bd = [jnp.where(v_same_head, jnp.concatenate([vv, vv], axis=0), 0.0) for vv in v]
        o_in = [_mm(a_scr[p], v_bd[p]) for p in pairs]
        o_st = [_mm_nt(q[p] * jnp.exp(b[p]), st[p]) for p in pairs]
        kv = [_mm_tn(v[p], kd[p]) for p in pairs]
        for p in pairs:
            o_ref[0, sl, vlanes[p]] = o_in[p] + o_st[p]
            st_scr[p] = st[p] * jnp.exp(b_end[p]) + jnp.where(s_same_head, kv[p], 0.0)
        return carry

    lax.fori_loop(0, nsub, sub_chunk, 0)

    @pl.when(blk == pl.num_programs(1) - 1)
    def _():
        sT_ref[0] = st_scr[...]


def gla_scan(q, k, v, logg, st0, *, reverse, interpret=False):
    B, L, Dk = q.shape
    Dv = v.shape[-1]
    T = GLA_BLOCK
    assert L % T == 0 and Dk % GLA_KPAIR == 0
    n, P = L // T, Dk // GLA_KPAIR
    bidx = (lambda c: n - 1 - c) if reverse else (lambda c: c)
    kspec = pl.BlockSpec((1, T, Dk), lambda b, c: (b, bidx(c), 0))
    vspec = pl.BlockSpec((1, T, Dv), lambda b, c: (b, bidx(c), 0))
    sspec = pl.BlockSpec((1, P, GLA_VPAIR, GLA_KPAIR), lambda b, c: (b, 0, 0, 0))
    return pl.pallas_call(
        functools.partial(_gla_body, reverse=reverse),
        grid=(B, n),
        in_specs=[kspec, kspec, vspec, kspec, sspec],
        out_specs=[vspec, sspec],
        out_shape=[jax.ShapeDtypeStruct((B, L, Dv), F32),
                   jax.ShapeDtypeStruct((B, P, GLA_VPAIR, GLA_KPAIR), F32)],
        scratch_shapes=[pltpu.VMEM((P, GLA_VPAIR, GLA_KPAIR), F32), pltpu.VMEM((P, GLA_SUB, 2 * GLA_SUB), F32)],
        compiler_params=pltpu.CompilerParams(dimension_semantics=("parallel", "arbitrary")),
        name="gla_scan_rev" if reverse else "gla_scan_fwd",
        interpret=interpret,
    )(q, k, v, logg, st0)


MOE_F_TILE = 512


def _expert_ffn_body(x_ref, g_ref, w1_ref, w3_ref, w2_ref, o_ref, acc_ref):
    f = pl.program_id(3)

    @pl.when(f == 0)
    def _():
        acc_ref[...] = jnp.zeros_like(acc_ref)

    x = x_ref[0, 0].astype(BF16)
    h1 = jnp.dot(x, w1_ref[0], preferred_element_type=F32)
    h3 = jnp.dot(x, w3_ref[0], preferred_element_type=F32)
    hid = (h1 * jax.nn.sigmoid(h1) * h3).astype(BF16)
    acc_ref[...] += jnp.dot(hid, w2_ref[0], preferred_element_type=F32)

    @pl.when(f == pl.num_programs(3) - 1)
    def _():
        o_ref[0, 0] = acc_ref[...] * g_ref[0, 0]


def expert_ffn(xin, gate, w1, w3, w2, interpret=False):
    B, E, cap, D = xin.shape
    F = w1.shape[-1]
    tm = min(cap, 1024)
    tf = MOE_F_TILE
    assert cap % tm == 0 and F % tf == 0
    return pl.pallas_call(
        _expert_ffn_body,
        grid=(E, B, cap // tm, F // tf),
        in_specs=[
            pl.BlockSpec((1, 1, tm, D), lambda e, b, i, f: (b, e, i, 0)),
            pl.BlockSpec((1, 1, tm, 1), lambda e, b, i, f: (b, e, i, 0)),
            pl.BlockSpec((1, D, tf), lambda e, b, i, f: (e, 0, f)),
            pl.BlockSpec((1, D, tf), lambda e, b, i, f: (e, 0, f)),
            pl.BlockSpec((1, tf, D), lambda e, b, i, f: (e, f, 0)),
        ],
        out_specs=pl.BlockSpec((1, 1, tm, D), lambda e, b, i, f: (b, e, i, 0)),
        out_shape=jax.ShapeDtypeStruct((B, E, cap, D), F32),
        scratch_shapes=[pltpu.VMEM((tm, D), F32)],
        compiler_params=pltpu.CompilerParams(
            dimension_semantics=("parallel", "parallel", "parallel", "arbitrary"),
            vmem_limit_bytes=48 * 1024 * 1024),
        name="expert_ffn",
        interpret=interpret,
    )(xin, gate[..., None], w1.astype(BF16), w3.astype(BF16), w2.astype(BF16))


def _split(p, sizes):
    return jnp.split(p, np.cumsum(sizes)[:-1].tolist(), axis=-1)


def rmsnorm(x, g, eps=NORM_EPS):
    xf = x.astype(F32)
    y = xf * lax.rsqrt(jnp.mean(xf * xf, axis=-1, keepdims=True) + eps)
    return (y * g.astype(F32)).astype(x.dtype)


def modulate(x, g, shift, scale):
    return rmsnorm(x, g) * (1 + scale) + shift


def conv3(u, w):
    up = jnp.pad(u, ((0, 0), (1, 1), (0, 0)))
    return up[:, :-2] * w[0] + up[:, 1:-1] * w[1] + up[:, 2:] * w[2]


def rope_2d(x):
    L, d = x.shape[1], x.shape[-1]
    half, nf = d // 2, d // 4
    t = jnp.arange(L)
    inv = ROPE_BASE ** (-jnp.arange(nf, dtype=F32) / nf)

    def rot(u, pos):
        ang = pos.astype(F32)[:, None] * inv[None, :]
        cos, sin = jnp.cos(ang)[None, :, None, :], jnp.sin(ang)[None, :, None, :]
        u1, u2 = u[..., :nf].astype(F32), u[..., nf:].astype(F32)
        return jnp.concatenate([u1 * cos - u2 * sin, u1 * sin + u2 * cos], axis=-1)

    return jnp.concatenate([rot(x[..., :half], t // GRID_W), rot(x[..., half:], t % GRID_W)], axis=-1).astype(x.dtype)


def ec_moe(h, router, w1, w3, w2):
    B, T, D = h.shape
    cap = EC_CAPACITY_FACTOR * T // N_EXPERTS
    aff = jax.nn.softmax((h @ router).astype(F32), axis=-1)
    gate, idx = lax.top_k(jnp.swapaxes(aff, 1, 2), cap)
    xin = jax.vmap(lambda hb, ib: hb[ib])(h, idx)
    if B * cap <= 1024:
        merge = lambda t: jnp.swapaxes(t, 0, 1).reshape((1, N_EXPERTS, B * cap) + t.shape[3:])
        y = expert_ffn(merge(xin), merge(gate), w1, w3, w2)
        y = jnp.swapaxes(y.reshape(N_EXPERTS, B, cap, D), 0, 1)
    else:
        y = expert_ffn(xin, gate, w1, w3, w2)
    out = jax.vmap(lambda ib, yb: jnp.zeros((T, D), yb.dtype).at[ib.reshape(-1)].add(yb.reshape(-1, D)))(idx, y)
    return out.astype(h.dtype)


def rwkv_readout(y, r, k, v, xg, r_k, g_up, ln_g, ln_b):
    B, L = r.shape[:2]
    hs = lambda t: t.reshape(B, L, RWKV_HEADS, RWKV_HEAD)
    yf = hs(y.astype(F32))
    mu = jnp.mean(yf, axis=-1, keepdims=True)
    var = jnp.mean(jnp.square(yf - mu), axis=-1, keepdims=True)
    yn = ((yf - mu) * lax.rsqrt(var + RWKV_LN_EPS)).reshape(B, L, D_RWKV) * ln_g + ln_b
    bonus = (jnp.sum(hs(r) * hs(k) * r_k, axis=-1, keepdims=True) * hs(v)).reshape(B, L, D_RWKV)
    g = jax.nn.sigmoid(xg) @ g_up
    return ((yn + bonus) * g).astype(r.dtype)


def even_mixer(h_lat, h_ctx, w_in, conv_w, k_k, k_a, r_k, w0, w_up, a0, a_up, g_up, ln_g, ln_b, need_ctx):
    B = h_lat.shape[0]
    p_lat = _split(h_lat @ w_in, EVEN_SPLIT)
    p_ctx = _split(h_ctx @ w_in, EVEN_SPLIT)

    def conv_branch(u, gate_b, gate_c):
        return gate_b * conv3(gate_c * u, conv_w)

    def unit_key(k):
        Bq, L = k.shape[:2]
        kkf = (k * k_k).astype(F32).reshape(Bq, L, RWKV_HEADS, RWKV_HEAD)
        return (kkf * lax.rsqrt(jnp.sum(kkf * kkf, axis=-1, keepdims=True) + 1e-12)).reshape(Bq, L, D_RWKV)

    def dir_gates(xw, xa, d):
        w_raw = (w0[d] + jnp.tanh(xw) @ w_up[d]).astype(F32)
        return -jnp.exp(-jax.nn.softplus(-w_raw) - 0.5), jax.nn.sigmoid(a0[d] + xa @ a_up[d])

    kk_ctx, kk_lat = unit_key(p_ctx[4]), unit_key(p_lat[4])
    st0 = jnp.zeros((B, RWKV_HEADS // 2, PAIR, PAIR), F32)
    y_lat, y_ctx = 0.0, 0.0
    for d, rev in ((0, False), (1, True)):
        yc, st_c = rwkv_chunked(*p_ctx[3:6], kk_ctx, *dir_gates(*p_ctx[6:8], d), k_a, st0, reverse=rev)
        yl, _ = rwkv_chunked(*p_lat[3:6], kk_lat, *dir_gates(*p_lat[6:8], d), k_a, st_c, reverse=rev)
        y_lat = y_lat + yl
        if need_ctx:
            y_ctx = y_ctx + yc
    cat_lat = jnp.concatenate([conv_branch(*p_lat[0:3]),
                               rwkv_readout(y_lat, *p_lat[3:6], p_lat[8], r_k, g_up, ln_g, ln_b)], axis=-1)
    cat_ctx = None
    if need_ctx:
        cat_ctx = jnp.concatenate([conv_branch(*p_ctx[0:3]),
                                   rwkv_readout(y_ctx, *p_ctx[3:6], p_ctx[8], r_k, g_up, ln_g, ln_b)], axis=-1)
    return cat_lat, cat_ctx


def ctx_attention(q, k, v):
    s = jnp.einsum('bhqd,bhkd->bhqk', q, k).astype(F32) * (q.shape[-1] ** -0.5)
    p = jax.nn.softmax(s, axis=-1).astype(v.dtype)
    return jnp.einsum('bhqk,bhkd->bhqd', p, v)


def gla_log_gate(ga, a_up_d, a_b_d):
    B, L = ga.shape[:2]
    lg = jax.nn.log_sigmoid((ga @ a_up_d + a_b_d).astype(F32)) / GLA_GATE_TEMP
    return lg.reshape(B, L, GLA_HEADS, GLA_DK)


def gla_readout(o, gr, ln_g):
    B, L = gr.shape[:2]
    return (rmsnorm(o, ln_g).reshape(B, L, D_GLA_V) * jax.nn.silu(gr)).astype(gr.dtype)


def odd_mixer(h_lat, h_ctx, w_in, qn_g, kn_g, rpb, a_up, a_b, gla_ln_g, need_ctx):
    B, L = h_lat.shape[:2]
    Lc = h_ctx.shape[1]
    nq, nk, nv, gq, gk, gv, gr, ga = _split(h_lat @ w_in, ODD_SPLIT)
    cnq, cnk, cnv, cgq, cgk, cgv, cgr, cga = _split(h_ctx @ w_in, ODD_SPLIT)

    def nat_heads(t, g=None):
        t = t.reshape(t.shape[0], t.shape[1], NAT_HEADS, NAT_HEAD)
        if g is not None:
            t = rmsnorm(t, g)
        return jnp.swapaxes(t, 1, 2)

    nat_lat = nat_attention(nq, nk, nv, cnk, cnv, rpb, qn_g, kn_g)

    gh = lambda t, d: t.reshape(t.shape[0], t.shape[1], GLA_HEADS, d)
    qscale = GLA_DK ** -0.5
    q = (rope_2d(gh(gq, GLA_DK)) * qscale).reshape(B, L, D_GLA_K)
    k = rope_2d(gh(gk, GLA_DK)).reshape(B, L, D_GLA_K)
    qc = cgq * qscale
    st0 = jnp.zeros((B, GLA_HEADS // 2, GLA_VPAIR, GLA_KPAIR), F32)
    o_lat, o_ctx = 0.0, 0.0
    for d in range(2):
        lg_c = gla_log_gate(cga, a_up[d], a_b[d]).reshape(B, Lc, D_GLA_K)
        lg_l = gla_log_gate(ga, a_up[d], a_b[d]).reshape(B, L, D_GLA_K)
        oc, st_c = gla_scan(qc, cgk, cgv, lg_c, st0, reverse=(d == 1))
        ol, _ = gla_scan(q, k, gv, lg_l, st_c, reverse=(d == 1))
        o_lat = o_lat + ol
        if need_ctx:
            o_ctx = o_ctx + oc
    o_lat = gh(o_lat, GLA_DV)
    if need_ctx:
        o_ctx = gh(o_ctx, GLA_DV)
    cat_lat = jnp.concatenate([nat_lat, gla_readout(o_lat, gr, gla_ln_g)], axis=-1)
    cat_ctx = None
    if need_ctx:
        kc, vc = nat_heads(cnk, kn_g), nat_heads(cnv)
        nat_ctx = jnp.swapaxes(ctx_attention(nat_heads(cnq, qn_g), kc, vc), 1, 2).reshape(B, Lc, D_NAT)
        cat_ctx = jnp.concatenate([nat_ctx, gla_readout(o_ctx, cgr, gla_ln_g)], axis=-1)
    return cat_lat, cat_ctx


def kernel(x, c, ctx, c_ctx, ada_w, ada_b, norm1_g, norm2_g, ev_w_in, ev_w_out, conv_w, rw_k_k, rw_k_a, rw_r_k, rw_w0, rw_w_up, rw_a0, rw_a_up, rw_g_up, rw_ln_g, rw_ln_b, od_w_in, od_w_out, nat_qn_g, nat_kn_g, nat_rpb, gla_a_up, gla_a_b, gla_ln_g, moe_router, moe_w1, moe_w3, moe_w2):
    depth = ada_w.shape[0]
    ctx_s = ctx
    silu_c = jax.nn.silu(c)
    silu_cc = jax.nn.silu(c_ctx)
    for l in range(depth):
        last = l == depth - 1
        j = l // 2
        sh1, sc1, gt1, sh2, sc2, gt2 = _split((silu_c @ ada_w[l] + ada_b[l])[:, None, :], [D_MODEL] * 6)
        csh1, csc1, cgt1, csh2, csc2, cgt2 = _split(silu_cc @ ada_w[l] + ada_b[l], [D_MODEL] * 6)
        h_lat = modulate(x, norm1_g[l], sh1, sc1)
        h_ctx = modulate(ctx_s, norm1_g[l], csh1, csc1)
        if l % 2 == 0:
            cat_lat, cat_ctx = even_mixer(h_lat, h_ctx, ev_w_in[j], conv_w[j], rw_k_k[j], rw_k_a[j], rw_r_k[j],
                                          rw_w0[j], rw_w_up[j], rw_a0[j], rw_a_up[j], rw_g_up[j], rw_ln_g[j],
                                          rw_ln_b[j], not last)
            w_out = ev_w_out[j]
        else:
            cat_lat, cat_ctx = odd_mixer(h_lat, h_ctx, od_w_in[j], nat_qn_g[j], nat_kn_g[j], nat_rpb[j],
                                         gla_a_up[j], gla_a_b[j], gla_ln_g[j], not last)
            w_out = od_w_out[j]
        experts = (moe_w1[l].astype(BF16), moe_w3[l].astype(BF16), moe_w2[l].astype(BF16))
        x = out_proj_residual(cat_lat, w_out, x, gt1)
        x = x + gt2 * ec_moe(modulate(x, norm2_g[l], sh2, sc2), moe_router[l], *experts)
        if not last:
            ctx_s = out_proj_residual(cat_ctx, w_out, ctx_s, cgt1)
            ctx_s = ctx_s + cgt2 * ec_moe(modulate(ctx_s, norm2_g[l], csh2, csc2), moe_router[l], *experts)
    return x
```

```python
import functools

import jax
import jax.numpy as jnp
import numpy as np
from jax import lax
from jax.experimental import pallas as pl
from jax.experimental.pallas import tpu as pltpu

D_MODEL = 1024
GRID_W = 64
NORM_EPS = 1e-6
F32 = jnp.float32
BF16 = jnp.bfloat16

D_CONV = 512
D_RWKV = 512
RWKV_HEAD = 64
RWKV_HEADS = D_RWKV // RWKV_HEAD
RWKV_DECAY_RANK = 64
RWKV_ICLR_RANK = 64
RWKV_GATE_RANK = 128
RWKV_LN_EPS = 64e-5
D_NAT = 512
NAT_HEAD = 64
NAT_HEADS = D_NAT // NAT_HEAD
NAT_ROWS = 8
NAT_COLS = 16
GLA_HEADS = 4
GLA_DK = 64
GLA_DV = 128
D_GLA_K = GLA_HEADS * GLA_DK
D_GLA_V = GLA_HEADS * GLA_DV
GLA_GATE_RANK = 16
GLA_GATE_TEMP = 16.0
GLA_CHUNK = 64
ROPE_BASE = 10000.0
N_EXPERTS = 16
EC_CAPACITY_FACTOR = 2

EVEN_SPLIT = [D_CONV, D_CONV, D_CONV, D_RWKV, D_RWKV, D_RWKV, RWKV_DECAY_RANK, RWKV_ICLR_RANK, RWKV_GATE_RANK]
ODD_SPLIT = [D_NAT, D_NAT, D_NAT, D_GLA_K, D_GLA_K, D_GLA_V, D_GLA_V, GLA_GATE_RANK]


def _out_proj_body(a_ref, w_ref, x_ref, g_ref, o_ref):
    acc = jnp.dot(a_ref[0].astype(BF16), w_ref[...], preferred_element_type=F32)
    o_ref[0] = x_ref[0] + g_ref[0] * acc


def out_proj_residual(a, w, x, gate, block_rows=512):
    B, L, K = a.shape
    N = w.shape[1]
    tm = min(block_rows, L)
    assert L % tm == 0
    gate = jnp.broadcast_to(gate.reshape(-1, 1, N), (B, 1, N))
    return pl.pallas_call(
        _out_proj_body,
        grid=(B, L // tm),
        in_specs=[
            pl.BlockSpec((1, tm, K), lambda b, i: (b, i, 0)),
            pl.BlockSpec((K, N), lambda b, i: (0, 0)),
            pl.BlockSpec((1, tm, N), lambda b, i: (b, i, 0)),
            pl.BlockSpec((1, 1, N), lambda b, i: (b, 0, 0)),
        ],
        out_specs=pl.BlockSpec((1, tm, N), lambda b, i: (b, i, 0)),
        out_shape=jax.ShapeDtypeStruct((B, L, N), F32),
        compiler_params=pltpu.CompilerParams(dimension_semantics=("parallel", "parallel")),
        name="out_proj_residual",
    )(a, w.astype(BF16), x, gate)


RWKV_CHUNK = 64
PAIR = 2 * RWKV_HEAD


def _mm(a, b):
    return jnp.dot(a.astype(BF16), b.astype(BF16), preferred_element_type=F32)


def _mm_nt(a, b):
    return lax.dot_general(a.astype(BF16), b.astype(BF16), (((1,), (1,)), ((), ())), preferred_element_type=F32)


def _mm_tn(a, b):
    return lax.dot_general(a.astype(BF16), b.astype(BF16), (((0,), (0,)), ((), ())), preferred_element_type=F32)


def _rwkv_chunk_pairs(ins, sts, *, reverse):
    C = RWKV_CHUNK
    row = lax.broadcasted_iota(jnp.int32, (PAIR, PAIR), 0)
    col = lax.broadcasted_iota(jnp.int32, (PAIR, PAIR), 1)
    same_head = (row // RWKV_HEAD) == (col // RWKV_HEAD)
    t_i, s_i = row % C, col % C
    before = (s_i > t_i) if reverse else (s_i < t_i)
    upto = before | (s_i == t_i)

    ct = lax.broadcasted_iota(jnp.int32, (C, C), 0)
    cs = lax.broadcasted_iota(jnp.int32, (C, C), 1)
    tri = ((cs >= ct) if reverse else (cs <= ct)).astype(F32)
    eye = jnp.where(row == col, 1.0, 0.0)

    def expand(x):
        return jnp.where(same_head, jnp.concatenate([x, x], axis=0), 0.0)

    def prepare(r, k, v, kk, lw, a, ka):
        keff = k * (1.0 + (a - 1.0) * ka)
        b = kk * a
        cum_in = jnp.dot(tri, lw, preferred_element_type=F32, precision=lax.Precision.HIGHEST)
        cum_ex = cum_in - lw
        tot = jnp.sum(lw, axis=0, keepdims=True)
        e_neg = jnp.exp(-cum_in)
        e_rem = jnp.exp(tot - cum_in)
        return dict(at2=expand(-kk * jnp.exp(cum_ex)), rt2=expand(r * jnp.exp(cum_in)),
                    bh2=expand(b * e_neg), kh2=expand(keff * e_neg), bp2=expand(b * e_rem),
                    kp2=expand(keff * e_rem), v2=expand(v), dtot=eye * jnp.exp(tot))

    ps = [prepare(*args) for args in ins]
    pps = [_mm_nt(jnp.concatenate([p["at2"], p["rt2"]], axis=0), jnp.concatenate([p["bh2"], p["kh2"]], axis=0))
           for p in ps]
    a_ab = [jnp.where(before, pp[:PAIR, :PAIR], 0.0) for pp in pps]
    a_ak = [jnp.where(before, pp[:PAIR, PAIR:], 0.0) for pp in pps]
    a_rb = [jnp.where(upto, pp[PAIR:, :PAIR], 0.0) for pp in pps]
    a_rk = [jnp.where(upto, pp[PAIR:, PAIR:], 0.0) for pp in pps]

    tinv = [eye + n for n in a_ab]
    npow = a_ab
    for _ in range(int(np.log2(C)) - 1):
        npow = [_mm(n, n) for n in npow]
        tinv = [t + _mm(t, n) for t, n in zip(tinv, npow)]

    av = [_mm(jnp.concatenate([ak, rk], axis=0), p["v2"]) for ak, rk, p in zip(a_ak, a_rk, ps)]
    x = [_mm(t, jnp.concatenate([p["at2"], w[:PAIR]], axis=1)) for t, p, w in zip(tinv, ps, av)]
    z = [_mm(rb, xx) for rb, xx in zip(a_rb, x)]
    bx = [_mm_tn(p["bp2"], xx) for p, xx in zip(ps, x)]
    kv = [_mm_tn(p["kp2"], p["v2"]) for p in ps]
    y2 = [_mm(p["rt2"] + zz[:, :PAIR], st) + zz[:, PAIR:] + w[PAIR:] for p, zz, st, w in zip(ps, z, sts, av)]
    st_new = [_mm(p["dtot"] + b[:, :PAIR], st) + b[:, PAIR:] + g for p, b, st, g in zip(ps, bx, sts, kv)]
    return [y[:C] + y[C:] for y in y2], st_new


def _rwkv_chunk_body(r_ref, k_ref, v_ref, kk_ref, lw_ref, a_ref, ka_ref, s0_ref, y_ref, sT_ref, st_scr, *, reverse):
    c = pl.program_id(1)

    @pl.when(c == 0)
    def _():
        st_scr[...] = s0_ref[0]

    npair = st_scr.shape[0]
    lanes = [slice(p * PAIR, (p + 1) * PAIR) for p in range(npair)]
    ins = [(r_ref[0, :, sl], k_ref[0, :, sl], v_ref[0, :, sl], kk_ref[0, :, sl], lw_ref[0, :, sl], a_ref[0, :, sl],
            ka_ref[:, sl]) for sl in lanes]
    ys, sts = _rwkv_chunk_pairs(ins, [st_scr[p] for p in range(npair)], reverse=reverse)
    for p, sl in enumerate(lanes):
        y_ref[0, :, sl] = ys[p]
        st_scr[p] = sts[p]

    @pl.when(c == pl.num_programs(1) - 1)
    def _():
        sT_ref[0] = st_scr[...]


def rwkv_state_pack(S):
    B, H = S.shape[:2]
    St = jnp.swapaxes(S, -1, -2).reshape(B, H // 2, 2, RWKV_HEAD, RWKV_HEAD)
    z = jnp.zeros_like(St[:, :, 0])
    top = jnp.concatenate([St[:, :, 0], z], axis=-1)
    bot = jnp.concatenate([z, St[:, :, 1]], axis=-1)
    return jnp.concatenate([top, bot], axis=-2)


def rwkv_state_unpack(St):
    B, P = St.shape[:2]
    h0 = St[:, :, :RWKV_HEAD, :RWKV_HEAD]
    h1 = St[:, :, RWKV_HEAD:, RWKV_HEAD:]
    return jnp.swapaxes(jnp.stack([h0, h1], axis=2).reshape(B, 2 * P, RWKV_HEAD, RWKV_HEAD), -1, -2)


def rwkv_chunked(r, k, v, kk, lw, a, k_a, st0, *, reverse, interpret=False):
    B, L, D = r.shape
    C = RWKV_CHUNK
    assert L % C == 0 and D % PAIR == 0
    n, P = L // C, D // PAIR
    cidx = (lambda c: n - 1 - c) if reverse else (lambda c: c)
    seq = pl.BlockSpec((1, C, D), lambda b, c: (b, cidx(c), 0))
    state = pl.BlockSpec((1, P, PAIR, PAIR), lambda b, c: (b, 0, 0, 0))
    return pl.pallas_call(
        functools.partial(_rwkv_chunk_body, reverse=reverse),
        grid=(B, n),
        in_specs=[seq] * 6 + [pl.BlockSpec((1, D), lambda b, c: (0, 0)), state],
        out_specs=[seq, state],
        out_shape=[jax.ShapeDtypeStruct((B, L, D), F32), jax.ShapeDtypeStruct((B, P, PAIR, PAIR), F32)],
        scratch_shapes=[pltpu.VMEM((P, PAIR, PAIR), F32)],
        compiler_params=pltpu.CompilerParams(dimension_semantics=("parallel", "arbitrary")),
        name="rwkv_chunked_rev" if reverse else "rwkv_chunked_fwd",
        interpret=interpret,
    )(r, k, v, kk, lw, a, k_a.reshape(1, D), st0)


NAT_WIN = NAT_ROWS * GRID_W
NAT_ROWS_PER_BLOCK = 16
NAT_MASKED = -1e30


def _pair_rmsnorm(x, g, lane_lo):
    sq = x * x
    s_lo = jnp.sum(jnp.where(lane_lo, sq, 0.0), axis=-1, keepdims=True)
    s_hi = jnp.sum(sq, axis=-1, keepdims=True) - s_lo
    ms = jnp.where(lane_lo, s_lo, s_hi) * (1.0 / NAT_HEAD)
    return x * lax.rsqrt(ms + NORM_EPS) * g


def _nat_body(q_ref, k_ref, v_ref, kc_ref, vc_ref, bias_ref, qg_ref, kg_ref, o_ref, kn_scr, vb_scr, kcn_scr, vcb_scr,
              *, rows):
    rb = pl.program_id(2)
    L = k_ref.shape[1]
    norm_rows = 512
    lane_lo_n = lax.broadcasted_iota(jnp.int32, (norm_rows, PAIR), 1) < NAT_HEAD

    @pl.when(rb == 0)
    def _():
        def norm_block(i, carry):
            sl = pl.ds(pl.multiple_of(i * norm_rows, norm_rows), norm_rows)
            kn_scr[sl, :] = _pair_rmsnorm(k_ref[0, sl, :], kg_ref[...], lane_lo_n).astype(BF16)
            vb_scr[sl, :] = v_ref[0, sl, :].astype(BF16)
            return carry
        lax.fori_loop(0, L // norm_rows, norm_block, 0)
        lane_lo_c = lax.broadcasted_iota(jnp.int32, kc_ref.shape[1:], 1) < NAT_HEAD
        kcn_scr[...] = _pair_rmsnorm(kc_ref[0], kg_ref[...], lane_lo_c).astype(BF16)
        vcb_scr[...] = vc_ref[0].astype(BF16)

    lane_lo = lax.broadcasted_iota(jnp.int32, (GRID_W, PAIR), 1) < NAT_HEAD
    scale = NAT_HEAD ** -0.5

    rows_per_iter = 2
    nt = (((1,), (1,)), ((), ()))

    def row_group(it, carry):
        qsl, kwin, vwin, qh, bias = [], [], [], [], []
        for u in range(rows_per_iter):
            j = it * rows_per_iter + u
            r = rb * NAT_ROWS_PER_BLOCK + j
            rs = jnp.clip(r - NAT_ROWS // 2, 0, rows - NAT_ROWS)
            sl = pl.ds(pl.multiple_of(j * GRID_W, GRID_W), GRID_W)
            wsl = pl.ds(pl.multiple_of(rs * GRID_W, GRID_W), NAT_WIN)
            qn = _pair_rmsnorm(q_ref[0, sl, :], qg_ref[...], lane_lo) * scale
            for h in range(2):
                qsl.append(sl)
                kwin.append(kn_scr[wsl, :])
                vwin.append(vb_scr[wsl, :])
                qh.append(jnp.where(lane_lo if h == 0 else ~lane_lo, qn, 0.0).astype(BF16))
                bias.append(bias_ref[h, r - rs])
        chains = range(2 * rows_per_iter)
        s_loc = [lax.dot_general(qh[i], kwin[i], nt, preferred_element_type=F32) + bias[i] for i in chains]
        s_ctx = [lax.dot_general(qh[i], kcn_scr[...], nt, preferred_element_type=F32) for i in chains]
        m = [jnp.maximum(jnp.max(s_loc[i], axis=-1, keepdims=True), jnp.max(s_ctx[i], axis=-1, keepdims=True))
             for i in chains]
        p_loc = [jnp.exp(s_loc[i] - m[i]) for i in chains]
        p_ctx = [jnp.exp(s_ctx[i] - m[i]) for i in chains]
        den = [jnp.sum(p_loc[i], axis=-1, keepdims=True) + jnp.sum(p_ctx[i], axis=-1, keepdims=True) for i in chains]
        o = [(jnp.dot(p_loc[i].astype(BF16), vwin[i], preferred_element_type=F32)
              + jnp.dot(p_ctx[i].astype(BF16), vcb_scr[...], preferred_element_type=F32)) / den[i] for i in chains]
        for u in range(rows_per_iter):
            o_ref[0, qsl[2 * u], :] = jnp.where(lane_lo, o[2 * u], o[2 * u + 1])
        return carry

    lax.fori_loop(0, NAT_ROWS_PER_BLOCK // rows_per_iter, row_group, 0)


def _nat_bias_table(rpb):
    col = jnp.arange(GRID_W)
    cstart = jnp.clip(col - NAT_COLS // 2, 0, GRID_W - NAT_COLS)
    delta = jnp.arange(NAT_ROWS)
    wrow = jnp.arange(NAT_ROWS)
    ridx = wrow[None, :] - delta[:, None] + (NAT_ROWS - 1)
    cidx = col[None, :] - col[:, None] + (NAT_COLS - 1)
    inwin = (col[None, :] >= cstart[:, None]) & (col[None, :] < cstart[:, None] + NAT_COLS)
    rsel = (ridx[:, :, None] == jnp.arange(2 * NAT_ROWS - 1)).astype(F32)
    csel = ((cidx[:, :, None] == jnp.arange(2 * NAT_COLS - 1)) & inwin[:, :, None]).astype(F32)
    tab = jnp.einsum('hab,dia,ckb->hdcik', rpb.astype(F32), rsel, csel, precision=lax.Precision.HIGHEST)
    tab = jnp.where(inwin[None, None, :, None, :], tab, NAT_MASKED)
    return tab.reshape(rpb.shape[0], NAT_ROWS, GRID_W, NAT_WIN)


def nat_attention(q, k, v, kc, vc, rpb, qn_g, kn_g, interpret=False):
    B, L, D = q.shape
    Lc = kc.shape[1]
    rows = L // GRID_W
    P = D // PAIR
    rpb_blk = NAT_ROWS_PER_BLOCK
    assert rows >= NAT_ROWS and rows % rpb_blk == 0 and L % 512 == 0
    bias = _nat_bias_table(rpb)
    g2 = lambda g: jnp.tile(g, 2).reshape(1, PAIR)
    full = pl.BlockSpec((1, L, PAIR), lambda b, p, i: (b, 0, p))
    cfull = pl.BlockSpec((1, Lc, PAIR), lambda b, p, i: (b, 0, p))
    qblk = pl.BlockSpec((1, rpb_blk * GRID_W, PAIR), lambda b, p, i: (b, i, p))
    gspec = pl.BlockSpec((1, PAIR), lambda b, p, i: (0, 0))
    return pl.pallas_call(
        functools.partial(_nat_body, rows=rows),
        grid=(B, P, rows // rpb_blk),
        in_specs=[qblk, full, full, cfull, cfull,
                  pl.BlockSpec((2, NAT_ROWS, GRID_W, NAT_WIN), lambda b, p, i: (p, 0, 0, 0)), gspec, gspec],
        out_specs=qblk,
        out_shape=jax.ShapeDtypeStruct((B, L, D), F32),
        scratch_shapes=[pltpu.VMEM((L, PAIR), BF16), pltpu.VMEM((L, PAIR), BF16),
                        pltpu.VMEM((Lc, PAIR), BF16), pltpu.VMEM((Lc, PAIR), BF16)],
        compiler_params=pltpu.CompilerParams(dimension_semantics=("parallel", "parallel", "arbitrary"),
                                             vmem_limit_bytes=48 * 1024 * 1024),
        name="nat_attention",
        interpret=interpret,
    )(q, k, v, kc, vc, bias, g2(qn_g), g2(kn_g))


GLA_SUB = 16
GLA_BLOCK = 256
GLA_KPAIR = 2 * GLA_DK
GLA_VPAIR = 2 * GLA_DV
GLA_UNROLL = 2


def _gla_body(q_ref, k_ref, v_ref, g_ref, s0_ref, o_ref, sT_ref, st_scr, *, reverse):
    C = GLA_SUB
    nsub = GLA_BLOCK // C
    npair = st_scr.shape[0]
    blk = pl.program_id(1)

    @pl.when(blk == 0)
    def _():
        st_scr[...] = s0_ref[0]

    ti = lax.broadcasted_iota(jnp.int32, (C, C), 0)
    si = lax.broadcasted_iota(jnp.int32, (C, C), 1)
    tri = ((si >= ti) if reverse else (si <= ti)).astype(F32)
    lane_lo = lax.broadcasted_iota(jnp.int32, (C, GLA_KPAIR), 1) < GLA_DK
    row_id = lax.broadcasted_iota(jnp.int32, (C, GLA_KPAIR), 0)
    vrow = lax.broadcasted_iota(jnp.int32, (2 * C, GLA_VPAIR), 0)
    vcol = lax.broadcasted_iota(jnp.int32, (2 * C, GLA_VPAIR), 1)
    v_same_head = (vrow // C) == (vcol // GLA_DV)
    srow = lax.broadcasted_iota(jnp.int32, (GLA_VPAIR, GLA_KPAIR), 0)
    scol = lax.broadcasted_iota(jnp.int32, (GLA_VPAIR, GLA_KPAIR), 1)
    s_same_head = (srow // GLA_DV) == (scol // GLA_DK)

    pairs = range(npair)
    klanes = [slice(p * GLA_KPAIR, (p + 1) * GLA_KPAIR) for p in pairs]
    vlanes = [slice(p * GLA_VPAIR, (p + 1) * GLA_VPAIR) for p in pairs]

    def sub_chunks(it, carry):
        chains = [(u, p) for u in range(GLA_UNROLL) for p in pairs]
        sl, q, k, g, v = {}, {}, {}, {}, {}
        for u in range(GLA_UNROLL):
            i = it * GLA_UNROLL + u
            ci = (nsub - 1 - i) if reverse else i
            sl[u] = pl.ds(pl.multiple_of(ci * C, C), C)
            for p in pairs:
                q[u, p], k[u, p], g[u, p] = (r[0, sl[u], klanes[p]] for r in (q_ref, k_ref, g_ref))
                v[u, p] = v_ref[0, sl[u], vlanes[p]]
        b = {c: jnp.dot(tri, g[c], preferred_element_type=F32, precision=lax.Precision.HIGHEST) for c in chains}
        b_end = {c: jnp.sum(g[c], axis=0, keepdims=True) for c in chains}
        a_lo = {c: jnp.zeros((C, C), F32) for c in chains}
        a_hi = {c: jnp.zeros((C, C), F32) for c in chains}
        for j in range(C):
            seen = (row_id <= j) if reverse else (row_id >= j)
            for c in chains:
                decay = jnp.exp(jnp.where(seen, b[c] - b[c][j:j + 1, :], 0.0))
                f = jnp.where(seen, q[c] * k[c][j:j + 1, :] * decay, 0.0)
                r_lo = jnp.sum(jnp.where(lane_lo, f, 0.0), axis=-1, keepdims=True)
                r_hi = jnp.sum(jnp.where(lane_lo, 0.0, f), axis=-1, keepdims=True)
                a_lo[c] = jnp.where(si == j, r_lo, a_lo[c])
                a_hi[c] = jnp.where(si == j, r_hi, a_hi[c])
        v_bd = {c: jnp.where(v_same_head, jnp.concatenate([v[c], v[c]], axis=0), 0.0) for c in chains}
        o_in = {c: _mm(jnp.concatenate([a_lo[c], a_hi[c]], axis=1), v_bd[c]) for c in chains}
        kv = {c: jnp.where(s_same_head, _mm_tn(v[c], k[c] * jnp.exp(b_end[c] - b[c])), 0.0) for c in chains}
        qd = {c: q[c] * jnp.exp(b[c]) for c in chains}
        e_end = {c: jnp.exp(b_end[c]) for c in chains}
        st = [st_scr[p] for p in pairs]
        for u in range(GLA_UNROLL):
            for p in pairs:
                o_ref[0, sl[u], vlanes[p]] = o_in[u, p] + _mm_nt(qd[u, p], st[p])
                st[p] = st[p] * e_end[u, p] + kv[u, p]
        for p in pairs:
            st_scr[p] = st[p]
        return carry

    lax.fori_loop(0, nsub // GLA_UNROLL, sub_chunks, 0)

    @pl.when(blk == pl.num_programs(1) - 1)
    def _():
        sT_ref[0] = st_scr[...]


def gla_scan(q, k, v, logg, st0, *, reverse, interpret=False):
    B, L, Dk = q.shape
    Dv = v.shape[-1]
    T = GLA_BLOCK
    assert L % T == 0 and Dk % GLA_KPAIR == 0
    n, P = L // T, Dk // GLA_KPAIR
    bidx = (lambda c: n - 1 - c) if reverse else (lambda c: c)
    kspec = pl.BlockSpec((1, T, Dk), lambda b, c: (b, bidx(c), 0))
    vspec = pl.BlockSpec((1, T, Dv), lambda b, c: (b, bidx(c), 0))
    sspec = pl.BlockSpec((1, P, GLA_VPAIR, GLA_KPAIR), lambda b, c: (b, 0, 0, 0))
    return pl.pallas_call(
        functools.partial(_gla_body, reverse=reverse),
        grid=(B, n),
        in_specs=[kspec, kspec, vspec, kspec, sspec],
        out_specs=[vspec, sspec],
        out_shape=[jax.ShapeDtypeStruct((B, L, Dv), F32),
                   jax.ShapeDtypeStruct((B, P, GLA_VPAIR, GLA_KPAIR), F32)],
        scratch_shapes=[pltpu.VMEM((P, GLA_VPAIR, GLA_KPAIR), F32)],
        compiler_params=pltpu.CompilerParams(dimension_semantics=("parallel", "arbitrary")),
        name="gla_scan_rev" if reverse else "gla_scan_fwd",
        interpret=interpret,
    )(q, k, v, logg, st0)


MOE_F_TILE = 512


def _expert_ffn_body(x_ref, g_ref, w1_ref, w3_ref, w2_ref, o_ref, acc_ref):
    f = pl.program_id(3)

    @pl.when(f == 0)
    def _():
        acc_ref[...] = jnp.zeros_like(acc_ref)

    x = x_ref[0, 0].astype(BF16)
    h1 = jnp.dot(x, w1_ref[0], preferred_element_type=F32)
    h3 = jnp.dot(x, w3_ref[0], preferred_element_type=F32)
    hid = (h1 * jax.nn.sigmoid(h1) * h3).astype(BF16)
    acc_ref[...] += jnp.dot(hid, w2_ref[0], preferred_element_type=F32)

    @pl.when(f == pl.num_programs(3) - 1)
    def _():
        o_ref[0, 0] = acc_ref[...] * g_ref[0, 0]


def expert_ffn(xin, gate, w1, w3, w2, interpret=False):
    B, E, cap, D = xin.shape
    F = w1.shape[-1]
    tm = min(cap, 1024)
    tf = MOE_F_TILE
    assert cap % tm == 0 and F % tf == 0
    return pl.pallas_call(
        _expert_ffn_body,
        grid=(E, B, cap // tm, F // tf),
        in_specs=[
            pl.BlockSpec((1, 1, tm, D), lambda e, b, i, f: (b, e, i, 0)),
            pl.BlockSpec((1, 1, tm, 1), lambda e, b, i, f: (b, e, i, 0)),
            pl.BlockSpec((1, D, tf), lambda e, b, i, f: (e, 0, f)),
            pl.BlockSpec((1, D, tf), lambda e, b, i, f: (e, 0, f)),
            pl.BlockSpec((1, tf, D), lambda e, b, i, f: (e, f, 0)),
        ],
        out_specs=pl.BlockSpec((1, 1, tm, D), lambda e, b, i, f: (b, e, i, 0)),
        out_shape=jax.ShapeDtypeStruct((B, E, cap, D), F32),
        scratch_shapes=[pltpu.VMEM((tm, D), F32)],
        compiler_params=pltpu.CompilerParams(
            dimension_semantics=("parallel", "parallel", "parallel", "arbitrary"),
            vmem_limit_bytes=48 * 1024 * 1024),
        name="expert_ffn",
        interpret=interpret,
    )(xin, gate[..., None], w1.astype(BF16), w3.astype(BF16), w2.astype(BF16))


def _split(p, sizes):
    return jnp.split(p, np.cumsum(sizes)[:-1].tolist(), axis=-1)


def rmsnorm(x, g, eps=NORM_EPS):
    xf = x.astype(F32)
    y = xf * lax.rsqrt(jnp.mean(xf * xf, axis=-1, keepdims=True) + eps)
    return (y * g.astype(F32)).astype(x.dtype)


def modulate(x, g, shift, scale):
    return rmsnorm(x, g) * (1 + scale) + shift


def conv3(u, w):
    up = jnp.pad(u, ((0, 0), (1, 1), (0, 0)))
    return up[:, :-2] * w[0] + up[:, 1:-1] * w[1] + up[:, 2:] * w[2]


def rope_2d(x):
    L, d = x.shape[1], x.shape[-1]
    half, nf = d // 2, d // 4
    t = jnp.arange(L)
    inv = ROPE_BASE ** (-jnp.arange(nf, dtype=F32) / nf)

    def rot(u, pos):
        ang = pos.astype(F32)[:, None] * inv[None, :]
        cos, sin = jnp.cos(ang)[None, :, None, :], jnp.sin(ang)[None, :, None, :]
        u1, u2 = u[..., :nf].astype(F32), u[..., nf:].astype(F32)
        return jnp.concatenate([u1 * cos - u2 * sin, u1 * sin + u2 * cos], axis=-1)

    return jnp.concatenate([rot(x[..., :half], t // GRID_W), rot(x[..., half:], t % GRID_W)], axis=-1).astype(x.dtype)


def ec_moe(h, router, w1, w3, w2):
    B, T, D = h.shape
    cap = EC_CAPACITY_FACTOR * T // N_EXPERTS
    aff = jax.nn.softmax((h @ router).astype(F32), axis=-1)
    gate, idx = lax.top_k(jnp.swapaxes(aff, 1, 2), cap)
    xin = jax.vmap(lambda hb, ib: hb[ib])(h, idx)
    if B * cap <= 1024:
        merge = lambda t: jnp.swapaxes(t, 0, 1).reshape((1, N_EXPERTS, B * cap) + t.shape[3:])
        y = expert_ffn(merge(xin), merge(gate), w1, w3, w2)
        y = jnp.swapaxes(y.reshape(N_EXPERTS, B, cap, D), 0, 1)
    else:
        y = expert_ffn(xin, gate, w1, w3, w2)
    out = jax.vmap(lambda ib, yb: jnp.zeros((T, D), yb.dtype).at[ib.reshape(-1)].add(yb.reshape(-1, D)))(idx, y)
    return out.astype(h.dtype)


def rwkv_readout(y, r, k, v, xg, r_k, g_up, ln_g, ln_b):
    B, L = r.shape[:2]
    hs = lambda t: t.reshape(B, L, RWKV_HEADS, RWKV_HEAD)
    yf = hs(y.astype(F32))
    mu = jnp.mean(yf, axis=-1, keepdims=True)
    var = jnp.mean(jnp.square(yf - mu), axis=-1, keepdims=True)
    yn = ((yf - mu) * lax.rsqrt(var + RWKV_LN_EPS)).reshape(B, L, D_RWKV) * ln_g + ln_b
    bonus = (jnp.sum(hs(r) * hs(k) * r_k, axis=-1, keepdims=True) * hs(v)).reshape(B, L, D_RWKV)
    g = jax.nn.sigmoid(xg) @ g_up
    return ((yn + bonus) * g).astype(r.dtype)


def even_mixer(h_lat, h_ctx, w_in, conv_w, k_k, k_a, r_k, w0, w_up, a0, a_up, g_up, ln_g, ln_b, need_ctx):
    B = h_lat.shape[0]
    p_lat = _split(h_lat @ w_in, EVEN_SPLIT)
    p_ctx = _split(h_ctx @ w_in, EVEN_SPLIT)

    def conv_branch(u, gate_b, gate_c):
        return gate_b * conv3(gate_c * u, conv_w)

    def unit_key(k):
        Bq, L = k.shape[:2]
        kkf = (k * k_k).astype(F32).reshape(Bq, L, RWKV_HEADS, RWKV_HEAD)
        return (kkf * lax.rsqrt(jnp.sum(kkf * kkf, axis=-1, keepdims=True) + 1e-12)).reshape(Bq, L, D_RWKV)

    def dir_gates(xw, xa, d):
        w_raw = (w0[d] + jnp.tanh(xw) @ w_up[d]).astype(F32)
        return -jnp.exp(-jax.nn.softplus(-w_raw) - 0.5), jax.nn.sigmoid(a0[d] + xa @ a_up[d])

    kk_ctx, kk_lat = unit_key(p_ctx[4]), unit_key(p_lat[4])
    st0 = jnp.zeros((B, RWKV_HEADS // 2, PAIR, PAIR), F32)
    y_lat, y_ctx = 0.0, 0.0
    for d, rev in ((0, False), (1, True)):
        yc, st_c = rwkv_chunked(*p_ctx[3:6], kk_ctx, *dir_gates(*p_ctx[6:8], d), k_a, st0, reverse=rev)
        yl, _ = rwkv_chunked(*p_lat[3:6], kk_lat, *dir_gates(*p_lat[6:8], d), k_a, st_c, reverse=rev)
        y_lat = y_lat + yl
        if need_ctx:
            y_ctx = y_ctx + yc
    cat_lat = jnp.concatenate([conv_branch(*p_lat[0:3]),
                               rwkv_readout(y_lat, *p_lat[3:6], p_lat[8], r_k, g_up, ln_g, ln_b)], axis=-1)
    cat_ctx = None
    if need_ctx:
        cat_ctx = jnp.concatenate([conv_branch(*p_ctx[0:3]),
                                   rwkv_readout(y_ctx, *p_ctx[3:6], p_ctx[8], r_k, g_up, ln_g, ln_b)], axis=-1)
    return cat_lat, cat_ctx


def ctx_attention(q, k, v):
    s = jnp.einsum('bhqd,bhkd->bhqk', q, k).astype(F32) * (q.shape[-1] ** -0.5)
    p = jax.nn.softmax(s, axis=-1).astype(v.dtype)
    return jnp.einsum('bhqk,bhkd->bhqd', p, v)


def gla_log_gate(ga, a_up_d, a_b_d):
    B, L = ga.shape[:2]
    lg = jax.nn.log_sigmoid((ga @ a_up_d + a_b_d).astype(F32)) / GLA_GATE_TEMP
    return lg.reshape(B, L, GLA_HEADS, GLA_DK)


def gla_readout(o, gr, ln_g):
    B, L = gr.shape[:2]
    return (rmsnorm(o, ln_g).reshape(B, L, D_GLA_V) * jax.nn.silu(gr)).astype(gr.dtype)


def odd_mixer(h_lat, h_ctx, w_in, qn_g, kn_g, rpb, a_up, a_b, gla_ln_g, need_ctx):
    B, L = h_lat.shape[:2]
    Lc = h_ctx.shape[1]
    nq, nk, nv, gq, gk, gv, gr, ga = _split(h_lat @ w_in, ODD_SPLIT)
    cnq, cnk, cnv, cgq, cgk, cgv, cgr, cga = _split(h_ctx @ w_in, ODD_SPLIT)

    def nat_heads(t, g=None):
        t = t.reshape(t.shape[0], t.shape[1], NAT_HEADS, NAT_HEAD)
        if g is not None:
            t = rmsnorm(t, g)
        return jnp.swapaxes(t, 1, 2)

    nat_lat = nat_attention(nq, nk, nv, cnk, cnv, rpb, qn_g, kn_g)

    gh = lambda t, d: t.reshape(t.shape[0], t.shape[1], GLA_HEADS, d)
    qscale = GLA_DK ** -0.5
    q = (rope_2d(gh(gq, GLA_DK)) * qscale).reshape(B, L, D_GLA_K)
    k = rope_2d(gh(gk, GLA_DK)).reshape(B, L, D_GLA_K)
    qc = cgq * qscale
    st0 = jnp.zeros((B, GLA_HEADS // 2, GLA_VPAIR, GLA_KPAIR), F32)
    o_lat, o_ctx = 0.0, 0.0
    for d in range(2):
        lg_c = gla_log_gate(cga, a_up[d], a_b[d]).reshape(B, Lc, D_GLA_K)
        lg_l = gla_log_gate(ga, a_up[d], a_b[d]).reshape(B, L, D_GLA_K)
        oc, st_c = gla_scan(qc, cgk, cgv, lg_c, st0, reverse=(d == 1))
        ol, _ = gla_scan(q, k, gv, lg_l, st_c, reverse=(d == 1))
        o_lat = o_lat + ol
        if need_ctx:
            o_ctx = o_ctx + oc
    o_lat = gh(o_lat, GLA_DV)
    if need_ctx:
        o_ctx = gh(o_ctx, GLA_DV)
    cat_lat = jnp.concatenate([nat_lat, gla_readout(o_lat, gr, gla_ln_g)], axis=-1)
    cat_ctx = None
    if need_ctx:
        kc, vc = nat_heads(cnk, kn_g), nat_heads(cnv)
        nat_ctx = jnp.swapaxes(ctx_attention(nat_heads(cnq, qn_g), kc, vc), 1, 2).reshape(B, Lc, D_NAT)
        cat_ctx = jnp.concatenate([nat_ctx, gla_readout(o_ctx, cgr, gla_ln_g)], axis=-1)
    return cat_lat, cat_ctx


def kernel(x, c, ctx, c_ctx, ada_w, ada_b, norm1_g, norm2_g, ev_w_in, ev_w_out, conv_w, rw_k_k, rw_k_a, rw_r_k, rw_w0, rw_w_up, rw_a0, rw_a_up, rw_g_up, rw_ln_g, rw_ln_b, od_w_in, od_w_out, nat_qn_g, nat_kn_g, nat_rpb, gla_a_up, gla_a_b, gla_ln_g, moe_router, moe_w1, moe_w3, moe_w2):
    depth = ada_w.shape[0]
    ctx_s = ctx
    silu_c = jax.nn.silu(c)
    silu_cc = jax.nn.silu(c_ctx)
    for l in range(depth):
        last = l == depth - 1
        j = l // 2
        sh1, sc1, gt1, sh2, sc2, gt2 = _split((silu_c @ ada_w[l] + ada_b[l])[:, None, :], [D_MODEL] * 6)
        csh1, csc1, cgt1, csh2, csc2, cgt2 = _split(silu_cc @ ada_w[l] + ada_b[l], [D_MODEL] * 6)
        h_lat = modulate(x, norm1_g[l], sh1, sc1)
        h_ctx = modulate(ctx_s, norm1_g[l], csh1, csc1)
        if l % 2 == 0:
            cat_lat, cat_ctx = even_mixer(h_lat, h_ctx, ev_w_in[j], conv_w[j], rw_k_k[j], rw_k_a[j], rw_r_k[j],
                                          rw_w0[j], rw_w_up[j], rw_a0[j], rw_a_up[j], rw_g_up[j], rw_ln_g[j],
                                          rw_ln_b[j], not last)
            w_out = ev_w_out[j]
        else:
            cat_lat, cat_ctx = odd_mixer(h_lat, h_ctx, od_w_in[j], nat_qn_g[j], nat_kn_g[j], nat_rpb[j],
                                         gla_a_up[j], gla_a_b[j], gla_ln_g[j], not last)
            w_out = od_w_out[j]
        experts = (moe_w1[l].astype(BF16), moe_w3[l].astype(BF16), moe_w2[l].astype(BF16))
        x = out_proj_residual(cat_lat, w_out, x, gt1)
        x = x + gt2 * ec_moe(modulate(x, norm2_g[l], sh2, sc2), moe_router[l], *experts)
        if not last:
            ctx_s = out_proj_residual(cat_ctx, w_out, ctx_s, cgt1)
            ctx_s = ctx_s + cgt2 * ec_moe(modulate(ctx_s, norm2_g[l], csh2, csc2), moe_router[l], *experts)
    return x
```

```python
import functools

import jax
import jax.numpy as jnp
import numpy as np
from jax import lax
from jax.experimental import pallas as pl
from jax.experimental.pallas import tpu as pltpu

D_MODEL = 1024
GRID_W = 64
NORM_EPS = 1e-6
F32 = jnp.float32
BF16 = jnp.bfloat16

D_CONV = 512
D_RWKV = 512
RWKV_HEAD = 64
RWKV_HEADS = D_RWKV // RWKV_HEAD
RWKV_DECAY_RANK = 64
RWKV_ICLR_RANK = 64
RWKV_GATE_RANK = 128
RWKV_LN_EPS = 64e-5
D_NAT = 512
NAT_HEAD = 64
NAT_HEADS = D_NAT // NAT_HEAD
NAT_ROWS = 8
NAT_COLS = 16
GLA_HEADS = 4
GLA_DK = 64
GLA_DV = 128
D_GLA_K = GLA_HEADS * GLA_DK
D_GLA_V = GLA_HEADS * GLA_DV
GLA_GATE_RANK = 16
GLA_GATE_TEMP = 16.0
GLA_CHUNK = 64
ROPE_BASE = 10000.0
N_EXPERTS = 16
EC_CAPACITY_FACTOR = 2

EVEN_SPLIT = [D_CONV, D_CONV, D_CONV, D_RWKV, D_RWKV, D_RWKV, RWKV_DECAY_RANK, RWKV_ICLR_RANK, RWKV_GATE_RANK]
ODD_SPLIT = [D_NAT, D_NAT, D_NAT, D_GLA_K, D_GLA_K, D_GLA_V, D_GLA_V, GLA_GATE_RANK]


def _out_proj_body(a_ref, w_ref, x_ref, g_ref, o_ref):
    acc = jnp.dot(a_ref[0].astype(BF16), w_ref[...], preferred_element_type=F32)
    o_ref[0] = x_ref[0] + g_ref[0] * acc


def out_proj_residual(a, w, x, gate, block_rows=512):
    B, L, K = a.shape
    N = w.shape[1]
    tm = min(block_rows, L)
    assert L % tm == 0
    gate = jnp.broadcast_to(gate.reshape(-1, 1, N), (B, 1, N))
    return pl.pallas_call(
        _out_proj_body,
        grid=(B, L // tm),
        in_specs=[
            pl.BlockSpec((1, tm, K), lambda b, i: (b, i, 0)),
            pl.BlockSpec((K, N), lambda b, i: (0, 0)),
            pl.BlockSpec((1, tm, N), lambda b, i: (b, i, 0)),
            pl.BlockSpec((1, 1, N), lambda b, i: (b, 0, 0)),
        ],
        out_specs=pl.BlockSpec((1, tm, N), lambda b, i: (b, i, 0)),
        out_shape=jax.ShapeDtypeStruct((B, L, N), F32),
        compiler_params=pltpu.CompilerParams(dimension_semantics=("parallel", "parallel")),
        name="out_proj_residual",
    )(a, w.astype(BF16), x, gate)


def _in_proj_body(x_ref, g_ref, shift_ref, scale_ref, w_ref, *o_refs, offsets):
    xf = x_ref[0]
    y = xf * lax.rsqrt(jnp.mean(xf * xf, axis=-1, keepdims=True) + NORM_EPS) * g_ref[...]
    h = (y * (1.0 + scale_ref[0]) + shift_ref[0]).astype(BF16)
    for o_ref, off in zip(o_refs, offsets):
        o_ref[0] = jnp.dot(h, w_ref[:, off:off + o_ref.shape[-1]], preferred_element_type=F32)


def in_proj_split(x, g, shift, scale, w, sizes, block_rows=512):
    B, L, D = x.shape
    N = w.shape[1]
    assert sum(sizes) == N
    tm = min(block_rows, L)
    assert L % tm == 0
    offsets = tuple(int(o) for o in np.cumsum([0] + list(sizes[:-1])))
    per_sample = lambda t: jnp.broadcast_to(t.reshape(-1, 1, D), (B, 1, D))
    vec = pl.BlockSpec((1, 1, D), lambda b, i: (b, 0, 0))
    return pl.pallas_call(
        functools.partial(_in_proj_body, offsets=offsets),
        grid=(B, L // tm),
        in_specs=[pl.BlockSpec((1, tm, D), lambda b, i: (b, i, 0)), pl.BlockSpec((1, D), lambda b, i: (0, 0)),
                  vec, vec, pl.BlockSpec((D, N), lambda b, i: (0, 0))],
        out_specs=[pl.BlockSpec((1, tm, s), lambda b, i: (b, i, 0)) for s in sizes],
        out_shape=[jax.ShapeDtypeStruct((B, L, s), F32) for s in sizes],
        compiler_params=pltpu.CompilerParams(dimension_semantics=("parallel", "parallel"),
                                             vmem_limit_bytes=48 * 1024 * 1024),
        name="in_proj_split",
    )(x, g.reshape(1, D), per_sample(shift), per_sample(scale), w.astype(BF16))


RWKV_CHUNK = 64
PAIR = 2 * RWKV_HEAD


def _mm(a, b):
    return jnp.dot(a.astype(BF16), b.astype(BF16), preferred_element_type=F32)


def _mm_nt(a, b):
    return lax.dot_general(a.astype(BF16), b.astype(BF16), (((1,), (1,)), ((), ())), preferred_element_type=F32)


def _mm_tn(a, b):
    return lax.dot_general(a.astype(BF16), b.astype(BF16), (((0,), (0,)), ((), ())), preferred_element_type=F32)


def _rwkv_chunk_pairs(ins, sts, *, reverse):
    C = RWKV_CHUNK
    row = lax.broadcasted_iota(jnp.int32, (PAIR, PAIR), 0)
    col = lax.broadcasted_iota(jnp.int32, (PAIR, PAIR), 1)
    same_head = (row // RWKV_HEAD) == (col // RWKV_HEAD)
    t_i, s_i = row % C, col % C
    before = (s_i > t_i) if reverse else (s_i < t_i)
    upto = before | (s_i == t_i)

    ct = lax.broadcasted_iota(jnp.int32, (C, C), 0)
    cs = lax.broadcasted_iota(jnp.int32, (C, C), 1)
    tri = ((cs >= ct) if reverse else (cs <= ct)).astype(F32)
    eye = jnp.where(row == col, 1.0, 0.0)

    def expand(x):
        return jnp.where(same_head, jnp.concatenate([x, x], axis=0), 0.0)

    def prepare(r, k, v, kk, lw, a, ka):
        keff = k * (1.0 + (a - 1.0) * ka)
        b = kk * a
        cum_in = jnp.dot(tri, lw, preferred_element_type=F32, precision=lax.Precision.HIGHEST)
        cum_ex = cum_in - lw
        tot = jnp.sum(lw, axis=0, keepdims=True)
        e_neg = jnp.exp(-cum_in)
        e_rem = jnp.exp(tot - cum_in)
        return dict(at2=expand(-kk * jnp.exp(cum_ex)), rt2=expand(r * jnp.exp(cum_in)),
                    bh2=expand(b * e_neg), kh2=expand(keff * e_neg), bp2=expand(b * e_rem),
                    kp2=expand(keff * e_rem), v2=expand(v), dtot=eye * jnp.exp(tot))

    ps = [prepare(*args) for args in ins]
    pps = [_mm_nt(jnp.concatenate([p["at2"], p["rt2"]], axis=0), jnp.concatenate([p["bh2"], p["kh2"]], axis=0))
           for p in ps]
    a_ab = [jnp.where(before, pp[:PAIR, :PAIR], 0.0) for pp in pps]
    a_ak = [jnp.where(before, pp[:PAIR, PAIR:], 0.0) for pp in pps]
    a_rb = [jnp.where(upto, pp[PAIR:, :PAIR], 0.0) for pp in pps]
    a_rk = [jnp.where(upto, pp[PAIR:, PAIR:], 0.0) for pp in pps]

    tinv = [eye + n for n in a_ab]
    npow = a_ab
    for _ in range(int(np.log2(C)) - 1):
        npow = [_mm(n, n) for n in npow]
        tinv = [t + _mm(t, n) for t, n in zip(tinv, npow)]

    av = [_mm(jnp.concatenate([ak, rk], axis=0), p["v2"]) for ak, rk, p in zip(a_ak, a_rk, ps)]
    x = [_mm(t, jnp.concatenate([p["at2"], w[:PAIR]], axis=1)) for t, p, w in zip(tinv, ps, av)]
    z = [_mm(rb, xx) for rb, xx in zip(a_rb, x)]
    bx = [_mm_tn(p["bp2"], xx) for p, xx in zip(ps, x)]
    kv = [_mm_tn(p["kp2"], p["v2"]) for p in ps]
    y2 = [_mm(p["rt2"] + zz[:, :PAIR], st) + zz[:, PAIR:] + w[PAIR:] for p, zz, st, w in zip(ps, z, sts, av)]
    st_new = [_mm(p["dtot"] + b[:, :PAIR], st) + b[:, PAIR:] + g for p, b, st, g in zip(ps, bx, sts, kv)]
    return [y[:C] + y[C:] for y in y2], st_new


def _rwkv_chunk_body(r_ref, k_ref, v_ref, kk_ref, lw_ref, a_ref, ka_ref, s0_ref, y_ref, sT_ref, st_scr, *, reverse):
    c = pl.program_id(1)

    @pl.when(c == 0)
    def _():
        st_scr[...] = s0_ref[0]

    npair = st_scr.shape[0]
    lanes = [slice(p * PAIR, (p + 1) * PAIR) for p in range(npair)]
    ins = [(r_ref[0, :, sl], k_ref[0, :, sl], v_ref[0, :, sl], kk_ref[0, :, sl], lw_ref[0, :, sl], a_ref[0, :, sl],
            ka_ref[:, sl]) for sl in lanes]
    ys, sts = _rwkv_chunk_pairs(ins, [st_scr[p] for p in range(npair)], reverse=reverse)
    for p, sl in enumerate(lanes):
        y_ref[0, :, sl] = ys[p]
        st_scr[p] = sts[p]

    @pl.when(c == pl.num_programs(1) - 1)
    def _():
        sT_ref[0] = st_scr[...]


def rwkv_state_pack(S):
    B, H = S.shape[:2]
    St = jnp.swapaxes(S, -1, -2).reshape(B, H // 2, 2, RWKV_HEAD, RWKV_HEAD)
    z = jnp.zeros_like(St[:, :, 0])
    top = jnp.concatenate([St[:, :, 0], z], axis=-1)
    bot = jnp.concatenate([z, St[:, :, 1]], axis=-1)
    return jnp.concatenate([top, bot], axis=-2)


def rwkv_state_unpack(St):
    B, P = St.shape[:2]
    h0 = St[:, :, :RWKV_HEAD, :RWKV_HEAD]
    h1 = St[:, :, RWKV_HEAD:, RWKV_HEAD:]
    return jnp.swapaxes(jnp.stack([h0, h1], axis=2).reshape(B, 2 * P, RWKV_HEAD, RWKV_HEAD), -1, -2)


def rwkv_chunked(r, k, v, kk, lw, a, k_a, st0, *, reverse, interpret=False):
    B, L, D = r.shape
    C = RWKV_CHUNK
    assert L % C == 0 and D % PAIR == 0
    n, P = L // C, D // PAIR
    cidx = (lambda c: n - 1 - c) if reverse else (lambda c: c)
    seq = pl.BlockSpec((1, C, D), lambda b, c: (b, cidx(c), 0))
    state = pl.BlockSpec((1, P, PAIR, PAIR), lambda b, c: (b, 0, 0, 0))
    return pl.pallas_call(
        functools.partial(_rwkv_chunk_body, reverse=reverse),
        grid=(B, n),
        in_specs=[seq] * 6 + [pl.BlockSpec((1, D), lambda b, c: (0, 0)), state],
        out_specs=[seq, state],
        out_shape=[jax.ShapeDtypeStruct((B, L, D), F32), jax.ShapeDtypeStruct((B, P, PAIR, PAIR), F32)],
        scratch_shapes=[pltpu.VMEM((P, PAIR, PAIR), F32)],
        compiler_params=pltpu.CompilerParams(dimension_semantics=("parallel", "arbitrary")),
        name="rwkv_chunked_rev" if reverse else "rwkv_chunked_fwd",
        interpret=interpret,
    )(r, k, v, kk, lw, a, k_a.reshape(1, D), st0)


NAT_WIN = NAT_ROWS * GRID_W
NAT_ROWS_PER_BLOCK = 16
NAT_MASKED = -1e30


def _pair_rmsnorm(x, g, lane_lo):
    sq = x * x
    s_lo = jnp.sum(jnp.where(lane_lo, sq, 0.0), axis=-1, keepdims=True)
    s_hi = jnp.sum(sq, axis=-1, keepdims=True) - s_lo
    ms = jnp.where(lane_lo, s_lo, s_hi) * (1.0 / NAT_HEAD)
    return x * lax.rsqrt(ms + NORM_EPS) * g


def _nat_body(q_ref, k_ref, v_ref, kc_ref, vc_ref, bias_ref, qg_ref, kg_ref, o_ref, kn_scr, vb_scr, kcn_scr, vcb_scr,
              *, rows):
    rb = pl.program_id(2)
    L = k_ref.shape[1]
    norm_rows = 512
    lane_lo_n = lax.broadcasted_iota(jnp.int32, (norm_rows, PAIR), 1) < NAT_HEAD

    @pl.when(rb == 0)
    def _():
        def norm_block(i, carry):
            sl = pl.ds(pl.multiple_of(i * norm_rows, norm_rows), norm_rows)
            kn_scr[sl, :] = _pair_rmsnorm(k_ref[0, sl, :], kg_ref[...], lane_lo_n).astype(BF16)
            vb_scr[sl, :] = v_ref[0, sl, :].astype(BF16)
            return carry
        lax.fori_loop(0, L // norm_rows, norm_block, 0)
        lane_lo_c = lax.broadcasted_iota(jnp.int32, kc_ref.shape[1:], 1) < NAT_HEAD
        kcn_scr[...] = _pair_rmsnorm(kc_ref[0], kg_ref[...], lane_lo_c).astype(BF16)
        vcb_scr[...] = vc_ref[0].astype(BF16)

    lane_lo = lax.broadcasted_iota(jnp.int32, (GRID_W, PAIR), 1) < NAT_HEAD
    scale = NAT_HEAD ** -0.5

    rows_per_iter = 2
    nt = (((1,), (1,)), ((), ()))

    def row_group(it, carry):
        qsl, kwin, vwin, qh, bias = [], [], [], [], []
        for u in range(rows_per_iter):
            j = it * rows_per_iter + u
            r = rb * NAT_ROWS_PER_BLOCK + j
            rs = jnp.clip(r - NAT_ROWS // 2, 0, rows - NAT_ROWS)
            sl = pl.ds(pl.multiple_of(j * GRID_W, GRID_W), GRID_W)
            wsl = pl.ds(pl.multiple_of(rs * GRID_W, GRID_W), NAT_WIN)
            qn = _pair_rmsnorm(q_ref[0, sl, :], qg_ref[...], lane_lo) * scale
            for h in range(2):
                qsl.append(sl)
                kwin.append(kn_scr[wsl, :])
                vwin.append(vb_scr[wsl, :])
                qh.append(jnp.where(lane_lo if h == 0 else ~lane_lo, qn, 0.0).astype(BF16))
                bias.append(bias_ref[h, r - rs])
        chains = range(2 * rows_per_iter)
        s_loc = [lax.dot_general(qh[i], kwin[i], nt, preferred_element_type=F32) + bias[i] for i in chains]
        s_ctx = [lax.dot_general(qh[i], kcn_scr[...], nt, preferred_element_type=F32) for i in chains]
        m = [jnp.maximum(jnp.max(s_loc[i], axis=-1, keepdims=True), jnp.max(s_ctx[i], axis=-1, keepdims=True))
             for i in chains]
        p_loc = [jnp.exp(s_loc[i] - m[i]) for i in chains]
        p_ctx = [jnp.exp(s_ctx[i] - m[i]) for i in chains]
        den = [jnp.sum(p_loc[i], axis=-1, keepdims=True) + jnp.sum(p_ctx[i], axis=-1, keepdims=True) for i in chains]
        o = [(jnp.dot(p_loc[i].astype(BF16), vwin[i], preferred_element_type=F32)
              + jnp.dot(p_ctx[i].astype(BF16), vcb_scr[...], preferred_element_type=F32)) / den[i] for i in chains]
        for u in range(rows_per_iter):
            o_ref[0, qsl[2 * u], :] = jnp.where(lane_lo, o[2 * u], o[2 * u + 1])
        return carry

    lax.fori_loop(0, NAT_ROWS_PER_BLOCK // rows_per_iter, row_group, 0)


def _nat_bias_table(rpb):
    col = jnp.arange(GRID_W)
    cstart = jnp.clip(col - NAT_COLS // 2, 0, GRID_W - NAT_COLS)
    delta = jnp.arange(NAT_ROWS)
    wrow = jnp.arange(NAT_ROWS)
    ridx = wrow[None, :] - delta[:, None] + (NAT_ROWS - 1)
    cidx = col[None, :] - col[:, None] + (NAT_COLS - 1)
    inwin = (col[None, :] >= cstart[:, None]) & (col[None, :] < cstart[:, None] + NAT_COLS)
    rsel = (ridx[:, :, None] == jnp.arange(2 * NAT_ROWS - 1)).astype(F32)
    csel = ((cidx[:, :, None] == jnp.arange(2 * NAT_COLS - 1)) & inwin[:, :, None]).astype(F32)
    tab = jnp.einsum('hab,dia,ckb->hdcik', rpb.astype(F32), rsel, csel, precision=lax.Precision.HIGHEST)
    tab = jnp.where(inwin[None, None, :, None, :], tab, NAT_MASKED)
    return tab.reshape(rpb.shape[0], NAT_ROWS, GRID_W, NAT_WIN)


def nat_attention(q, k, v, kc, vc, rpb, qn_g, kn_g, interpret=False):
    B, L, D = q.shape
    Lc = kc.shape[1]
    rows = L // GRID_W
    P = D // PAIR
    rpb_blk = NAT_ROWS_PER_BLOCK
    assert rows >= NAT_ROWS and rows % rpb_blk == 0 and L % 512 == 0
    bias = _nat_bias_table(rpb)
    g2 = lambda g: jnp.tile(g, 2).reshape(1, PAIR)
    full = pl.BlockSpec((1, L, PAIR), lambda b, p, i: (b, 0, p))
    cfull = pl.BlockSpec((1, Lc, PAIR), lambda b, p, i: (b, 0, p))
    qblk = pl.BlockSpec((1, rpb_blk * GRID_W, PAIR), lambda b, p, i: (b, i, p))
    gspec = pl.BlockSpec((1, PAIR), lambda b, p, i: (0, 0))
    return pl.pallas_call(
        functools.partial(_nat_body, rows=rows),
        grid=(B, P, rows // rpb_blk),
        in_specs=[qblk, full, full, cfull, cfull,
                  pl.BlockSpec((2, NAT_ROWS, GRID_W, NAT_WIN), lambda b, p, i: (p, 0, 0, 0)), gspec, gspec],
        out_specs=qblk,
        out_shape=jax.ShapeDtypeStruct((B, L, D), F32),
        scratch_shapes=[pltpu.VMEM((L, PAIR), BF16), pltpu.VMEM((L, PAIR), BF16),
                        pltpu.VMEM((Lc, PAIR), BF16), pltpu.VMEM((Lc, PAIR), BF16)],
        compiler_params=pltpu.CompilerParams(dimension_semantics=("parallel", "parallel", "arbitrary"),
                                             vmem_limit_bytes=48 * 1024 * 1024),
        name="nat_attention",
        interpret=interpret,
    )(q, k, v, kc, vc, bias, g2(qn_g), g2(kn_g))


GLA_SUB = 16
GLA_BLOCK = 256
GLA_KPAIR = 2 * GLA_DK
GLA_VPAIR = 2 * GLA_DV
GLA_UNROLL = 4


def _gla_body(q_ref, k_ref, v_ref, g_ref, s0_ref, o_ref, sT_ref, st_scr, *, reverse):
    C = GLA_SUB
    nsub = GLA_BLOCK // C
    npair = st_scr.shape[0]
    blk = pl.program_id(1)

    @pl.when(blk == 0)
    def _():
        st_scr[...] = s0_ref[0]

    ti = lax.broadcasted_iota(jnp.int32, (C, C), 0)
    si = lax.broadcasted_iota(jnp.int32, (C, C), 1)
    tri = ((si >= ti) if reverse else (si <= ti)).astype(F32)
    lane_lo = lax.broadcasted_iota(jnp.int32, (C, GLA_KPAIR), 1) < GLA_DK
    row_id = lax.broadcasted_iota(jnp.int32, (C, GLA_KPAIR), 0)
    vrow = lax.broadcasted_iota(jnp.int32, (2 * C, GLA_VPAIR), 0)
    vcol = lax.broadcasted_iota(jnp.int32, (2 * C, GLA_VPAIR), 1)
    v_same_head = (vrow // C) == (vcol // GLA_DV)
    srow = lax.broadcasted_iota(jnp.int32, (GLA_VPAIR, GLA_KPAIR), 0)
    scol = lax.broadcasted_iota(jnp.int32, (GLA_VPAIR, GLA_KPAIR), 1)
    s_same_head = (srow // GLA_DV) == (scol // GLA_DK)

    pairs = range(npair)
    klanes = [slice(p * GLA_KPAIR, (p + 1) * GLA_KPAIR) for p in pairs]
    vlanes = [slice(p * GLA_VPAIR, (p + 1) * GLA_VPAIR) for p in pairs]

    def sub_chunks(it, carry):
        chains = [(u, p) for u in range(GLA_UNROLL) for p in pairs]
        sl, q, k, g, v = {}, {}, {}, {}, {}
        for u in range(GLA_UNROLL):
            i = it * GLA_UNROLL + u
            ci = (nsub - 1 - i) if reverse else i
            sl[u] = pl.ds(pl.multiple_of(ci * C, C), C)
            for p in pairs:
                q[u, p], k[u, p], g[u, p] = (r[0, sl[u], klanes[p]] for r in (q_ref, k_ref, g_ref))
                v[u, p] = v_ref[0, sl[u], vlanes[p]]
        b = {c: jnp.dot(tri, g[c], preferred_element_type=F32, precision=lax.Precision.HIGHEST) for c in chains}
        b_end = {c: jnp.sum(g[c], axis=0, keepdims=True) for c in chains}
        a_lo = {c: jnp.zeros((C, C), F32) for c in chains}
        a_hi = {c: jnp.zeros((C, C), F32) for c in chains}
        for j in range(C):
            seen = (row_id <= j) if reverse else (row_id >= j)
            for c in chains:
                decay = jnp.exp(jnp.where(seen, b[c] - b[c][j:j + 1, :], 0.0))
                f = jnp.where(seen, q[c] * k[c][j:j + 1, :] * decay, 0.0)
                r_lo = jnp.sum(jnp.where(lane_lo, f, 0.0), axis=-1, keepdims=True)
                r_hi = jnp.sum(jnp.where(lane_lo, 0.0, f), axis=-1, keepdims=True)
                a_lo[c] = jnp.where(si == j, r_lo, a_lo[c])
                a_hi[c] = jnp.where(si == j, r_hi, a_hi[c])
        v_bd = {c: jnp.where(v_same_head, jnp.concatenate([v[c], v[c]], axis=0), 0.0) for c in chains}
        o_in = {c: _mm(jnp.concatenate([a_lo[c], a_hi[c]], axis=1), v_bd[c]) for c in chains}
        kv = {c: jnp.where(s_same_head, _mm_tn(v[c], k[c] * jnp.exp(b_end[c] - b[c])), 0.0) for c in chains}
        qd = {c: q[c] * jnp.exp(b[c]) for c in chains}
        e_end = {c: jnp.exp(b_end[c]) for c in chains}
        st = [st_scr[p] for p in pairs]
        for u in range(GLA_UNROLL):
            for p in pairs:
                o_ref[0, sl[u], vlanes[p]] = o_in[u, p] + _mm_nt(qd[u, p], st[p])
                st[p] = st[p] * e_end[u, p] + kv[u, p]
        for p in pairs:
            st_scr[p] = st[p]
        return carry

    lax.fori_loop(0, nsub // GLA_UNROLL, sub_chunks, 0)

    @pl.when(blk == pl.num_programs(1) - 1)
    def _():
        sT_ref[0] = st_scr[...]


def gla_scan(q, k, v, logg, st0, *, reverse, interpret=False):
    B, L, Dk = q.shape
    Dv = v.shape[-1]
    T = GLA_BLOCK
    assert L % T == 0 and Dk % GLA_KPAIR == 0
    n, P = L // T, Dk // GLA_KPAIR
    bidx = (lambda c: n - 1 - c) if reverse else (lambda c: c)
    kspec = pl.BlockSpec((1, T, Dk), lambda b, c: (b, bidx(c), 0))
    vspec = pl.BlockSpec((1, T, Dv), lambda b, c: (b, bidx(c), 0))
    sspec = pl.BlockSpec((1, P, GLA_VPAIR, GLA_KPAIR), lambda b, c: (b, 0, 0, 0))
    return pl.pallas_call(
        functools.partial(_gla_body, reverse=reverse),
        grid=(B, n),
        in_specs=[kspec, kspec, vspec, kspec, sspec],
        out_specs=[vspec, sspec],
        out_shape=[jax.ShapeDtypeStruct((B, L, Dv), F32),
                   jax.ShapeDtypeStruct((B, P, GLA_VPAIR, GLA_KPAIR), F32)],
        scratch_shapes=[pltpu.VMEM((P, GLA_VPAIR, GLA_KPAIR), F32)],
        compiler_params=pltpu.CompilerParams(dimension_semantics=("parallel", "arbitrary")),
        name="gla_scan_rev" if reverse else "gla_scan_fwd",
        interpret=interpret,
    )(q, k, v, logg, st0)


MOE_F_TILE = 512


def _expert_ffn_body(x_ref, g_ref, w1_ref, w3_ref, w2_ref, o_ref, acc_ref):
    f = pl.program_id(3)

    @pl.when(f == 0)
    def _():
        acc_ref[...] = jnp.zeros_like(acc_ref)

    x = x_ref[0, 0].astype(BF16)
    h1 = jnp.dot(x, w1_ref[0], preferred_element_type=F32)
    h3 = jnp.dot(x, w3_ref[0], preferred_element_type=F32)
    hid = (h1 * jax.nn.sigmoid(h1) * h3).astype(BF16)
    acc_ref[...] += jnp.dot(hid, w2_ref[0], preferred_element_type=F32)

    @pl.when(f == pl.num_programs(3) - 1)
    def _():
        o_ref[0, 0] = acc_ref[...] * g_ref[0, 0]


def expert_ffn(xin, gate, w1, w3, w2, interpret=False):
    B, E, cap, D = xin.shape
    F = w1.shape[-1]
    tm = min(cap, 1024)
    tf = MOE_F_TILE
    assert cap % tm == 0 and F % tf == 0
    return pl.pallas_call(
        _expert_ffn_body,
        grid=(E, B, cap // tm, F // tf),
        in_specs=[
            pl.BlockSpec((1, 1, tm, D), lambda e, b, i, f: (b, e, i, 0)),
            pl.BlockSpec((1, 1, tm, 1), lambda e, b, i, f: (b, e, i, 0)),
            pl.BlockSpec((1, D, tf), lambda e, b, i, f: (e, 0, f)),
            pl.BlockSpec((1, D, tf), lambda e, b, i, f: (e, 0, f)),
            pl.BlockSpec((1, tf, D), lambda e, b, i, f: (e, f, 0)),
        ],
        out_specs=pl.BlockSpec((1, 1, tm, D), lambda e, b, i, f: (b, e, i, 0)),
        out_shape=jax.ShapeDtypeStruct((B, E, cap, D), F32),
        scratch_shapes=[pltpu.VMEM((tm, D), F32)],
        compiler_params=pltpu.CompilerParams(
            dimension_semantics=("parallel", "parallel", "parallel", "arbitrary"),
            vmem_limit_bytes=48 * 1024 * 1024),
        name="expert_ffn",
        interpret=interpret,
    )(xin, gate[..., None], w1.astype(BF16), w3.astype(BF16), w2.astype(BF16))


def _split(p, sizes):
    return jnp.split(p, np.cumsum(sizes)[:-1].tolist(), axis=-1)


def rmsnorm(x, g, eps=NORM_EPS):
    xf = x.astype(F32)
    y = xf * lax.rsqrt(jnp.mean(xf * xf, axis=-1, keepdims=True) + eps)
    return (y * g.astype(F32)).astype(x.dtype)


def modulate(x, g, shift, scale):
    return rmsnorm(x, g) * (1 + scale) + shift


def conv3(u, w):
    up = jnp.pad(u, ((0, 0), (1, 1), (0, 0)))
    return up[:, :-2] * w[0] + up[:, 1:-1] * w[1] + up[:, 2:] * w[2]


def rope_2d(x):
    L, d = x.shape[1], x.shape[-1]
    half, nf = d // 2, d // 4
    t = jnp.arange(L)
    inv = ROPE_BASE ** (-jnp.arange(nf, dtype=F32) / nf)

    def rot(u, pos):
        ang = pos.astype(F32)[:, None] * inv[None, :]
        cos, sin = jnp.cos(ang)[None, :, None, :], jnp.sin(ang)[None, :, None, :]
        u1, u2 = u[..., :nf].astype(F32), u[..., nf:].astype(F32)
        return jnp.concatenate([u1 * cos - u2 * sin, u1 * sin + u2 * cos], axis=-1)

    return jnp.concatenate([rot(x[..., :half], t // GRID_W), rot(x[..., half:], t % GRID_W)], axis=-1).astype(x.dtype)


def ec_moe(h, router, w1, w3, w2):
    B, T, D = h.shape
    cap = EC_CAPACITY_FACTOR * T // N_EXPERTS
    aff = jax.nn.softmax((h @ router).astype(F32), axis=-1)
    gate, idx = lax.top_k(jnp.swapaxes(aff, 1, 2), cap)
    xin = jax.vmap(lambda hb, ib: hb[ib])(h, idx)
    if B * cap <= 1024:
        merge = lambda t: jnp.swapaxes(t, 0, 1).reshape((1, N_EXPERTS, B * cap) + t.shape[3:])
        y = expert_ffn(merge(xin), merge(gate), w1, w3, w2)
        y = jnp.swapaxes(y.reshape(N_EXPERTS, B, cap, D), 0, 1)
    else:
        y = expert_ffn(xin, gate, w1, w3, w2)
    out = jax.vmap(lambda ib, yb: jnp.zeros((T, D), yb.dtype).at[ib.reshape(-1)].add(yb.reshape(-1, D)))(idx, y)
    return out.astype(h.dtype)


def rwkv_readout(y, r, k, v, xg, r_k, g_up, ln_g, ln_b):
    B, L = r.shape[:2]
    hs = lambda t: t.reshape(B, L, RWKV_HEADS, RWKV_HEAD)
    yf = hs(y.astype(F32))
    mu = jnp.mean(yf, axis=-1, keepdims=True)
    var = jnp.mean(jnp.square(yf - mu), axis=-1, keepdims=True)
    yn = ((yf - mu) * lax.rsqrt(var + RWKV_LN_EPS)).reshape(B, L, D_RWKV) * ln_g + ln_b
    bonus = (jnp.sum(hs(r) * hs(k) * r_k, axis=-1, keepdims=True) * hs(v)).reshape(B, L, D_RWKV)
    g = jax.nn.sigmoid(xg) @ g_up
    return ((yn + bonus) * g).astype(r.dtype)


EVEN_TAIL = EVEN_SPLIT[6:]


def even_mixer(proj_lat, proj_ctx, conv_w, k_k, k_a, r_k, w0, w_up, a0, a_up, g_up, ln_g, ln_b, need_ctx):
    B = proj_lat[0].shape[0]
    p_lat = list(proj_lat[:6]) + _split(proj_lat[6], EVEN_TAIL)
    p_ctx = list(proj_ctx[:6]) + _split(proj_ctx[6], EVEN_TAIL)

    def conv_branch(u, gate_b, gate_c):
        return gate_b * conv3(gate_c * u, conv_w)

    def unit_key(k):
        Bq, L = k.shape[:2]
        kkf = (k * k_k).astype(F32).reshape(Bq, L, RWKV_HEADS, RWKV_HEAD)
        return (kkf * lax.rsqrt(jnp.sum(kkf * kkf, axis=-1, keepdims=True) + 1e-12)).reshape(Bq, L, D_RWKV)

    def dir_gates(xw, xa, d):
        w_raw = (w0[d] + jnp.tanh(xw) @ w_up[d]).astype(F32)
        return -jnp.exp(-jax.nn.softplus(-w_raw) - 0.5), jax.nn.sigmoid(a0[d] + xa @ a_up[d])

    kk_ctx, kk_lat = unit_key(p_ctx[4]), unit_key(p_lat[4])
    st0 = jnp.zeros((B, RWKV_HEADS // 2, PAIR, PAIR), F32)
    y_lat, y_ctx = 0.0, 0.0
    for d, rev in ((0, False), (1, True)):
        yc, st_c = rwkv_chunked(*p_ctx[3:6], kk_ctx, *dir_gates(*p_ctx[6:8], d), k_a, st0, reverse=rev)
        yl, _ = rwkv_chunked(*p_lat[3:6], kk_lat, *dir_gates(*p_lat[6:8], d), k_a, st_c, reverse=rev)
        y_lat = y_lat + yl
        if need_ctx:
            y_ctx = y_ctx + yc
    cat_lat = jnp.concatenate([conv_branch(*p_lat[0:3]),
                               rwkv_readout(y_lat, *p_lat[3:6], p_lat[8], r_k, g_up, ln_g, ln_b)], axis=-1)
    cat_ctx = None
    if need_ctx:
        cat_ctx = jnp.concatenate([conv_branch(*p_ctx[0:3]),
                                   rwkv_readout(y_ctx, *p_ctx[3:6], p_ctx[8], r_k, g_up, ln_g, ln_b)], axis=-1)
    return cat_lat, cat_ctx


def ctx_attention(q, k, v):
    s = jnp.einsum('bhqd,bhkd->bhqk', q, k).astype(F32) * (q.shape[-1] ** -0.5)
    p = jax.nn.softmax(s, axis=-1).astype(v.dtype)
    return jnp.einsum('bhqk,bhkd->bhqd', p, v)


def gla_log_gate(ga, a_up_d, a_b_d):
    B, L = ga.shape[:2]
    lg = jax.nn.log_sigmoid((ga @ a_up_d + a_b_d).astype(F32)) / GLA_GATE_TEMP
    return lg.reshape(B, L, GLA_HEADS, GLA_DK)


def gla_readout(o, gr, ln_g):
    B, L = gr.shape[:2]
    return (rmsnorm(o, ln_g).reshape(B, L, D_GLA_V) * jax.nn.silu(gr)).astype(gr.dtype)


def odd_mixer(proj_lat, proj_ctx, qn_g, kn_g, rpb, a_up, a_b, gla_ln_g, need_ctx):
    nq, nk, nv, gq, gk, gv, gr, ga = proj_lat
    cnq, cnk, cnv, cgq, cgk, cgv, cgr, cga = proj_ctx
    B, L = nq.shape[:2]
    Lc = cnq.shape[1]

    def nat_heads(t, g=None):
        t = t.reshape(t.shape[0], t.shape[1], NAT_HEADS, NAT_HEAD)
        if g is not None:
            t = rmsnorm(t, g)
        return jnp.swapaxes(t, 1, 2)

    nat_lat = nat_attention(nq, nk, nv, cnk, cnv, rpb, qn_g, kn_g)

    gh = lambda t, d: t.reshape(t.shape[0], t.shape[1], GLA_HEADS, d)
    qscale = GLA_DK ** -0.5
    q = (rope_2d(gh(gq, GLA_DK)) * qscale).reshape(B, L, D_GLA_K)
    k = rope_2d(gh(gk, GLA_DK)).reshape(B, L, D_GLA_K)
    qc = cgq * qscale
    st0 = jnp.zeros((B, GLA_HEADS // 2, GLA_VPAIR, GLA_KPAIR), F32)
    o_lat, o_ctx = 0.0, 0.0
    for d in range(2):
        lg_c = gla_log_gate(cga, a_up[d], a_b[d]).reshape(B, Lc, D_GLA_K)
        lg_l = gla_log_gate(ga, a_up[d], a_b[d]).reshape(B, L, D_GLA_K)
        oc, st_c = gla_scan(qc, cgk, cgv, lg_c, st0, reverse=(d == 1))
        ol, _ = gla_scan(q, k, gv, lg_l, st_c, reverse=(d == 1))
        o_lat = o_lat + ol
        if need_ctx:
            o_ctx = o_ctx + oc
    o_lat = gh(o_lat, GLA_DV)
    if need_ctx:
        o_ctx = gh(o_ctx, GLA_DV)
    cat_lat = jnp.concatenate([nat_lat, gla_readout(o_lat, gr, gla_ln_g)], axis=-1)
    cat_ctx = None
    if need_ctx:
        kc, vc = nat_heads(cnk, kn_g), nat_heads(cnv)
        nat_ctx = jnp.swapaxes(ctx_attention(nat_heads(cnq, qn_g), kc, vc), 1, 2).reshape(B, Lc, D_NAT)
        cat_ctx = jnp.concatenate([nat_ctx, gla_readout(o_ctx, cgr, gla_ln_g)], axis=-1)
    return cat_lat, cat_ctx


def kernel(x, c, ctx, c_ctx, ada_w, ada_b, norm1_g, norm2_g, ev_w_in, ev_w_out, conv_w, rw_k_k, rw_k_a, rw_r_k, rw_w0, rw_w_up, rw_a0, rw_a_up, rw_g_up, rw_ln_g, rw_ln_b, od_w_in, od_w_out, nat_qn_g, nat_kn_g, nat_rpb, gla_a_up, gla_a_b, gla_ln_g, moe_router, moe_w1, moe_w3, moe_w2):
    depth = ada_w.shape[0]
    ctx_s = ctx
    silu_c = jax.nn.silu(c)
    silu_cc = jax.nn.silu(c_ctx)
    for l in range(depth):
        last = l == depth - 1
        j = l // 2
        sh1, sc1, gt1, sh2, sc2, gt2 = _split((silu_c @ ada_w[l] + ada_b[l])[:, None, :], [D_MODEL] * 6)
        csh1, csc1, cgt1, csh2, csc2, cgt2 = _split(silu_cc @ ada_w[l] + ada_b[l], [D_MODEL] * 6)
        even = l % 2 == 0
        w_in = ev_w_in[j] if even else od_w_in[j]
        sizes = EVEN_SPLIT[:6] + [sum(EVEN_TAIL)] if even else ODD_SPLIT
        proj_lat = in_proj_split(x, norm1_g[l], sh1, sc1, w_in, sizes)
        proj_ctx = in_proj_split(ctx_s, norm1_g[l], csh1, csc1, w_in, sizes)
        if even:
            cat_lat, cat_ctx = even_mixer(proj_lat, proj_ctx, conv_w[j], rw_k_k[j], rw_k_a[j], rw_r_k[j],
                                          rw_w0[j], rw_w_up[j], rw_a0[j], rw_a_up[j], rw_g_up[j], rw_ln_g[j],
                                          rw_ln_b[j], not last)
            w_out = ev_w_out[j]
        else:
            cat_lat, cat_ctx = odd_mixer(proj_lat, proj_ctx, nat_qn_g[j], nat_kn_g[j], nat_rpb[j],
                                         gla_a_up[j], gla_a_b[j], gla_ln_g[j], not last)
            w_out = od_w_out[j]
        experts = (moe_w1[l].astype(BF16), moe_w3[l].astype(BF16), moe_w2[l].astype(BF16))
        x = out_proj_residual(cat_lat, w_out, x, gt1)
        x = x + gt2 * ec_moe(modulate(x, norm2_g[l], sh2, sc2), moe_router[l], *experts)
        if not last:
            ctx_s = out_proj_residual(cat_ctx, w_out, ctx_s, cgt1)
            ctx_s = ctx_s + cgt2 * ec_moe(modulate(ctx_s, norm2_g[l], csh2, csc2), moe_router[l], *experts)
    return x
```

```python
import functools

import jax
import jax.numpy as jnp
import numpy as np
from jax import lax
from jax.experimental import pallas as pl
from jax.experimental.pallas import tpu as pltpu

D_MODEL = 1024
GRID_W = 64
NORM_EPS = 1e-6
F32 = jnp.float32
BF16 = jnp.bfloat16

D_CONV = 512
D_RWKV = 512
RWKV_HEAD = 64
RWKV_HEADS = D_RWKV // RWKV_HEAD
RWKV_DECAY_RANK = 64
RWKV_ICLR_RANK = 64
RWKV_GATE_RANK = 128
RWKV_LN_EPS = 64e-5
D_NAT = 512
NAT_HEAD = 64
NAT_HEADS = D_NAT // NAT_HEAD
NAT_ROWS = 8
NAT_COLS = 16
GLA_HEADS = 4
GLA_DK = 64
GLA_DV = 128
D_GLA_K = GLA_HEADS * GLA_DK
D_GLA_V = GLA_HEADS * GLA_DV
GLA_GATE_RANK = 16
GLA_GATE_TEMP = 16.0
GLA_CHUNK = 64
ROPE_BASE = 10000.0
N_EXPERTS = 16
EC_CAPACITY_FACTOR = 2

EVEN_SPLIT = [D_CONV, D_CONV, D_CONV, D_RWKV, D_RWKV, D_RWKV, RWKV_DECAY_RANK, RWKV_ICLR_RANK, RWKV_GATE_RANK]
ODD_SPLIT = [D_NAT, D_NAT, D_NAT, D_GLA_K, D_GLA_K, D_GLA_V, D_GLA_V, GLA_GATE_RANK]


def _out_proj_body(a_ref, w_ref, x_ref, g_ref, o_ref):
    acc = jnp.dot(a_ref[0].astype(BF16), w_ref[...], preferred_element_type=F32)
    o_ref[0] = x_ref[0] + g_ref[0] * acc


def out_proj_residual(a, w, x, gate, block_rows=512):
    B, L, K = a.shape
    N = w.shape[1]
    tm = min(block_rows, L)
    assert L % tm == 0
    gate = jnp.broadcast_to(gate.reshape(-1, 1, N), (B, 1, N))
    return pl.pallas_call(
        _out_proj_body,
        grid=(B, L // tm),
        in_specs=[
            pl.BlockSpec((1, tm, K), lambda b, i: (b, i, 0)),
            pl.BlockSpec((K, N), lambda b, i: (0, 0)),
            pl.BlockSpec((1, tm, N), lambda b, i: (b, i, 0)),
            pl.BlockSpec((1, 1, N), lambda b, i: (b, 0, 0)),
        ],
        out_specs=pl.BlockSpec((1, tm, N), lambda b, i: (b, i, 0)),
        out_shape=jax.ShapeDtypeStruct((B, L, N), F32),
        compiler_params=pltpu.CompilerParams(dimension_semantics=("parallel", "parallel")),
        name="out_proj_residual",
    )(a, w.astype(BF16), x, gate)


def _in_proj_body(x_ref, g_ref, shift_ref, scale_ref, w_ref, *o_refs, offsets):
    xf = x_ref[0]
    y = xf * lax.rsqrt(jnp.mean(xf * xf, axis=-1, keepdims=True) + NORM_EPS) * g_ref[...]
    h = (y * (1.0 + scale_ref[0]) + shift_ref[0]).astype(BF16)
    for o_ref, off in zip(o_refs, offsets):
        o_ref[0] = jnp.dot(h, w_ref[:, off:off + o_ref.shape[-1]], preferred_element_type=F32)


def in_proj_split(x, g, shift, scale, w, sizes, block_rows=512):
    B, L, D = x.shape
    N = w.shape[1]
    assert sum(sizes) == N
    tm = min(block_rows, L)
    assert L % tm == 0
    offsets = tuple(int(o) for o in np.cumsum([0] + list(sizes[:-1])))
    per_sample = lambda t: jnp.broadcast_to(t.reshape(-1, 1, D), (B, 1, D))
    vec = pl.BlockSpec((1, 1, D), lambda b, i: (b, 0, 0))
    return pl.pallas_call(
        functools.partial(_in_proj_body, offsets=offsets),
        grid=(B, L // tm),
        in_specs=[pl.BlockSpec((1, tm, D), lambda b, i: (b, i, 0)), pl.BlockSpec((1, D), lambda b, i: (0, 0)),
                  vec, vec, pl.BlockSpec((D, N), lambda b, i: (0, 0))],
        out_specs=[pl.BlockSpec((1, tm, s), lambda b, i: (b, i, 0)) for s in sizes],
        out_shape=[jax.ShapeDtypeStruct((B, L, s), F32) for s in sizes],
        compiler_params=pltpu.CompilerParams(dimension_semantics=("parallel", "parallel"),
                                             vmem_limit_bytes=48 * 1024 * 1024),
        name="in_proj_split",
    )(x, g.reshape(1, D), per_sample(shift), per_sample(scale), w.astype(BF16))


RWKV_CHUNK = 64
RWKV_CHUNKS_PER_STEP = 2
PAIR = 2 * RWKV_HEAD


def _mm(a, b):
    return jnp.dot(a.astype(BF16), b.astype(BF16), preferred_element_type=F32)


def _mm_nt(a, b):
    return lax.dot_general(a.astype(BF16), b.astype(BF16), (((1,), (1,)), ((), ())), preferred_element_type=F32)


def _mm_tn(a, b):
    return lax.dot_general(a.astype(BF16), b.astype(BF16), (((0,), (0,)), ((), ())), preferred_element_type=F32)


def _rwkv_chunk_pairs(ins, sts, slots, *, reverse):
    C = RWKV_CHUNK
    row = lax.broadcasted_iota(jnp.int32, (PAIR, PAIR), 0)
    col = lax.broadcasted_iota(jnp.int32, (PAIR, PAIR), 1)
    same_head = (row // RWKV_HEAD) == (col // RWKV_HEAD)
    t_i, s_i = row % C, col % C
    before = (s_i > t_i) if reverse else (s_i < t_i)
    upto = before | (s_i == t_i)

    ct = lax.broadcasted_iota(jnp.int32, (C, C), 0)
    cs = lax.broadcasted_iota(jnp.int32, (C, C), 1)
    tri = ((cs >= ct) if reverse else (cs <= ct)).astype(F32)
    eye = jnp.where(row == col, 1.0, 0.0)

    def expand(x):
        return jnp.where(same_head, jnp.concatenate([x, x], axis=0), 0.0)

    def prepare(r, k, v, kk, lw, a, ka):
        keff = k * (1.0 + (a - 1.0) * ka)
        b = kk * a
        cum_in = jnp.dot(tri, lw, preferred_element_type=F32, precision=lax.Precision.HIGHEST)
        cum_ex = cum_in - lw
        tot = jnp.sum(lw, axis=0, keepdims=True)
        e_neg = jnp.exp(-cum_in)
        e_rem = jnp.exp(tot - cum_in)
        return dict(at2=expand(-kk * jnp.exp(cum_ex)), rt2=expand(r * jnp.exp(cum_in)),
                    bh2=expand(b * e_neg), kh2=expand(keff * e_neg), bp2=expand(b * e_rem),
                    kp2=expand(keff * e_rem), v2=expand(v), dtot=eye * jnp.exp(tot))

    ps = [prepare(*args) for args in ins]
    pps = [_mm_nt(jnp.concatenate([p["at2"], p["rt2"]], axis=0), jnp.concatenate([p["bh2"], p["kh2"]], axis=0))
           for p in ps]
    a_ab = [jnp.where(before, pp[:PAIR, :PAIR], 0.0) for pp in pps]
    a_ak = [jnp.where(before, pp[:PAIR, PAIR:], 0.0) for pp in pps]
    a_rb = [jnp.where(upto, pp[PAIR:, :PAIR], 0.0) for pp in pps]
    a_rk = [jnp.where(upto, pp[PAIR:, PAIR:], 0.0) for pp in pps]

    tinv = [eye + n for n in a_ab]
    npow = a_ab
    for _ in range(int(np.log2(C)) - 1):
        npow = [_mm(n, n) for n in npow]
        tinv = [t + _mm(t, n) for t, n in zip(tinv, npow)]

    av = [_mm(jnp.concatenate([ak, rk], axis=0), p["v2"]) for ak, rk, p in zip(a_ak, a_rk, ps)]
    x = [_mm(t, jnp.concatenate([p["at2"], w[:PAIR]], axis=1)) for t, p, w in zip(tinv, ps, av)]
    z = [_mm(rb, xx) for rb, xx in zip(a_rb, x)]
    bx = [_mm_tn(p["bp2"], xx) for p, xx in zip(ps, x)]
    kv = [_mm_tn(p["kp2"], p["v2"]) for p in ps]
    qe = [p["rt2"] + zz[:, :PAIR] for p, zz in zip(ps, z)]
    yloc = [zz[:, PAIR:] + w[PAIR:] for zz, w in zip(z, av)]
    mt = [p["dtot"] + b[:, :PAIR] for p, b in zip(ps, bx)]
    gt = [b[:, PAIR:] + g for b, g in zip(bx, kv)]
    sts = list(sts)
    ys = []
    for i, s in enumerate(slots):
        y2 = _mm(qe[i], sts[s]) + yloc[i]
        sts[s] = _mm(mt[i], sts[s]) + gt[i]
        ys.append(y2[:C] + y2[C:])
    return ys, sts


def _rwkv_chunk_body(r_ref, k_ref, v_ref, kk_ref, lw_ref, a_ref, ka_ref, s0_ref, y_ref, sT_ref, st_scr, *, reverse):
    c = pl.program_id(1)

    @pl.when(c == 0)
    def _():
        st_scr[...] = s0_ref[0]

    C = RWKV_CHUNK
    npair = st_scr.shape[0]
    nchunk = r_ref.shape[1] // C
    order = range(nchunk - 1, -1, -1) if reverse else range(nchunk)
    chains = [(slice(u * C, (u + 1) * C), slice(p * PAIR, (p + 1) * PAIR), p) for u in order for p in range(npair)]
    ins = [(r_ref[0, rs, ls], k_ref[0, rs, ls], v_ref[0, rs, ls], kk_ref[0, rs, ls], lw_ref[0, rs, ls],
            a_ref[0, rs, ls], ka_ref[:, ls]) for rs, ls, _ in chains]
    ys, sts = _rwkv_chunk_pairs(ins, [st_scr[p] for p in range(npair)], [p for _, _, p in chains], reverse=reverse)
    for (rs, ls, _), y in zip(chains, ys):
        y_ref[0, rs, ls] = y
    for p in range(npair):
        st_scr[p] = sts[p]

    @pl.when(c == pl.num_programs(1) - 1)
    def _():
        sT_ref[0] = st_scr[...]


def rwkv_state_pack(S):
    B, H = S.shape[:2]
    St = jnp.swapaxes(S, -1, -2).reshape(B, H // 2, 2, RWKV_HEAD, RWKV_HEAD)
    z = jnp.zeros_like(St[:, :, 0])
    top = jnp.concatenate([St[:, :, 0], z], axis=-1)
    bot = jnp.concatenate([z, St[:, :, 1]], axis=-1)
    return jnp.concatenate([top, bot], axis=-2)


def rwkv_state_unpack(St):
    B, P = St.shape[:2]
    h0 = St[:, :, :RWKV_HEAD, :RWKV_HEAD]
    h1 = St[:, :, RWKV_HEAD:, RWKV_HEAD:]
    return jnp.swapaxes(jnp.stack([h0, h1], axis=2).reshape(B, 2 * P, RWKV_HEAD, RWKV_HEAD), -1, -2)


def rwkv_chunked(r, k, v, kk, lw, a, k_a, st0, *, reverse, interpret=False):
    B, L, D = r.shape
    T = RWKV_CHUNK * RWKV_CHUNKS_PER_STEP
    assert L % T == 0 and D % PAIR == 0
    n, P = L // T, D // PAIR
    cidx = (lambda c: n - 1 - c) if reverse else (lambda c: c)
    seq = pl.BlockSpec((1, T, D), lambda b, c: (b, cidx(c), 0))
    state = pl.BlockSpec((1, P, PAIR, PAIR), lambda b, c: (b, 0, 0, 0))
    return pl.pallas_call(
        functools.partial(_rwkv_chunk_body, reverse=reverse),
        grid=(B, n),
        in_specs=[seq] * 6 + [pl.BlockSpec((1, D), lambda b, c: (0, 0)), state],
        out_specs=[seq, state],
        out_shape=[jax.ShapeDtypeStruct((B, L, D), F32), jax.ShapeDtypeStruct((B, P, PAIR, PAIR), F32)],
        scratch_shapes=[pltpu.VMEM((P, PAIR, PAIR), F32)],
        compiler_params=pltpu.CompilerParams(dimension_semantics=("parallel", "arbitrary")),
        name="rwkv_chunked_rev" if reverse else "rwkv_chunked_fwd",
        interpret=interpret,
    )(r, k, v, kk, lw, a, k_a.reshape(1, D), st0)


NAT_WIN = NAT_ROWS * GRID_W
NAT_ROWS_PER_BLOCK = 16
NAT_MASKED = -1e30


def _pair_rmsnorm(x, g, lane_lo):
    sq = x * x
    s_lo = jnp.sum(jnp.where(lane_lo, sq, 0.0), axis=-1, keepdims=True)
    s_hi = jnp.sum(sq, axis=-1, keepdims=True) - s_lo
    ms = jnp.where(lane_lo, s_lo, s_hi) * (1.0 / NAT_HEAD)
    return x * lax.rsqrt(ms + NORM_EPS) * g


def _nat_body(q_ref, k_ref, v_ref, kc_ref, vc_ref, bias_ref, qg_ref, kg_ref, o_ref, kn_scr, vb_scr, kcn_scr, vcb_scr,
              *, rows):
    rb = pl.program_id(2)
    L = k_ref.shape[1]
    norm_rows = 512
    lane_lo_n = lax.broadcasted_iota(jnp.int32, (norm_rows, PAIR), 1) < NAT_HEAD

    @pl.when(rb == 0)
    def _():
        def norm_block(i, carry):
            sl = pl.ds(pl.multiple_of(i * norm_rows, norm_rows), norm_rows)
            kn_scr[sl, :] = _pair_rmsnorm(k_ref[0, sl, :], kg_ref[...], lane_lo_n).astype(BF16)
            vb_scr[sl, :] = v_ref[0, sl, :].astype(BF16)
            return carry
        lax.fori_loop(0, L // norm_rows, norm_block, 0)
        lane_lo_c = lax.broadcasted_iota(jnp.int32, kc_ref.shape[1:], 1) < NAT_HEAD
        kcn_scr[...] = _pair_rmsnorm(kc_ref[0], kg_ref[...], lane_lo_c).astype(BF16)
        vcb_scr[...] = vc_ref[0].astype(BF16)

    lane_lo = lax.broadcasted_iota(jnp.int32, (GRID_W, PAIR), 1) < NAT_HEAD
    scale = NAT_HEAD ** -0.5

    rows_per_iter = 2
    nt = (((1,), (1,)), ((), ()))

    def row_group(it, carry):
        qsl, kwin, vwin, qh, bias = [], [], [], [], []
        for u in range(rows_per_iter):
            j = it * rows_per_iter + u
            r = rb * NAT_ROWS_PER_BLOCK + j
            rs = jnp.clip(r - NAT_ROWS // 2, 0, rows - NAT_ROWS)
            sl = pl.ds(pl.multiple_of(j * GRID_W, GRID_W), GRID_W)
            wsl = pl.ds(pl.multiple_of(rs * GRID_W, GRID_W), NAT_WIN)
            qn = _pair_rmsnorm(q_ref[0, sl, :], qg_ref[...], lane_lo) * scale
            for h in range(2):
                qsl.append(sl)
                kwin.append(kn_scr[wsl, :])
                vwin.append(vb_scr[wsl, :])
                qh.append(jnp.where(lane_lo if h == 0 else ~lane_lo, qn, 0.0).astype(BF16))
                bias.append(bias_ref[h, r - rs])
        chains = range(2 * rows_per_iter)
        s_loc = [lax.dot_general(qh[i], kwin[i], nt, preferred_element_type=F32) + bias[i] for i in chains]
        s_ctx = [lax.dot_general(qh[i], kcn_scr[...], nt, preferred_element_type=F32) for i in chains]
        m = [jnp.maximum(jnp.max(s_loc[i], axis=-1, keepdims=True), jnp.max(s_ctx[i], axis=-1, keepdims=True))
             for i in chains]
        p_loc = [jnp.exp(s_loc[i] - m[i]) for i in chains]
        p_ctx = [jnp.exp(s_ctx[i] - m[i]) for i in chains]
        den = [jnp.sum(p_loc[i], axis=-1, keepdims=True) + jnp.sum(p_ctx[i], axis=-1, keepdims=True) for i in chains]
        o = [(jnp.dot(p_loc[i].astype(BF16), vwin[i], preferred_element_type=F32)
              + jnp.dot(p_ctx[i].astype(BF16), vcb_scr[...], preferred_element_type=F32)) / den[i] for i in chains]
        for u in range(rows_per_iter):
            o_ref[0, qsl[2 * u], :] = jnp.where(lane_lo, o[2 * u], o[2 * u + 1])
        return carry

    lax.fori_loop(0, NAT_ROWS_PER_BLOCK // rows_per_iter, row_group, 0)


def _nat_bias_table(rpb):
    col = jnp.arange(GRID_W)
    cstart = jnp.clip(col - NAT_COLS // 2, 0, GRID_W - NAT_COLS)
    delta = jnp.arange(NAT_ROWS)
    wrow = jnp.arange(NAT_ROWS)
    ridx = wrow[None, :] - delta[:, None] + (NAT_ROWS - 1)
    cidx = col[None, :] - col[:, None] + (NAT_COLS - 1)
    inwin = (col[None, :] >= cstart[:, None]) & (col[None, :] < cstart[:, None] + NAT_COLS)
    rsel = (ridx[:, :, None] == jnp.arange(2 * NAT_ROWS - 1)).astype(F32)
    csel = ((cidx[:, :, None] == jnp.arange(2 * NAT_COLS - 1)) & inwin[:, :, None]).astype(F32)
    tab = jnp.einsum('hab,dia,ckb->hdcik', rpb.astype(F32), rsel, csel, precision=lax.Precision.HIGHEST)
    tab = jnp.where(inwin[None, None, :, None, :], tab, NAT_MASKED)
    return tab.reshape(rpb.shape[0], NAT_ROWS, GRID_W, NAT_WIN)


def nat_attention(q, k, v, kc, vc, rpb, qn_g, kn_g, interpret=False):
    B, L, D = q.shape
    Lc = kc.shape[1]
    rows = L // GRID_W
    P = D // PAIR
    rpb_blk = NAT_ROWS_PER_BLOCK
    assert rows >= NAT_ROWS and rows % rpb_blk == 0 and L % 512 == 0
    bias = _nat_bias_table(rpb)
    g2 = lambda g: jnp.tile(g, 2).reshape(1, PAIR)
    full = pl.BlockSpec((1, L, PAIR), lambda b, p, i: (b, 0, p))
    cfull = pl.BlockSpec((1, Lc, PAIR), lambda b, p, i: (b, 0, p))
    qblk = pl.BlockSpec((1, rpb_blk * GRID_W, PAIR), lambda b, p, i: (b, i, p))
    gspec = pl.BlockSpec((1, PAIR), lambda b, p, i: (0, 0))
    return pl.pallas_call(
        functools.partial(_nat_body, rows=rows),
        grid=(B, P, rows // rpb_blk),
        in_specs=[qblk, full, full, cfull, cfull,
                  pl.BlockSpec((2, NAT_ROWS, GRID_W, NAT_WIN), lambda b, p, i: (p, 0, 0, 0)), gspec, gspec],
        out_specs=qblk,
        out_shape=jax.ShapeDtypeStruct((B, L, D), F32),
        scratch_shapes=[pltpu.VMEM((L, PAIR), BF16), pltpu.VMEM((L, PAIR), BF16),
                        pltpu.VMEM((Lc, PAIR), BF16), pltpu.VMEM((Lc, PAIR), BF16)],
        compiler_params=pltpu.CompilerParams(dimension_semantics=("parallel", "parallel", "arbitrary"),
                                             vmem_limit_bytes=48 * 1024 * 1024),
        name="nat_attention",
        interpret=interpret,
    )(q, k, v, kc, vc, bias, g2(qn_g), g2(kn_g))


GLA_SUB = 16
GLA_BLOCK = 256
GLA_KPAIR = 2 * GLA_DK
GLA_VPAIR = 2 * GLA_DV
GLA_UNROLL = 4


def _gla_body(q_ref, k_ref, v_ref, g_ref, s0_ref, o_ref, sT_ref, st_scr, *, reverse):
    C = GLA_SUB
    nsub = GLA_BLOCK // C
    npair = st_scr.shape[0]
    blk = pl.program_id(1)

    @pl.when(blk == 0)
    def _():
        st_scr[...] = s0_ref[0]

    ti = lax.broadcasted_iota(jnp.int32, (C, C), 0)
    si = lax.broadcasted_iota(jnp.int32, (C, C), 1)
    tri = ((si >= ti) if reverse else (si <= ti)).astype(F32)
    lane_lo = lax.broadcasted_iota(jnp.int32, (C, GLA_KPAIR), 1) < GLA_DK
    row_id = lax.broadcasted_iota(jnp.int32, (C, GLA_KPAIR), 0)
    vrow = lax.broadcasted_iota(jnp.int32, (2 * C, GLA_VPAIR), 0)
    vcol = lax.broadcasted_iota(jnp.int32, (2 * C, GLA_VPAIR), 1)
    v_same_head = (vrow // C) == (vcol // GLA_DV)
    srow = lax.broadcasted_iota(jnp.int32, (GLA_VPAIR, GLA_KPAIR), 0)
    scol = lax.broadcasted_iota(jnp.int32, (GLA_VPAIR, GLA_KPAIR), 1)
    s_same_head = (srow // GLA_DV) == (scol // GLA_DK)

    pairs = range(npair)
    klanes = [slice(p * GLA_KPAIR, (p + 1) * GLA_KPAIR) for p in pairs]
    vlanes = [slice(p * GLA_VPAIR, (p + 1) * GLA_VPAIR) for p in pairs]

    def sub_chunks(it, carry):
        chains = [(u, p) for u in range(GLA_UNROLL) for p in pairs]
        sl, q, k, g, v = {}, {}, {}, {}, {}
        for u in range(GLA_UNROLL):
            i = it * GLA_UNROLL + u
            ci = (nsub - 1 - i) if reverse else i
            sl[u] = pl.ds(pl.multiple_of(ci * C, C), C)
            for p in pairs:
                q[u, p], k[u, p], g[u, p] = (r[0, sl[u], klanes[p]] for r in (q_ref, k_ref, g_ref))
                v[u, p] = v_ref[0, sl[u], vlanes[p]]
        b = {c: jnp.dot(tri, g[c], preferred_element_type=F32, precision=lax.Precision.HIGHEST) for c in chains}
        b_end = {c: jnp.sum(g[c], axis=0, keepdims=True) for c in chains}
        a_lo = {c: jnp.zeros((C, C), F32) for c in chains}
        a_hi = {c: jnp.zeros((C, C), F32) for c in chains}
        for j in range(C):
            seen = (row_id <= j) if reverse else (row_id >= j)
            for c in chains:
                decay = jnp.exp(jnp.where(seen, b[c] - b[c][j:j + 1, :], 0.0))
                f = jnp.where(seen, q[c] * k[c][j:j + 1, :] * decay, 0.0)
                r_lo = jnp.sum(jnp.where(lane_lo, f, 0.0), axis=-1, keepdims=True)
                r_hi = jnp.sum(jnp.where(lane_lo, 0.0, f), axis=-1, keepdims=True)
                a_lo[c] = jnp.where(si == j, r_lo, a_lo[c])
                a_hi[c] = jnp.where(si == j, r_hi, a_hi[c])
        v_bd = {c: jnp.where(v_same_head, jnp.concatenate([v[c], v[c]], axis=0), 0.0) for c in chains}
        o_in = {c: _mm(jnp.concatenate([a_lo[c], a_hi[c]], axis=1), v_bd[c]) for c in chains}
        kv = {c: jnp.where(s_same_head, _mm_tn(v[c], k[c] * jnp.exp(b_end[c] - b[c])), 0.0) for c in chains}
        qd = {c: q[c] * jnp.exp(b[c]) for c in chains}
        e_end = {c: jnp.exp(b_end[c]) for c in chains}
        st = [st_scr[p] for p in pairs]
        for u in range(GLA_UNROLL):
            for p in pairs:
                o_ref[0, sl[u], vlanes[p]] = o_in[u, p] + _mm_nt(qd[u, p], st[p])
                st[p] = st[p] * e_end[u, p] + kv[u, p]
        for p in pairs:
            st_scr[p] = st[p]
        return carry

    lax.fori_loop(0, nsub // GLA_UNROLL, sub_chunks, 0)

    @pl.when(blk == pl.num_programs(1) - 1)
    def _():
        sT_ref[0] = st_scr[...]


def gla_scan(q, k, v, logg, st0, *, reverse, interpret=False):
    B, L, Dk = q.shape
    Dv = v.shape[-1]
    T = GLA_BLOCK
    assert L % T == 0 and Dk % GLA_KPAIR == 0
    n, P = L // T, Dk // GLA_KPAIR
    bidx = (lambda c: n - 1 - c) if reverse else (lambda c: c)
    kspec = pl.BlockSpec((1, T, Dk), lambda b, c: (b, bidx(c), 0))
    vspec = pl.BlockSpec((1, T, Dv), lambda b, c: (b, bidx(c), 0))
    sspec = pl.BlockSpec((1, P, GLA_VPAIR, GLA_KPAIR), lambda b, c: (b, 0, 0, 0))
    return pl.pallas_call(
        functools.partial(_gla_body, reverse=reverse),
        grid=(B, n),
        in_specs=[kspec, kspec, vspec, kspec, sspec],
        out_specs=[vspec, sspec],
        out_shape=[jax.ShapeDtypeStruct((B, L, Dv), F32),
                   jax.ShapeDtypeStruct((B, P, GLA_VPAIR, GLA_KPAIR), F32)],
        scratch_shapes=[pltpu.VMEM((P, GLA_VPAIR, GLA_KPAIR), F32)],
        compiler_params=pltpu.CompilerParams(dimension_semantics=("parallel", "arbitrary")),
        name="gla_scan_rev" if reverse else "gla_scan_fwd",
        interpret=interpret,
    )(q, k, v, logg, st0)


MOE_F_TILE = 512


def _expert_ffn_body(x_ref, g_ref, w1_ref, w3_ref, w2_ref, o_ref, acc_ref):
    f = pl.program_id(3)

    @pl.when(f == 0)
    def _():
        acc_ref[...] = jnp.zeros_like(acc_ref)

    x = x_ref[0, 0].astype(BF16)
    h1 = jnp.dot(x, w1_ref[0].astype(BF16), preferred_element_type=F32)
    h3 = jnp.dot(x, w3_ref[0].astype(BF16), preferred_element_type=F32)
    hid = (h1 * jax.nn.sigmoid(h1) * h3).astype(BF16)
    acc_ref[...] += jnp.dot(hid, w2_ref[0].astype(BF16), preferred_element_type=F32)

    @pl.when(f == pl.num_programs(3) - 1)
    def _():
        o_ref[0, 0] = acc_ref[...] * g_ref[0, 0]


def expert_ffn(xin, gate, w1, w3, w2, interpret=False):
    B, E, cap, D = xin.shape
    F = w1.shape[-1]
    tm = min(cap, 1024)
    tf = MOE_F_TILE
    assert cap % tm == 0 and F % tf == 0
    return pl.pallas_call(
        _expert_ffn_body,
        grid=(E, B, cap // tm, F // tf),
        in_specs=[
            pl.BlockSpec((1, 1, tm, D), lambda e, b, i, f: (b, e, i, 0)),
            pl.BlockSpec((1, 1, tm, 1), lambda e, b, i, f: (b, e, i, 0)),
            pl.BlockSpec((1, D, tf), lambda e, b, i, f: (e, 0, f)),
            pl.BlockSpec((1, D, tf), lambda e, b, i, f: (e, 0, f)),
            pl.BlockSpec((1, tf, D), lambda e, b, i, f: (e, f, 0)),
        ],
        out_specs=pl.BlockSpec((1, 1, tm, D), lambda e, b, i, f: (b, e, i, 0)),
        out_shape=jax.ShapeDtypeStruct((B, E, cap, D), F32),
        scratch_shapes=[pltpu.VMEM((tm, D), F32)],
        compiler_params=pltpu.CompilerParams(
            dimension_semantics=("parallel", "parallel", "parallel", "arbitrary"),
            vmem_limit_bytes=56 * 1024 * 1024),
        name="expert_ffn",
        interpret=interpret,
    )(xin, gate[..., None], w1, w3, w2)


def _split(p, sizes):
    return jnp.split(p, np.cumsum(sizes)[:-1].tolist(), axis=-1)


def rmsnorm(x, g, eps=NORM_EPS):
    xf = x.astype(F32)
    y = xf * lax.rsqrt(jnp.mean(xf * xf, axis=-1, keepdims=True) + eps)
    return (y * g.astype(F32)).astype(x.dtype)


def modulate(x, g, shift, scale):
    return rmsnorm(x, g) * (1 + scale) + shift


def conv3(u, w):
    up = jnp.pad(u, ((0, 0), (1, 1), (0, 0)))
    return up[:, :-2] * w[0] + up[:, 1:-1] * w[1] + up[:, 2:] * w[2]


def rope_2d(x):
    L, d = x.shape[1], x.shape[-1]
    half, nf = d // 2, d // 4
    t = jnp.arange(L)
    inv = ROPE_BASE ** (-jnp.arange(nf, dtype=F32) / nf)

    def rot(u, pos):
        ang = pos.astype(F32)[:, None] * inv[None, :]
        cos, sin = jnp.cos(ang)[None, :, None, :], jnp.sin(ang)[None, :, None, :]
        u1, u2 = u[..., :nf].astype(F32), u[..., nf:].astype(F32)
        return jnp.concatenate([u1 * cos - u2 * sin, u1 * sin + u2 * cos], axis=-1)

    return jnp.concatenate([rot(x[..., :half], t // GRID_W), rot(x[..., half:], t % GRID_W)], axis=-1).astype(x.dtype)


def ec_moe(h, router, w1, w3, w2):
    B, T, D = h.shape
    cap = EC_CAPACITY_FACTOR * T // N_EXPERTS
    aff = jax.nn.softmax((h @ router).astype(F32), axis=-1)
    gate, idx = lax.top_k(jnp.swapaxes(aff, 1, 2), cap)
    xin = jax.vmap(lambda hb, ib: hb[ib])(h, idx)
    if B * cap <= 1024:
        merge = lambda t: jnp.swapaxes(t, 0, 1).reshape((1, N_EXPERTS, B * cap) + t.shape[3:])
        y = expert_ffn(merge(xin), merge(gate), w1, w3, w2)
        y = jnp.swapaxes(y.reshape(N_EXPERTS, B, cap, D), 0, 1)
    else:
        y = expert_ffn(xin, gate, w1, w3, w2)
    out = jax.vmap(lambda ib, yb: jnp.zeros((T, D), yb.dtype).at[ib.reshape(-1)].add(yb.reshape(-1, D)))(idx, y)
    return out.astype(h.dtype)


def rwkv_readout(y, r, k, v, xg, r_k, g_up, ln_g, ln_b):
    B, L = r.shape[:2]
    hs = lambda t: t.reshape(B, L, RWKV_HEADS, RWKV_HEAD)
    yf = hs(y.astype(F32))
    mu = jnp.mean(yf, axis=-1, keepdims=True)
    var = jnp.mean(jnp.square(yf - mu), axis=-1, keepdims=True)
    yn = ((yf - mu) * lax.rsqrt(var + RWKV_LN_EPS)).reshape(B, L, D_RWKV) * ln_g + ln_b
    bonus = (jnp.sum(hs(r) * hs(k) * r_k, axis=-1, keepdims=True) * hs(v)).reshape(B, L, D_RWKV)
    g = jax.nn.sigmoid(xg) @ g_up
    return ((yn + bonus) * g).astype(r.dtype)


EVEN_TAIL = EVEN_SPLIT[6:]


def even_mixer(proj_lat, proj_ctx, conv_w, k_k, k_a, r_k, w0, w_up, a0, a_up, g_up, ln_g, ln_b, need_ctx):
    B = proj_lat[0].shape[0]
    p_lat = list(proj_lat[:6]) + _split(proj_lat[6], EVEN_TAIL)
    p_ctx = list(proj_ctx[:6]) + _split(proj_ctx[6], EVEN_TAIL)

    def conv_branch(u, gate_b, gate_c):
        return gate_b * conv3(gate_c * u, conv_w)

    def unit_key(k):
        Bq, L = k.shape[:2]
        kkf = (k * k_k).astype(F32).reshape(Bq, L, RWKV_HEADS, RWKV_HEAD)
        return (kkf * lax.rsqrt(jnp.sum(kkf * kkf, axis=-1, keepdims=True) + 1e-12)).reshape(Bq, L, D_RWKV)

    def dir_gates(xw, xa, d):
        w_raw = (w0[d] + jnp.tanh(xw) @ w_up[d]).astype(F32)
        return -jnp.exp(-jax.nn.softplus(-w_raw) - 0.5), jax.nn.sigmoid(a0[d] + xa @ a_up[d])

    kk_ctx, kk_lat = unit_key(p_ctx[4]), unit_key(p_lat[4])
    st0 = jnp.zeros((B, RWKV_HEADS // 2, PAIR, PAIR), F32)
    y_lat, y_ctx = 0.0, 0.0
    for d, rev in ((0, False), (1, True)):
        yc, st_c = rwkv_chunked(*p_ctx[3:6], kk_ctx, *dir_gates(*p_ctx[6:8], d), k_a, st0, reverse=rev)
        yl, _ = rwkv_chunked(*p_lat[3:6], kk_lat, *dir_gates(*p_lat[6:8], d), k_a, st_c, reverse=rev)
        y_lat = y_lat + yl
        if need_ctx:
            y_ctx = y_ctx + yc
    cat_lat = jnp.concatenate([conv_branch(*p_lat[0:3]),
                               rwkv_readout(y_lat, *p_lat[3:6], p_lat[8], r_k, g_up, ln_g, ln_b)], axis=-1)
    cat_ctx = None
    if need_ctx:
        cat_ctx = jnp.concatenate([conv_branch(*p_ctx[0:3]),
                                   rwkv_readout(y_ctx, *p_ctx[3:6], p_ctx[8], r_k, g_up, ln_g, ln_b)], axis=-1)
    return cat_lat, cat_ctx


def ctx_attention(q, k, v):
    s = jnp.einsum('bhqd,bhkd->bhqk', q, k).astype(F32) * (q.shape[-1] ** -0.5)
    p = jax.nn.softmax(s, axis=-1).astype(v.dtype)
    return jnp.einsum('bhqk,bhkd->bhqd', p, v)


def gla_log_gate(ga, a_up_d, a_b_d):
    B, L = ga.shape[:2]
    lg = jax.nn.log_sigmoid((ga @ a_up_d + a_b_d).astype(F32)) / GLA_GATE_TEMP
    return lg.reshape(B, L, GLA_HEADS, GLA_DK)


def gla_readout(o, gr, ln_g):
    B, L = gr.shape[:2]
    return (rmsnorm(o, ln_g).reshape(B, L, D_GLA_V) * jax.nn.silu(gr)).astype(gr.dtype)


def odd_mixer(proj_lat, proj_ctx, qn_g, kn_g, rpb, a_up, a_b, gla_ln_g, need_ctx):
    nq, nk, nv, gq, gk, gv, gr, ga = proj_lat
    cnq, cnk, cnv, cgq, cgk, cgv, cgr, cga = proj_ctx
    B, L = nq.shape[:2]
    Lc = cnq.shape[1]

    def nat_heads(t, g=None):
        t = t.reshape(t.shape[0], t.shape[1], NAT_HEADS, NAT_HEAD)
        if g is not None:
            t = rmsnorm(t, g)
        return jnp.swapaxes(t, 1, 2)

    nat_lat = nat_attention(nq, nk, nv, cnk, cnv, rpb, qn_g, kn_g)

    gh = lambda t, d: t.reshape(t.shape[0], t.shape[1], GLA_HEADS, d)
    qscale = GLA_DK ** -0.5
    q = (rope_2d(gh(gq, GLA_DK)) * qscale).reshape(B, L, D_GLA_K)
    k = rope_2d(gh(gk, GLA_DK)).reshape(B, L, D_GLA_K)
    qc = cgq * qscale
    st0 = jnp.zeros((B, GLA_HEADS // 2, GLA_VPAIR, GLA_KPAIR), F32)
    o_lat, o_ctx = 0.0, 0.0
    for d in range(2):
        lg_c = gla_log_gate(cga, a_up[d], a_b[d]).reshape(B, Lc, D_GLA_K)
        lg_l = gla_log_gate(ga, a_up[d], a_b[d]).reshape(B, L, D_GLA_K)
        oc, st_c = gla_scan(qc, cgk, cgv, lg_c, st0, reverse=(d == 1))
        ol, _ = gla_scan(q, k, gv, lg_l, st_c, reverse=(d == 1))
        o_lat = o_lat + ol
        if need_ctx:
            o_ctx = o_ctx + oc
    o_lat = gh(o_lat, GLA_DV)
    if need_ctx:
        o_ctx = gh(o_ctx, GLA_DV)
    cat_lat = jnp.concatenate([nat_lat, gla_readout(o_lat, gr, gla_ln_g)], axis=-1)
    cat_ctx = None
    if need_ctx:
        kc, vc = nat_heads(cnk, kn_g), nat_heads(cnv)
        nat_ctx = jnp.swapaxes(ctx_attention(nat_heads(cnq, qn_g), kc, vc), 1, 2).reshape(B, Lc, D_NAT)
        cat_ctx = jnp.concatenate([nat_ctx, gla_readout(o_ctx, cgr, gla_ln_g)], axis=-1)
    return cat_lat, cat_ctx


def kernel(x, c, ctx, c_ctx, ada_w, ada_b, norm1_g, norm2_g, ev_w_in, ev_w_out, conv_w, rw_k_k, rw_k_a, rw_r_k, rw_w0, rw_w_up, rw_a0, rw_a_up, rw_g_up, rw_ln_g, rw_ln_b, od_w_in, od_w_out, nat_qn_g, nat_kn_g, nat_rpb, gla_a_up, gla_a_b, gla_ln_g, moe_router, moe_w1, moe_w3, moe_w2):
    depth = ada_w.shape[0]
    ctx_s = ctx
    silu_c = jax.nn.silu(c)
    silu_cc = jax.nn.silu(c_ctx)
    for l in range(depth):
        last = l == depth - 1
        j = l // 2
        sh1, sc1, gt1, sh2, sc2, gt2 = _split((silu_c @ ada_w[l] + ada_b[l])[:, None, :], [D_MODEL] * 6)
        csh1, csc1, cgt1, csh2, csc2, cgt2 = _split(silu_cc @ ada_w[l] + ada_b[l], [D_MODEL] * 6)
        even = l % 2 == 0
        w_in = ev_w_in[j] if even else od_w_in[j]
        sizes = EVEN_SPLIT[:6] + [sum(EVEN_TAIL)] if even else ODD_SPLIT
        proj_lat = in_proj_split(x, norm1_g[l], sh1, sc1, w_in, sizes)
        proj_ctx = in_proj_split(ctx_s, norm1_g[l], csh1, csc1, w_in, sizes)
        if even:
            cat_lat, cat_ctx = even_mixer(proj_lat, proj_ctx, conv_w[j], rw_k_k[j], rw_k_a[j], rw_r_k[j],
                                          rw_w0[j], rw_w_up[j], rw_a0[j], rw_a_up[j], rw_g_up[j], rw_ln_g[j],
                                          rw_ln_b[j], not last)
            w_out = ev_w_out[j]
        else:
            cat_lat, cat_ctx = odd_mixer(proj_lat, proj_ctx, nat_qn_g[j], nat_kn_g[j], nat_rpb[j],
                                         gla_a_up[j], gla_a_b[j], gla_ln_g[j], not last)
            w_out = od_w_out[j]
        experts = (moe_w1[l], moe_w3[l], moe_w2[l])
        x = out_proj_residual(cat_lat, w_out, x, gt1)
        x = x + gt2 * ec_moe(modulate(x, norm2_g[l], sh2, sc2), moe_router[l], *experts)
        if not last:
            ctx_s = out_proj_residual(cat_ctx, w_out, ctx_s, cgt1)
            ctx_s = ctx_s + cgt2 * ec_moe(modulate(ctx_s, norm2_g[l], csh2, csc2), moe_router[l], *experts)
    return x
```

```python
import functools

import jax
import jax.numpy as jnp
import numpy as np
from jax import lax
from jax.experimental import pallas as pl
from jax.experimental.pallas import tpu as pltpu

D_MODEL = 1024
GRID_W = 64
NORM_EPS = 1e-6
F32 = jnp.float32
BF16 = jnp.bfloat16

D_CONV = 512
D_RWKV = 512
RWKV_HEAD = 64
RWKV_HEADS = D_RWKV // RWKV_HEAD
RWKV_DECAY_RANK = 64
RWKV_ICLR_RANK = 64
RWKV_GATE_RANK = 128
RWKV_LN_EPS = 64e-5
D_NAT = 512
NAT_HEAD = 64
NAT_HEADS = D_NAT // NAT_HEAD
NAT_ROWS = 8
NAT_COLS = 16
GLA_HEADS = 4
GLA_DK = 64
GLA_DV = 128
D_GLA_K = GLA_HEADS * GLA_DK
D_GLA_V = GLA_HEADS * GLA_DV
GLA_GATE_RANK = 16
GLA_GATE_TEMP = 16.0
GLA_CHUNK = 64
ROPE_BASE = 10000.0
N_EXPERTS = 16
EC_CAPACITY_FACTOR = 2

EVEN_SPLIT = [D_CONV, D_CONV, D_CONV, D_RWKV, D_RWKV, D_RWKV, RWKV_DECAY_RANK, RWKV_ICLR_RANK, RWKV_GATE_RANK]
ODD_SPLIT = [D_NAT, D_NAT, D_NAT, D_GLA_K, D_GLA_K, D_GLA_V, D_GLA_V, GLA_GATE_RANK]


def _out_proj_body(a_ref, w_ref, x_ref, g_ref, o_ref):
    acc = jnp.dot(a_ref[0].astype(BF16), w_ref[...], preferred_element_type=F32)
    o_ref[0] = x_ref[0] + g_ref[0] * acc


def out_proj_residual(a, w, x, gate, block_rows=512):
    B, L, K = a.shape
    N = w.shape[1]
    tm = min(block_rows, L)
    assert L % tm == 0
    gate = jnp.broadcast_to(gate.reshape(-1, 1, N), (B, 1, N))
    return pl.pallas_call(
        _out_proj_body,
        grid=(B, L // tm),
        in_specs=[
            pl.BlockSpec((1, tm, K), lambda b, i: (b, i, 0)),
            pl.BlockSpec((K, N), lambda b, i: (0, 0)),
            pl.BlockSpec((1, tm, N), lambda b, i: (b, i, 0)),
            pl.BlockSpec((1, 1, N), lambda b, i: (b, 0, 0)),
        ],
        out_specs=pl.BlockSpec((1, tm, N), lambda b, i: (b, i, 0)),
        out_shape=jax.ShapeDtypeStruct((B, L, N), F32),
        compiler_params=pltpu.CompilerParams(dimension_semantics=("parallel", "parallel")),
        name="out_proj_residual",
    )(a, w.astype(BF16), x, gate)


def _in_proj_body(x_ref, g_ref, shift_ref, scale_ref, w_ref, *o_refs, offsets):
    xf = x_ref[0]
    y = xf * lax.rsqrt(jnp.mean(xf * xf, axis=-1, keepdims=True) + NORM_EPS) * g_ref[...]
    h = (y * (1.0 + scale_ref[0]) + shift_ref[0]).astype(BF16)
    for o_ref, off in zip(o_refs, offsets):
        o_ref[0] = jnp.dot(h, w_ref[:, off:off + o_ref.shape[-1]], preferred_element_type=F32)


def in_proj_split(x, g, shift, scale, w, sizes, block_rows=512):
    B, L, D = x.shape
    N = w.shape[1]
    assert sum(sizes) == N
    tm = min(block_rows, L)
    assert L % tm == 0
    offsets = tuple(int(o) for o in np.cumsum([0] + list(sizes[:-1])))
    per_sample = lambda t: jnp.broadcast_to(t.reshape(-1, 1, D), (B, 1, D))
    vec = pl.BlockSpec((1, 1, D), lambda b, i: (b, 0, 0))
    return pl.pallas_call(
        functools.partial(_in_proj_body, offsets=offsets),
        grid=(B, L // tm),
        in_specs=[pl.BlockSpec((1, tm, D), lambda b, i: (b, i, 0)), pl.BlockSpec((1, D), lambda b, i: (0, 0)),
                  vec, vec, pl.BlockSpec((D, N), lambda b, i: (0, 0))],
        out_specs=[pl.BlockSpec((1, tm, s), lambda b, i: (b, i, 0)) for s in sizes],
        out_shape=[jax.ShapeDtypeStruct((B, L, s), F32) for s in sizes],
        compiler_params=pltpu.CompilerParams(dimension_semantics=("parallel", "parallel"),
                                             vmem_limit_bytes=48 * 1024 * 1024),
        name="in_proj_split",
    )(x, g.reshape(1, D), per_sample(shift), per_sample(scale), w.astype(BF16))


RWKV_CHUNK = 64
RWKV_CHUNKS_PER_STEP = 2
PAIR = 2 * RWKV_HEAD


def _mm(a, b):
    return jnp.dot(a.astype(BF16), b.astype(BF16), preferred_element_type=F32)


def _mm_nt(a, b):
    return lax.dot_general(a.astype(BF16), b.astype(BF16), (((1,), (1,)), ((), ())), preferred_element_type=F32)


def _mm_tn(a, b):
    return lax.dot_general(a.astype(BF16), b.astype(BF16), (((0,), (0,)), ((), ())), preferred_element_type=F32)


def _rwkv_chunk_pairs(ins, sts, slots, *, reverse):
    C = RWKV_CHUNK
    row = lax.broadcasted_iota(jnp.int32, (PAIR, PAIR), 0)
    col = lax.broadcasted_iota(jnp.int32, (PAIR, PAIR), 1)
    same_head = (row // RWKV_HEAD) == (col // RWKV_HEAD)
    t_i, s_i = row % C, col % C
    before = (s_i > t_i) if reverse else (s_i < t_i)
    upto = before | (s_i == t_i)

    ct = lax.broadcasted_iota(jnp.int32, (C, C), 0)
    cs = lax.broadcasted_iota(jnp.int32, (C, C), 1)
    tri = ((cs >= ct) if reverse else (cs <= ct)).astype(F32)
    eye = jnp.where(row == col, 1.0, 0.0)

    def expand(x):
        return jnp.where(same_head, jnp.concatenate([x, x], axis=0), 0.0)

    def prepare(r, k, v, kk, lw, a, ka):
        keff = k * (1.0 + (a - 1.0) * ka)
        b = kk * a
        cum_in = jnp.dot(tri, lw, preferred_element_type=F32, precision=lax.Precision.HIGHEST)
        cum_ex = cum_in - lw
        tot = jnp.sum(lw, axis=0, keepdims=True)
        e_neg = jnp.exp(-cum_in)
        e_rem = jnp.exp(tot - cum_in)
        return dict(at2=expand(-kk * jnp.exp(cum_ex)), rt2=expand(r * jnp.exp(cum_in)),
                    bh2=expand(b * e_neg), kh2=expand(keff * e_neg), bp2=expand(b * e_rem),
                    kp2=expand(keff * e_rem), v2=expand(v), dtot=eye * jnp.exp(tot))

    ps = [prepare(*args) for args in ins]
    pps = [_mm_nt(jnp.concatenate([p["at2"], p["rt2"]], axis=0), jnp.concatenate([p["bh2"], p["kh2"]], axis=0))
           for p in ps]
    a_ab = [jnp.where(before, pp[:PAIR, :PAIR], 0.0) for pp in pps]
    a_ak = [jnp.where(before, pp[:PAIR, PAIR:], 0.0) for pp in pps]
    a_rb = [jnp.where(upto, pp[PAIR:, :PAIR], 0.0) for pp in pps]
    a_rk = [jnp.where(upto, pp[PAIR:, PAIR:], 0.0) for pp in pps]

    tinv = [eye + n for n in a_ab]
    npow = a_ab
    for _ in range(int(np.log2(C)) - 1):
        npow = [_mm(n, n) for n in npow]
        tinv = [t + _mm(t, n) for t, n in zip(tinv, npow)]

    av = [_mm(jnp.concatenate([ak, rk], axis=0), p["v2"]) for ak, rk, p in zip(a_ak, a_rk, ps)]
    x = [_mm(t, jnp.concatenate([p["at2"], w[:PAIR]], axis=1)) for t, p, w in zip(tinv, ps, av)]
    z = [_mm(rb, xx) for rb, xx in zip(a_rb, x)]
    bx = [_mm_tn(p["bp2"], xx) for p, xx in zip(ps, x)]
    kv = [_mm_tn(p["kp2"], p["v2"]) for p in ps]
    qe = [p["rt2"] + zz[:, :PAIR] for p, zz in zip(ps, z)]
    yloc = [zz[:, PAIR:] + w[PAIR:] for zz, w in zip(z, av)]
    mt = [p["dtot"] + b[:, :PAIR] for p, b in zip(ps, bx)]
    gt = [b[:, PAIR:] + g for b, g in zip(bx, kv)]
    sts = list(sts)
    ys = []
    for i, s in enumerate(slots):
        y2 = _mm(qe[i], sts[s]) + yloc[i]
        sts[s] = _mm(mt[i], sts[s]) + gt[i]
        ys.append(y2[:C] + y2[C:])
    return ys, sts


def _rwkv_chunk_body(r_ref, k_ref, v_ref, kk_ref, lw_ref, a_ref, ka_ref, s0_ref, y_ref, sT_ref, st_scr, *, reverse):
    c = pl.program_id(1)

    @pl.when(c == 0)
    def _():
        st_scr[...] = s0_ref[0]

    C = RWKV_CHUNK
    npair = st_scr.shape[0]
    nchunk = r_ref.shape[1] // C
    order = range(nchunk - 1, -1, -1) if reverse else range(nchunk)
    chains = [(slice(u * C, (u + 1) * C), slice(p * PAIR, (p + 1) * PAIR), p) for u in order for p in range(npair)]
    ins = [(r_ref[0, rs, ls], k_ref[0, rs, ls], v_ref[0, rs, ls], kk_ref[0, rs, ls], lw_ref[0, rs, ls],
            a_ref[0, rs, ls], ka_ref[:, ls]) for rs, ls, _ in chains]
    ys, sts = _rwkv_chunk_pairs(ins, [st_scr[p] for p in range(npair)], [p for _, _, p in chains], reverse=reverse)
    for (rs, ls, _), y in zip(chains, ys):
        y_ref[0, rs, ls] = y
    for p in range(npair):
        st_scr[p] = sts[p]

    @pl.when(c == pl.num_programs(1) - 1)
    def _():
        sT_ref[0] = st_scr[...]


def rwkv_state_pack(S):
    B, H = S.shape[:2]
    St = jnp.swapaxes(S, -1, -2).reshape(B, H // 2, 2, RWKV_HEAD, RWKV_HEAD)
    z = jnp.zeros_like(St[:, :, 0])
    top = jnp.concatenate([St[:, :, 0], z], axis=-1)
    bot = jnp.concatenate([z, St[:, :, 1]], axis=-1)
    return jnp.concatenate([top, bot], axis=-2)


def rwkv_state_unpack(St):
    B, P = St.shape[:2]
    h0 = St[:, :, :RWKV_HEAD, :RWKV_HEAD]
    h1 = St[:, :, RWKV_HEAD:, RWKV_HEAD:]
    return jnp.swapaxes(jnp.stack([h0, h1], axis=2).reshape(B, 2 * P, RWKV_HEAD, RWKV_HEAD), -1, -2)


def rwkv_chunked(r, k, v, kk, lw, a, k_a, st0, *, reverse, interpret=False):
    B, L, D = r.shape
    T = RWKV_CHUNK * RWKV_CHUNKS_PER_STEP
    assert L % T == 0 and D % PAIR == 0
    n, P = L // T, D // PAIR
    cidx = (lambda c: n - 1 - c) if reverse else (lambda c: c)
    seq = pl.BlockSpec((1, T, D), lambda b, c: (b, cidx(c), 0))
    state = pl.BlockSpec((1, P, PAIR, PAIR), lambda b, c: (b, 0, 0, 0))
    return pl.pallas_call(
        functools.partial(_rwkv_chunk_body, reverse=reverse),
        grid=(B, n),
        in_specs=[seq] * 6 + [pl.BlockSpec((1, D), lambda b, c: (0, 0)), state],
        out_specs=[seq, state],
        out_shape=[jax.ShapeDtypeStruct((B, L, D), F32), jax.ShapeDtypeStruct((B, P, PAIR, PAIR), F32)],
        scratch_shapes=[pltpu.VMEM((P, PAIR, PAIR), F32)],
        compiler_params=pltpu.CompilerParams(dimension_semantics=("parallel", "arbitrary")),
        name="rwkv_chunked_rev" if reverse else "rwkv_chunked_fwd",
        interpret=interpret,
    )(r, k, v, kk, lw, a, k_a.reshape(1, D), st0)


NAT_WIN = NAT_ROWS * GRID_W
NAT_ROWS_PER_BLOCK = 16
NAT_MASKED = -1e30


def _pair_rmsnorm(x, g, lane_lo):
    sq = x * x
    s_lo = jnp.sum(jnp.where(lane_lo, sq, 0.0), axis=-1, keepdims=True)
    s_hi = jnp.sum(sq, axis=-1, keepdims=True) - s_lo
    ms = jnp.where(lane_lo, s_lo, s_hi) * (1.0 / NAT_HEAD)
    return x * lax.rsqrt(ms + NORM_EPS) * g


def _nat_body(q_ref, k_ref, v_ref, kc_ref, vc_ref, bias_ref, qg_ref, kg_ref, o_ref, kn_scr, vb_scr, kcn_scr, vcb_scr,
              *, rows):
    rb = pl.program_id(2)
    L = k_ref.shape[1]
    norm_rows = 512
    lane_lo_n = lax.broadcasted_iota(jnp.int32, (norm_rows, PAIR), 1) < NAT_HEAD

    @pl.when(rb == 0)
    def _():
        def norm_block(i, carry):
            sl = pl.ds(pl.multiple_of(i * norm_rows, norm_rows), norm_rows)
            kn_scr[sl, :] = _pair_rmsnorm(k_ref[0, sl, :], kg_ref[...], lane_lo_n).astype(BF16)
            vb_scr[sl, :] = v_ref[0, sl, :].astype(BF16)
            return carry
        lax.fori_loop(0, L // norm_rows, norm_block, 0)
        lane_lo_c = lax.broadcasted_iota(jnp.int32, kc_ref.shape[1:], 1) < NAT_HEAD
        kcn_scr[...] = _pair_rmsnorm(kc_ref[0], kg_ref[...], lane_lo_c).astype(BF16)
        vcb_scr[...] = vc_ref[0].astype(BF16)

    lane_lo = lax.broadcasted_iota(jnp.int32, (GRID_W, PAIR), 1) < NAT_HEAD
    scale = NAT_HEAD ** -0.5

    rows_per_iter = 4
    nt = (((1,), (1,)), ((), ()))

    def row_group(it, carry):
        qsl, kwin, vwin, qh, bias = [], [], [], [], []
        for u in range(rows_per_iter):
            j = it * rows_per_iter + u
            r = rb * NAT_ROWS_PER_BLOCK + j
            rs = jnp.clip(r - NAT_ROWS // 2, 0, rows - NAT_ROWS)
            sl = pl.ds(pl.multiple_of(j * GRID_W, GRID_W), GRID_W)
            wsl = pl.ds(pl.multiple_of(rs * GRID_W, GRID_W), NAT_WIN)
            qn = _pair_rmsnorm(q_ref[0, sl, :], qg_ref[...], lane_lo) * scale
            for h in range(2):
                qsl.append(sl)
                kwin.append(kn_scr[wsl, :])
                vwin.append(vb_scr[wsl, :])
                qh.append(jnp.where(lane_lo if h == 0 else ~lane_lo, qn, 0.0).astype(BF16))
                bias.append(bias_ref[h, r - rs])
        chains = range(2 * rows_per_iter)
        s_loc = [lax.dot_general(qh[i], kwin[i], nt, preferred_element_type=F32) + bias[i] for i in chains]
        s_ctx = [lax.dot_general(qh[i], kcn_scr[...], nt, preferred_element_type=F32) for i in chains]
        m = [jnp.maximum(jnp.max(s_loc[i], axis=-1, keepdims=True), jnp.max(s_ctx[i], axis=-1, keepdims=True))
             for i in chains]
        p_loc = [jnp.exp(s_loc[i] - m[i]) for i in chains]
        p_ctx = [jnp.exp(s_ctx[i] - m[i]) for i in chains]
        den = [jnp.sum(p_loc[i], axis=-1, keepdims=True) + jnp.sum(p_ctx[i], axis=-1, keepdims=True) for i in chains]
        o = [(jnp.dot(p_loc[i].astype(BF16), vwin[i], preferred_element_type=F32)
              + jnp.dot(p_ctx[i].astype(BF16), vcb_scr[...], preferred_element_type=F32)) / den[i] for i in chains]
        for u in range(rows_per_iter):
            o_ref[0, qsl[2 * u], :] = jnp.where(lane_lo, o[2 * u], o[2 * u + 1])
        return carry

    lax.fori_loop(0, NAT_ROWS_PER_BLOCK // rows_per_iter, row_group, 0)


def _nat_bias_table(rpb):
    col = jnp.arange(GRID_W)
    cstart = jnp.clip(col - NAT_COLS // 2, 0, GRID_W - NAT_COLS)
    delta = jnp.arange(NAT_ROWS)
    wrow = jnp.arange(NAT_ROWS)
    ridx = wrow[None, :] - delta[:, None] + (NAT_ROWS - 1)
    cidx = col[None, :] - col[:, None] + (NAT_COLS - 1)
    inwin = (col[None, :] >= cstart[:, None]) & (col[None, :] < cstart[:, None] + NAT_COLS)
    rsel = (ridx[:, :, None] == jnp.arange(2 * NAT_ROWS - 1)).astype(F32)
    csel = ((cidx[:, :, None] == jnp.arange(2 * NAT_COLS - 1)) & inwin[:, :, None]).astype(F32)
    tab = jnp.einsum('hab,dia,ckb->hdcik', rpb.astype(F32), rsel, csel, precision=lax.Precision.HIGHEST)
    tab = jnp.where(inwin[None, None, :, None, :], tab, NAT_MASKED)
    return tab.reshape(rpb.shape[0], NAT_ROWS, GRID_W, NAT_WIN)


def nat_attention(q, k, v, kc, vc, rpb, qn_g, kn_g, interpret=False):
    B, L, D = q.shape
    Lc = kc.shape[1]
    rows = L // GRID_W
    P = D // PAIR
    rpb_blk = NAT_ROWS_PER_BLOCK
    assert rows >= NAT_ROWS and rows % rpb_blk == 0 and L % 512 == 0
    bias = _nat_bias_table(rpb)
    g2 = lambda g: jnp.tile(g, 2).reshape(1, PAIR)
    full = pl.BlockSpec((1, L, PAIR), lambda b, p, i: (b, 0, p))
    cfull = pl.BlockSpec((1, Lc, PAIR), lambda b, p, i: (b, 0, p))
    qblk = pl.BlockSpec((1, rpb_blk * GRID_W, PAIR), lambda b, p, i: (b, i, p))
    gspec = pl.BlockSpec((1, PAIR), lambda b, p, i: (0, 0))
    return pl.pallas_call(
        functools.partial(_nat_body, rows=rows),
        grid=(B, P, rows // rpb_blk),
        in_specs=[qblk, full, full, cfull, cfull,
                  pl.BlockSpec((2, NAT_ROWS, GRID_W, NAT_WIN), lambda b, p, i: (p, 0, 0, 0)), gspec, gspec],
        out_specs=qblk,
        out_shape=jax.ShapeDtypeStruct((B, L, D), F32),
        scratch_shapes=[pltpu.VMEM((L, PAIR), BF16), pltpu.VMEM((L, PAIR), BF16),
                        pltpu.VMEM((Lc, PAIR), BF16), pltpu.VMEM((Lc, PAIR), BF16)],
        compiler_params=pltpu.CompilerParams(dimension_semantics=("parallel", "parallel", "arbitrary"),
                                             vmem_limit_bytes=48 * 1024 * 1024),
        name="nat_attention",
        interpret=interpret,
    )(q, k, v, kc, vc, bias, g2(qn_g), g2(kn_g))


GLA_SUB = 16
GLA_BLOCK = 256
GLA_KPAIR = 2 * GLA_DK
GLA_VPAIR = 2 * GLA_DV
GLA_UNROLL = 4


def _gla_body(q_ref, k_ref, v_ref, g_ref, s0_ref, o_ref, sT_ref, st_scr, *, reverse):
    C = GLA_SUB
    nsub = GLA_BLOCK // C
    npair = st_scr.shape[0]
    blk = pl.program_id(1)

    @pl.when(blk == 0)
    def _():
        st_scr[...] = s0_ref[0]

    ti = lax.broadcasted_iota(jnp.int32, (C, C), 0)
    si = lax.broadcasted_iota(jnp.int32, (C, C), 1)
    tri = ((si >= ti) if reverse else (si <= ti)).astype(F32)
    lane_lo = lax.broadcasted_iota(jnp.int32, (C, GLA_KPAIR), 1) < GLA_DK
    row_id = lax.broadcasted_iota(jnp.int32, (C, GLA_KPAIR), 0)
    vrow = lax.broadcasted_iota(jnp.int32, (2 * C, GLA_VPAIR), 0)
    vcol = lax.broadcasted_iota(jnp.int32, (2 * C, GLA_VPAIR), 1)
    v_same_head = (vrow // C) == (vcol // GLA_DV)
    srow = lax.broadcasted_iota(jnp.int32, (GLA_VPAIR, GLA_KPAIR), 0)
    scol = lax.broadcasted_iota(jnp.int32, (GLA_VPAIR, GLA_KPAIR), 1)
    s_same_head = (srow // GLA_DV) == (scol // GLA_DK)

    pairs = range(npair)
    klanes = [slice(p * GLA_KPAIR, (p + 1) * GLA_KPAIR) for p in pairs]
    vlanes = [slice(p * GLA_VPAIR, (p + 1) * GLA_VPAIR) for p in pairs]

    def sub_chunks(it, carry):
        chains = [(u, p) for u in range(GLA_UNROLL) for p in pairs]
        sl, q, k, g, v = {}, {}, {}, {}, {}
        for u in range(GLA_UNROLL):
            i = it * GLA_UNROLL + u
            ci = (nsub - 1 - i) if reverse else i
            sl[u] = pl.ds(pl.multiple_of(ci * C, C), C)
            for p in pairs:
                q[u, p], k[u, p], g[u, p] = (r[0, sl[u], klanes[p]] for r in (q_ref, k_ref, g_ref))
                v[u, p] = v_ref[0, sl[u], vlanes[p]]
        b = {c: jnp.dot(tri, g[c], preferred_element_type=F32, precision=lax.Precision.HIGHEST) for c in chains}
        b_end = {c: jnp.sum(g[c], axis=0, keepdims=True) for c in chains}
        a_lo = {c: jnp.zeros((C, C), F32) for c in chains}
        a_hi = {c: jnp.zeros((C, C), F32) for c in chains}
        for j in range(C):
            seen = (row_id <= j) if reverse else (row_id >= j)
            for c in chains:
                decay = jnp.exp(jnp.where(seen, b[c] - b[c][j:j + 1, :], 0.0))
                f = jnp.where(seen, q[c] * k[c][j:j + 1, :] * decay, 0.0)
                r_lo = jnp.sum(jnp.where(lane_lo, f, 0.0), axis=-1, keepdims=True)
                r_hi = jnp.sum(jnp.where(lane_lo, 0.0, f), axis=-1, keepdims=True)
                a_lo[c] = jnp.where(si == j, r_lo, a_lo[c])
                a_hi[c] = jnp.where(si == j, r_hi, a_hi[c])
        v_bd = {c: jnp.where(v_same_head, jnp.concatenate([v[c], v[c]], axis=0), 0.0) for c in chains}
        o_in = {c: _mm(jnp.concatenate([a_lo[c], a_hi[c]], axis=1), v_bd[c]) for c in chains}
        kv = {c: jnp.where(s_same_head, _mm_tn(v[c], k[c] * jnp.exp(b_end[c] - b[c])), 0.0) for c in chains}
        qd = {c: q[c] * jnp.exp(b[c]) for c in chains}
        e_end = {c: jnp.exp(b_end[c]) for c in chains}
        st = [st_scr[p] for p in pairs]
        for u in range(GLA_UNROLL):
            for p in pairs:
                o_ref[0, sl[u], vlanes[p]] = o_in[u, p] + _mm_nt(qd[u, p], st[p])
                st[p] = st[p] * e_end[u, p] + kv[u, p]
        for p in pairs:
            st_scr[p] = st[p]
        return carry

    lax.fori_loop(0, nsub // GLA_UNROLL, sub_chunks, 0)

    @pl.when(blk == pl.num_programs(1) - 1)
    def _():
        sT_ref[0] = st_scr[...]


def gla_scan(q, k, v, logg, st0, *, reverse, interpret=False):
    B, L, Dk = q.shape
    Dv = v.shape[-1]
    T = GLA_BLOCK
    assert L % T == 0 and Dk % GLA_KPAIR == 0
    n, P = L // T, Dk // GLA_KPAIR
    bidx = (lambda c: n - 1 - c) if reverse else (lambda c: c)
    kspec = pl.BlockSpec((1, T, Dk), lambda b, c: (b, bidx(c), 0))
    vspec = pl.BlockSpec((1, T, Dv), lambda b, c: (b, bidx(c), 0))
    sspec = pl.BlockSpec((1, P, GLA_VPAIR, GLA_KPAIR), lambda b, c: (b, 0, 0, 0))
    return pl.pallas_call(
        functools.partial(_gla_body, reverse=reverse),
        grid=(B, n),
        in_specs=[kspec, kspec, vspec, kspec, sspec],
        out_specs=[vspec, sspec],
        out_shape=[jax.ShapeDtypeStruct((B, L, Dv), F32),
                   jax.ShapeDtypeStruct((B, P, GLA_VPAIR, GLA_KPAIR), F32)],
        scratch_shapes=[pltpu.VMEM((P, GLA_VPAIR, GLA_KPAIR), F32)],
        compiler_params=pltpu.CompilerParams(dimension_semantics=("parallel", "arbitrary")),
        name="gla_scan_rev" if reverse else "gla_scan_fwd",
        interpret=interpret,
    )(q, k, v, logg, st0)


MOE_F_TILE = 512


def _expert_ffn_body(x_ref, g_ref, w1_ref, w3_ref, w2_ref, o_ref, acc_ref):
    f = pl.program_id(3)

    @pl.when(f == 0)
    def _():
        acc_ref[...] = jnp.zeros_like(acc_ref)

    x = x_ref[0, 0].astype(BF16)
    h1 = jnp.dot(x, w1_ref[0].astype(BF16), preferred_element_type=F32)
    h3 = jnp.dot(x, w3_ref[0].astype(BF16), preferred_element_type=F32)
    hid = (h1 * jax.nn.sigmoid(h1) * h3).astype(BF16)
    acc_ref[...] += jnp.dot(hid, w2_ref[0].astype(BF16), preferred_element_type=F32)

    @pl.when(f == pl.num_programs(3) - 1)
    def _():
        o_ref[0, 0] = acc_ref[...] * g_ref[0, 0]


def expert_ffn(xin, gate, w1, w3, w2, layer, interpret=False):
    B, E, cap, D = xin.shape
    F = w1.shape[-1]
    tm = min(cap, 1024)
    tf = MOE_F_TILE
    assert cap % tm == 0 and F % tf == 0
    return pl.pallas_call(
        _expert_ffn_body,
        grid=(E, B, cap // tm, F // tf),
        in_specs=[
            pl.BlockSpec((1, 1, tm, D), lambda e, b, i, f: (b, e, i, 0)),
            pl.BlockSpec((1, 1, tm, 1), lambda e, b, i, f: (b, e, i, 0)),
            pl.BlockSpec((None, 1, D, tf), lambda e, b, i, f: (layer, e, 0, f)),
            pl.BlockSpec((None, 1, D, tf), lambda e, b, i, f: (layer, e, 0, f)),
            pl.BlockSpec((None, 1, tf, D), lambda e, b, i, f: (layer, e, f, 0)),
        ],
        out_specs=pl.BlockSpec((1, 1, tm, D), lambda e, b, i, f: (b, e, i, 0)),
        out_shape=jax.ShapeDtypeStruct((B, E, cap, D), F32),
        scratch_shapes=[pltpu.VMEM((tm, D), F32)],
        compiler_params=pltpu.CompilerParams(
            dimension_semantics=("parallel", "parallel", "parallel", "arbitrary"),
            vmem_limit_bytes=56 * 1024 * 1024),
        name="expert_ffn",
        interpret=interpret,
    )(xin, gate[..., None], w1, w3, w2)


def _split(p, sizes):
    return jnp.split(p, np.cumsum(sizes)[:-1].tolist(), axis=-1)


def rmsnorm(x, g, eps=NORM_EPS):
    xf = x.astype(F32)
    y = xf * lax.rsqrt(jnp.mean(xf * xf, axis=-1, keepdims=True) + eps)
    return (y * g.astype(F32)).astype(x.dtype)


def modulate(x, g, shift, scale):
    return rmsnorm(x, g) * (1 + scale) + shift


def conv3(u, w):
    up = jnp.pad(u, ((0, 0), (1, 1), (0, 0)))
    return up[:, :-2] * w[0] + up[:, 1:-1] * w[1] + up[:, 2:] * w[2]


def rope_2d(x):
    L, d = x.shape[1], x.shape[-1]
    half, nf = d // 2, d // 4
    t = jnp.arange(L)
    inv = ROPE_BASE ** (-jnp.arange(nf, dtype=F32) / nf)

    def rot(u, pos):
        ang = pos.astype(F32)[:, None] * inv[None, :]
        cos, sin = jnp.cos(ang)[None, :, None, :], jnp.sin(ang)[None, :, None, :]
        u1, u2 = u[..., :nf].astype(F32), u[..., nf:].astype(F32)
        return jnp.concatenate([u1 * cos - u2 * sin, u1 * sin + u2 * cos], axis=-1)

    return jnp.concatenate([rot(x[..., :half], t // GRID_W), rot(x[..., half:], t % GRID_W)], axis=-1).astype(x.dtype)


def ec_moe(h, router, w1, w3, w2, layer):
    B, T, D = h.shape
    cap = EC_CAPACITY_FACTOR * T // N_EXPERTS
    aff = jax.nn.softmax((h @ router).astype(F32), axis=-1)
    gate, idx = lax.top_k(jnp.swapaxes(aff, 1, 2), cap)
    xin = jax.vmap(lambda hb, ib: hb[ib])(h, idx)
    if B * cap <= 1024:
        merge = lambda t: jnp.swapaxes(t, 0, 1).reshape((1, N_EXPERTS, B * cap) + t.shape[3:])
        y = expert_ffn(merge(xin), merge(gate), w1, w3, w2, layer)
        y = jnp.swapaxes(y.reshape(N_EXPERTS, B, cap, D), 0, 1)
    else:
        y = expert_ffn(xin, gate, w1, w3, w2, layer)
    out = jax.vmap(lambda ib, yb: jnp.zeros((T, D), yb.dtype).at[ib.reshape(-1)].add(yb.reshape(-1, D)))(idx, y)
    return out.astype(h.dtype)


def rwkv_readout(y, r, k, v, xg, r_k, g_up, ln_g, ln_b):
    B, L = r.shape[:2]
    hs = lambda t: t.reshape(B, L, RWKV_HEADS, RWKV_HEAD)
    yf = hs(y.astype(F32))
    mu = jnp.mean(yf, axis=-1, keepdims=True)
    var = jnp.mean(jnp.square(yf - mu), axis=-1, keepdims=True)
    yn = ((yf - mu) * lax.rsqrt(var + RWKV_LN_EPS)).reshape(B, L, D_RWKV) * ln_g + ln_b
    bonus = (jnp.sum(hs(r) * hs(k) * r_k, axis=-1, keepdims=True) * hs(v)).reshape(B, L, D_RWKV)
    g = jax.nn.sigmoid(xg) @ g_up
    return ((yn + bonus) * g).astype(r.dtype)


EVEN_TAIL = EVEN_SPLIT[6:]


def even_mixer(proj_lat, proj_ctx, conv_w, k_k, k_a, r_k, w0, w_up, a0, a_up, g_up, ln_g, ln_b, need_ctx):
    B = proj_lat[0].shape[0]
    p_lat = list(proj_lat[:6]) + _split(proj_lat[6], EVEN_TAIL)
    p_ctx = list(proj_ctx[:6]) + _split(proj_ctx[6], EVEN_TAIL)

    def conv_branch(u, gate_b, gate_c):
        return gate_b * conv3(gate_c * u, conv_w)

    def unit_key(k):
        Bq, L = k.shape[:2]
        kkf = (k * k_k).astype(F32).reshape(Bq, L, RWKV_HEADS, RWKV_HEAD)
        return (kkf * lax.rsqrt(jnp.sum(kkf * kkf, axis=-1, keepdims=True) + 1e-12)).reshape(Bq, L, D_RWKV)

    def dir_gates(xw, xa, d):
        w_raw = (w0[d] + jnp.tanh(xw) @ w_up[d]).astype(F32)
        return -jnp.exp(-jax.nn.softplus(-w_raw) - 0.5), jax.nn.sigmoid(a0[d] + xa @ a_up[d])

    kk_ctx, kk_lat = unit_key(p_ctx[4]), unit_key(p_lat[4])
    st0 = jnp.zeros((B, RWKV_HEADS // 2, PAIR, PAIR), F32)
    y_lat, y_ctx = 0.0, 0.0
    for d, rev in ((0, False), (1, True)):
        yc, st_c = rwkv_chunked(*p_ctx[3:6], kk_ctx, *dir_gates(*p_ctx[6:8], d), k_a, st0, reverse=rev)
        yl, _ = rwkv_chunked(*p_lat[3:6], kk_lat, *dir_gates(*p_lat[6:8], d), k_a, st_c, reverse=rev)
        y_lat = y_lat + yl
        if need_ctx:
            y_ctx = y_ctx + yc
    cat_lat = jnp.concatenate([conv_branch(*p_lat[0:3]),
                               rwkv_readout(y_lat, *p_lat[3:6], p_lat[8], r_k, g_up, ln_g, ln_b)], axis=-1)
    cat_ctx = None
    if need_ctx:
        cat_ctx = jnp.concatenate([conv_branch(*p_ctx[0:3]),
                                   rwkv_readout(y_ctx, *p_ctx[3:6], p_ctx[8], r_k, g_up, ln_g, ln_b)], axis=-1)
    return cat_lat, cat_ctx


def ctx_attention(q, k, v):
    s = jnp.einsum('bhqd,bhkd->bhqk', q, k).astype(F32) * (q.shape[-1] ** -0.5)
    p = jax.nn.softmax(s, axis=-1).astype(v.dtype)
    return jnp.einsum('bhqk,bhkd->bhqd', p, v)


def gla_log_gate(ga, a_up_d, a_b_d):
    B, L = ga.shape[:2]
    lg = jax.nn.log_sigmoid((ga @ a_up_d + a_b_d).astype(F32)) / GLA_GATE_TEMP
    return lg.reshape(B, L, GLA_HEADS, GLA_DK)


def gla_readout(o, gr, ln_g):
    B, L = gr.shape[:2]
    return (rmsnorm(o, ln_g).reshape(B, L, D_GLA_V) * jax.nn.silu(gr)).astype(gr.dtype)


def odd_mixer(proj_lat, proj_ctx, qn_g, kn_g, rpb, a_up, a_b, gla_ln_g, need_ctx):
    nq, nk, nv, gq, gk, gv, gr, ga = proj_lat
    cnq, cnk, cnv, cgq, cgk, cgv, cgr, cga = proj_ctx
    B, L = nq.shape[:2]
    Lc = cnq.shape[1]

    def nat_heads(t, g=None):
        t = t.reshape(t.shape[0], t.shape[1], NAT_HEADS, NAT_HEAD)
        if g is not None:
            t = rmsnorm(t, g)
        return jnp.swapaxes(t, 1, 2)

    nat_lat = nat_attention(nq, nk, nv, cnk, cnv, rpb, qn_g, kn_g)

    gh = lambda t, d: t.reshape(t.shape[0], t.shape[1], GLA_HEADS, d)
    qscale = GLA_DK ** -0.5
    q = (rope_2d(gh(gq, GLA_DK)) * qscale).reshape(B, L, D_GLA_K)
    k = rope_2d(gh(gk, GLA_DK)).reshape(B, L, D_GLA_K)
    qc = cgq * qscale
    st0 = jnp.zeros((B, GLA_HEADS // 2, GLA_VPAIR, GLA_KPAIR), F32)
    o_lat, o_ctx = 0.0, 0.0
    for d in range(2):
        lg_c = gla_log_gate(cga, a_up[d], a_b[d]).reshape(B, Lc, D_GLA_K)
        lg_l = gla_log_gate(ga, a_up[d], a_b[d]).reshape(B, L, D_GLA_K)
        oc, st_c = gla_scan(qc, cgk, cgv, lg_c, st0, reverse=(d == 1))
        ol, _ = gla_scan(q, k, gv, lg_l, st_c, reverse=(d == 1))
        o_lat = o_lat + ol
        if need_ctx:
            o_ctx = o_ctx + oc
    o_lat = gh(o_lat, GLA_DV)
    if need_ctx:
        o_ctx = gh(o_ctx, GLA_DV)
    cat_lat = jnp.concatenate([nat_lat, gla_readout(o_lat, gr, gla_ln_g)], axis=-1)
    cat_ctx = None
    if need_ctx:
        kc, vc = nat_heads(cnk, kn_g), nat_heads(cnv)
        nat_ctx = jnp.swapaxes(ctx_attention(nat_heads(cnq, qn_g), kc, vc), 1, 2).reshape(B, Lc, D_NAT)
        cat_ctx = jnp.concatenate([nat_ctx, gla_readout(o_ctx, cgr, gla_ln_g)], axis=-1)
    return cat_lat, cat_ctx


def kernel(x, c, ctx, c_ctx, ada_w, ada_b, norm1_g, norm2_g, ev_w_in, ev_w_out, conv_w, rw_k_k, rw_k_a, rw_r_k, rw_w0, rw_w_up, rw_a0, rw_a_up, rw_g_up, rw_ln_g, rw_ln_b, od_w_in, od_w_out, nat_qn_g, nat_kn_g, nat_rpb, gla_a_up, gla_a_b, gla_ln_g, moe_router, moe_w1, moe_w3, moe_w2):
    depth = ada_w.shape[0]
    ctx_s = ctx
    silu_c = jax.nn.silu(c)
    silu_cc = jax.nn.silu(c_ctx)
    for l in range(depth):
        last = l == depth - 1
        j = l // 2
        sh1, sc1, gt1, sh2, sc2, gt2 = _split((silu_c @ ada_w[l] + ada_b[l])[:, None, :], [D_MODEL] * 6)
        csh1, csc1, cgt1, csh2, csc2, cgt2 = _split(silu_cc @ ada_w[l] + ada_b[l], [D_MODEL] * 6)
        even = l % 2 == 0
        w_in = ev_w_in[j] if even else od_w_in[j]
        sizes = EVEN_SPLIT[:6] + [sum(EVEN_TAIL)] if even else ODD_SPLIT
        proj_lat = in_proj_split(x, norm1_g[l], sh1, sc1, w_in, sizes)
        proj_ctx = in_proj_split(ctx_s, norm1_g[l], csh1, csc1, w_in, sizes)
        if even:
            cat_lat, cat_ctx = even_mixer(proj_lat, proj_ctx, conv_w[j], rw_k_k[j], rw_k_a[j], rw_r_k[j],
                                          rw_w0[j], rw_w_up[j], rw_a0[j], rw_a_up[j], rw_g_up[j], rw_ln_g[j],
                                          rw_ln_b[j], not last)
            w_out = ev_w_out[j]
        else:
            cat_lat, cat_ctx = odd_mixer(proj_lat, proj_ctx, nat_qn_g[j], nat_kn_g[j], nat_rpb[j],
                                         gla_a_up[j], gla_a_b[j], gla_ln_g[j], not last)
            w_out = od_w_out[j]
        experts = (moe_w1, moe_w3, moe_w2, l)
        x = out_proj_residual(cat_lat, w_out, x, gt1)
        x = x + gt2 * ec_moe(modulate(x, norm2_g[l], sh2, sc2), moe_router[l], *experts)
        if not last:
            ctx_s = out_proj_residual(cat_ctx, w_out, ctx_s, cgt1)
            ctx_s = ctx_s + cgt2 * ec_moe(modulate(ctx_s, norm2_g[l], csh2, csc2), moe_router[l], *experts)
    return x
```

```python
import functools

import jax
import jax.numpy as jnp
import numpy as np
from jax import lax
from jax.experimental import pallas as pl
from jax.experimental.pallas import tpu as pltpu

D_MODEL = 1024
GRID_W = 64
NORM_EPS = 1e-6
F32 = jnp.float32
BF16 = jnp.bfloat16

D_CONV = 512
D_RWKV = 512
RWKV_HEAD = 64
RWKV_HEADS = D_RWKV // RWKV_HEAD
RWKV_DECAY_RANK = 64
RWKV_ICLR_RANK = 64
RWKV_GATE_RANK = 128
RWKV_LN_EPS = 64e-5
D_NAT = 512
NAT_HEAD = 64
NAT_HEADS = D_NAT // NAT_HEAD
NAT_ROWS = 8
NAT_COLS = 16
GLA_HEADS = 4
GLA_DK = 64
GLA_DV = 128
D_GLA_K = GLA_HEADS * GLA_DK
D_GLA_V = GLA_HEADS * GLA_DV
GLA_GATE_RANK = 16
GLA_GATE_TEMP = 16.0
GLA_CHUNK = 64
ROPE_BASE = 10000.0
N_EXPERTS = 16
EC_CAPACITY_FACTOR = 2

EVEN_SPLIT = [D_CONV, D_CONV, D_CONV, D_RWKV, D_RWKV, D_RWKV, RWKV_DECAY_RANK, RWKV_ICLR_RANK, RWKV_GATE_RANK]
ODD_SPLIT = [D_NAT, D_NAT, D_NAT, D_GLA_K, D_GLA_K, D_GLA_V, D_GLA_V, GLA_GATE_RANK]


def _out_proj_body(a_ref, w_ref, x_ref, g_ref, o_ref):
    acc = jnp.dot(a_ref[0].astype(BF16), w_ref[...], preferred_element_type=F32)
    o_ref[0] = x_ref[0] + g_ref[0] * acc


def out_proj_residual(a, w, x, gate, block_rows=512):
    B, L, K = a.shape
    N = w.shape[1]
    tm = min(block_rows, L)
    assert L % tm == 0
    gate = jnp.broadcast_to(gate.reshape(-1, 1, N), (B, 1, N))
    return pl.pallas_call(
        _out_proj_body,
        grid=(B, L // tm),
        in_specs=[
            pl.BlockSpec((1, tm, K), lambda b, i: (b, i, 0)),
            pl.BlockSpec((K, N), lambda b, i: (0, 0)),
            pl.BlockSpec((1, tm, N), lambda b, i: (b, i, 0)),
            pl.BlockSpec((1, 1, N), lambda b, i: (b, 0, 0)),
        ],
        out_specs=pl.BlockSpec((1, tm, N), lambda b, i: (b, i, 0)),
        out_shape=jax.ShapeDtypeStruct((B, L, N), F32),
        compiler_params=pltpu.CompilerParams(dimension_semantics=("parallel", "parallel")),
        name="out_proj_residual",
    )(a, w.astype(BF16), x, gate)


def _in_proj_body(x_ref, g_ref, shift_ref, scale_ref, w_ref, *o_refs, offsets):
    xf = x_ref[0]
    y = xf * lax.rsqrt(jnp.mean(xf * xf, axis=-1, keepdims=True) + NORM_EPS) * g_ref[...]
    h = (y * (1.0 + scale_ref[0]) + shift_ref[0]).astype(BF16)
    for o_ref, off in zip(o_refs, offsets):
        o_ref[0] = jnp.dot(h, w_ref[:, off:off + o_ref.shape[-1]], preferred_element_type=F32)


def in_proj_split(x, g, shift, scale, w, sizes, block_rows=512):
    B, L, D = x.shape
    N = w.shape[1]
    assert sum(sizes) == N
    tm = min(block_rows, L)
    assert L % tm == 0
    offsets = tuple(int(o) for o in np.cumsum([0] + list(sizes[:-1])))
    per_sample = lambda t: jnp.broadcast_to(t.reshape(-1, 1, D), (B, 1, D))
    vec = pl.BlockSpec((1, 1, D), lambda b, i: (b, 0, 0))
    return pl.pallas_call(
        functools.partial(_in_proj_body, offsets=offsets),
        grid=(B, L // tm),
        in_specs=[pl.BlockSpec((1, tm, D), lambda b, i: (b, i, 0)), pl.BlockSpec((1, D), lambda b, i: (0, 0)),
                  vec, vec, pl.BlockSpec((D, N), lambda b, i: (0, 0))],
        out_specs=[pl.BlockSpec((1, tm, s), lambda b, i: (b, i, 0)) for s in sizes],
        out_shape=[jax.ShapeDtypeStruct((B, L, s), F32) for s in sizes],
        compiler_params=pltpu.CompilerParams(dimension_semantics=("parallel", "parallel"),
                                             vmem_limit_bytes=48 * 1024 * 1024),
        name="in_proj_split",
    )(x, g.reshape(1, D), per_sample(shift), per_sample(scale), w.astype(BF16))


RWKV_CHUNK = 64
RWKV_CHUNKS_PER_STEP = 4
PAIR = 2 * RWKV_HEAD


def _mm(a, b):
    return jnp.dot(a.astype(BF16), b.astype(BF16), preferred_element_type=F32)


def _mm_nt(a, b):
    return lax.dot_general(a.astype(BF16), b.astype(BF16), (((1,), (1,)), ((), ())), preferred_element_type=F32)


def _mm_tn(a, b):
    return lax.dot_general(a.astype(BF16), b.astype(BF16), (((0,), (0,)), ((), ())), preferred_element_type=F32)


def _rwkv_chunk_pairs(ins, sts, slots, *, reverse):
    C = RWKV_CHUNK
    row = lax.broadcasted_iota(jnp.int32, (PAIR, PAIR), 0)
    col = lax.broadcasted_iota(jnp.int32, (PAIR, PAIR), 1)
    same_head = (row // RWKV_HEAD) == (col // RWKV_HEAD)
    t_i, s_i = row % C, col % C
    before = (s_i > t_i) if reverse else (s_i < t_i)
    upto = before | (s_i == t_i)

    ct = lax.broadcasted_iota(jnp.int32, (C, C), 0)
    cs = lax.broadcasted_iota(jnp.int32, (C, C), 1)
    tri = ((cs >= ct) if reverse else (cs <= ct)).astype(F32)
    eye = jnp.where(row == col, 1.0, 0.0)

    def expand(x):
        return jnp.where(same_head, jnp.concatenate([x, x], axis=0), 0.0)

    def prepare(r, k, v, kk, lw, a, ka):
        keff = k * (1.0 + (a - 1.0) * ka)
        b = kk * a
        cum_in = jnp.dot(tri, lw, preferred_element_type=F32, precision=lax.Precision.HIGHEST)
        cum_ex = cum_in - lw
        tot = jnp.sum(lw, axis=0, keepdims=True)
        e_neg = jnp.exp(-cum_in)
        e_rem = jnp.exp(tot - cum_in)
        return dict(at2=expand(-kk * jnp.exp(cum_ex)), rt2=expand(r * jnp.exp(cum_in)),
                    bh2=expand(b * e_neg), kh2=expand(keff * e_neg), bp2=expand(b * e_rem),
                    kp2=expand(keff * e_rem), v2=expand(v), dtot=eye * jnp.exp(tot))

    ps = [prepare(*args) for args in ins]
    pps = [_mm_nt(jnp.concatenate([p["at2"], p["rt2"]], axis=0), jnp.concatenate([p["bh2"], p["kh2"]], axis=0))
           for p in ps]
    a_ab = [jnp.where(before, pp[:PAIR, :PAIR], 0.0) for pp in pps]
    a_ak = [jnp.where(before, pp[:PAIR, PAIR:], 0.0) for pp in pps]
    a_rb = [jnp.where(upto, pp[PAIR:, :PAIR], 0.0) for pp in pps]
    a_rk = [jnp.where(upto, pp[PAIR:, PAIR:], 0.0) for pp in pps]

    tinv = [eye + n for n in a_ab]
    npow = a_ab
    for _ in range(int(np.log2(C)) - 1):
        npow = [_mm(n, n) for n in npow]
        tinv = [t + _mm(t, n) for t, n in zip(tinv, npow)]

    av = [_mm(jnp.concatenate([ak, rk], axis=0), p["v2"]) for ak, rk, p in zip(a_ak, a_rk, ps)]
    x = [_mm(t, jnp.concatenate([p["at2"], w[:PAIR]], axis=1)) for t, p, w in zip(tinv, ps, av)]
    z = [_mm(rb, xx) for rb, xx in zip(a_rb, x)]
    bx = [_mm_tn(p["bp2"], xx) for p, xx in zip(ps, x)]
    kv = [_mm_tn(p["kp2"], p["v2"]) for p in ps]
    qe = [p["rt2"] + zz[:, :PAIR] for p, zz in zip(ps, z)]
    yloc = [zz[:, PAIR:] + w[PAIR:] for zz, w in zip(z, av)]
    mt = [p["dtot"] + b[:, :PAIR] for p, b in zip(ps, bx)]
    gt = [b[:, PAIR:] + g for b, g in zip(bx, kv)]
    sts = list(sts)
    ys = []
    for i, s in enumerate(slots):
        y2 = _mm(qe[i], sts[s]) + yloc[i]
        sts[s] = _mm(mt[i], sts[s]) + gt[i]
        ys.append(y2[:C] + y2[C:])
    return ys, sts


def _rwkv_chunk_body(r_ref, k_ref, v_ref, kk_ref, lw_ref, a_ref, ka_ref, s0_ref, y_ref, sT_ref, st_scr, *, reverse):
    c = pl.program_id(1)

    @pl.when(c == 0)
    def _():
        st_scr[...] = s0_ref[0]

    C = RWKV_CHUNK
    npair = st_scr.shape[0]
    nchunk = r_ref.shape[1] // C
    order = range(nchunk - 1, -1, -1) if reverse else range(nchunk)
    chains = [(slice(u * C, (u + 1) * C), slice(p * PAIR, (p + 1) * PAIR), p) for u in order for p in range(npair)]
    ins = [(r_ref[0, rs, ls], k_ref[0, rs, ls], v_ref[0, rs, ls], kk_ref[0, rs, ls], lw_ref[0, rs, ls],
            a_ref[0, rs, ls], ka_ref[:, ls]) for rs, ls, _ in chains]
    ys, sts = _rwkv_chunk_pairs(ins, [st_scr[p] for p in range(npair)], [p for _, _, p in chains], reverse=reverse)
    for (rs, ls, _), y in zip(chains, ys):
        y_ref[0, rs, ls] = y
    for p in range(npair):
        st_scr[p] = sts[p]

    @pl.when(c == pl.num_programs(1) - 1)
    def _():
        sT_ref[0] = st_scr[...]


def rwkv_state_pack(S):
    B, H = S.shape[:2]
    St = jnp.swapaxes(S, -1, -2).reshape(B, H // 2, 2, RWKV_HEAD, RWKV_HEAD)
    z = jnp.zeros_like(St[:, :, 0])
    top = jnp.concatenate([St[:, :, 0], z], axis=-1)
    bot = jnp.concatenate([z, St[:, :, 1]], axis=-1)
    return jnp.concatenate([top, bot], axis=-2)


def rwkv_state_unpack(St):
    B, P = St.shape[:2]
    h0 = St[:, :, :RWKV_HEAD, :RWKV_HEAD]
    h1 = St[:, :, RWKV_HEAD:, RWKV_HEAD:]
    return jnp.swapaxes(jnp.stack([h0, h1], axis=2).reshape(B, 2 * P, RWKV_HEAD, RWKV_HEAD), -1, -2)


def rwkv_chunked(r, k, v, kk, lw, a, k_a, st0, *, reverse, interpret=False):
    B, L, D = r.shape
    T = RWKV_CHUNK * RWKV_CHUNKS_PER_STEP
    assert L % T == 0 and D % PAIR == 0
    n, P = L // T, D // PAIR
    cidx = (lambda c: n - 1 - c) if reverse else (lambda c: c)
    seq = pl.BlockSpec((1, T, D), lambda b, c: (b, cidx(c), 0))
    state = pl.BlockSpec((1, P, PAIR, PAIR), lambda b, c: (b, 0, 0, 0))
    return pl.pallas_call(
        functools.partial(_rwkv_chunk_body, reverse=reverse),
        grid=(B, n),
        in_specs=[seq] * 6 + [pl.BlockSpec((1, D), lambda b, c: (0, 0)), state],
        out_specs=[seq, state],
        out_shape=[jax.ShapeDtypeStruct((B, L, D), F32), jax.ShapeDtypeStruct((B, P, PAIR, PAIR), F32)],
        scratch_shapes=[pltpu.VMEM((P, PAIR, PAIR), F32)],
        compiler_params=pltpu.CompilerParams(dimension_semantics=("parallel", "arbitrary")),
        name="rwkv_chunked_rev" if reverse else "rwkv_chunked_fwd",
        interpret=interpret,
    )(r, k, v, kk, lw, a, k_a.reshape(1, D), st0)


NAT_WIN = NAT_ROWS * GRID_W
NAT_ROWS_PER_BLOCK = 16
NAT_MASKED = -1e30


def _pair_rmsnorm(x, g, lane_lo):
    sq = x * x
    s_lo = jnp.sum(jnp.where(lane_lo, sq, 0.0), axis=-1, keepdims=True)
    s_hi = jnp.sum(sq, axis=-1, keepdims=True) - s_lo
    ms = jnp.where(lane_lo, s_lo, s_hi) * (1.0 / NAT_HEAD)
    return x * lax.rsqrt(ms + NORM_EPS) * g


def _nat_body(q_ref, k_ref, v_ref, kc_ref, vc_ref, bias_ref, qg_ref, kg_ref, o_ref, kn_scr, vb_scr, kcn_scr, vcb_scr,
              *, rows):
    rb = pl.program_id(2)
    L = k_ref.shape[1]
    norm_rows = 512
    lane_lo_n = lax.broadcasted_iota(jnp.int32, (norm_rows, PAIR), 1) < NAT_HEAD

    @pl.when(rb == 0)
    def _():
        def norm_block(i, carry):
            sl = pl.ds(pl.multiple_of(i * norm_rows, norm_rows), norm_rows)
            kn_scr[sl, :] = _pair_rmsnorm(k_ref[0, sl, :], kg_ref[...], lane_lo_n).astype(BF16)
            vb_scr[sl, :] = v_ref[0, sl, :].astype(BF16)
            return carry
        lax.fori_loop(0, L // norm_rows, norm_block, 0)
        lane_lo_c = lax.broadcasted_iota(jnp.int32, kc_ref.shape[1:], 1) < NAT_HEAD
        kcn_scr[...] = _pair_rmsnorm(kc_ref[0], kg_ref[...], lane_lo_c).astype(BF16)
        vcb_scr[...] = vc_ref[0].astype(BF16)

    lane_lo = lax.broadcasted_iota(jnp.int32, (GRID_W, PAIR), 1) < NAT_HEAD
    scale = NAT_HEAD ** -0.5

    rows_per_iter = 4
    nt = (((1,), (1,)), ((), ()))

    def row_group(it, carry):
        qsl, kwin, vwin, qh, bias = [], [], [], [], []
        for u in range(rows_per_iter):
            j = it * rows_per_iter + u
            r = rb * NAT_ROWS_PER_BLOCK + j
            rs = jnp.clip(r - NAT_ROWS // 2, 0, rows - NAT_ROWS)
            sl = pl.ds(pl.multiple_of(j * GRID_W, GRID_W), GRID_W)
            wsl = pl.ds(pl.multiple_of(rs * GRID_W, GRID_W), NAT_WIN)
            qn = _pair_rmsnorm(q_ref[0, sl, :], qg_ref[...], lane_lo) * scale
            for h in range(2):
                qsl.append(sl)
                kwin.append(kn_scr[wsl, :])
                vwin.append(vb_scr[wsl, :])
                qh.append(jnp.where(lane_lo if h == 0 else ~lane_lo, qn, 0.0).astype(BF16))
                bias.append(bias_ref[h, r - rs])
        chains = range(2 * rows_per_iter)
        s_loc = [lax.dot_general(qh[i], kwin[i], nt, preferred_element_type=F32) + bias[i] for i in chains]
        s_ctx = [lax.dot_general(qh[i], kcn_scr[...], nt, preferred_element_type=F32) for i in chains]
        m = [jnp.maximum(jnp.max(s_loc[i], axis=-1, keepdims=True), jnp.max(s_ctx[i], axis=-1, keepdims=True))
             for i in chains]
        p_loc = [jnp.exp(s_loc[i] - m[i]) for i in chains]
        p_ctx = [jnp.exp(s_ctx[i] - m[i]) for i in chains]
        den = [jnp.sum(p_loc[i], axis=-1, keepdims=True) + jnp.sum(p_ctx[i], axis=-1, keepdims=True) for i in chains]
        o = [(jnp.dot(p_loc[i].astype(BF16), vwin[i], preferred_element_type=F32)
              + jnp.dot(p_ctx[i].astype(BF16), vcb_scr[...], preferred_element_type=F32)) / den[i] for i in chains]
        for u in range(rows_per_iter):
            o_ref[0, qsl[2 * u], :] = jnp.where(lane_lo, o[2 * u], o[2 * u + 1])
        return carry

    lax.fori_loop(0, NAT_ROWS_PER_BLOCK // rows_per_iter, row_group, 0)


def _nat_bias_table(rpb):
    col = jnp.arange(GRID_W)
    cstart = jnp.clip(col - NAT_COLS // 2, 0, GRID_W - NAT_COLS)
    delta = jnp.arange(NAT_ROWS)
    wrow = jnp.arange(NAT_ROWS)
    ridx = wrow[None, :] - delta[:, None] + (NAT_ROWS - 1)
    cidx = col[None, :] - col[:, None] + (NAT_COLS - 1)
    inwin = (col[None, :] >= cstart[:, None]) & (col[None, :] < cstart[:, None] + NAT_COLS)
    rsel = (ridx[:, :, None] == jnp.arange(2 * NAT_ROWS - 1)).astype(F32)
    csel = ((cidx[:, :, None] == jnp.arange(2 * NAT_COLS - 1)) & inwin[:, :, None]).astype(F32)
    tab = jnp.einsum('hab,dia,ckb->hdcik', rpb.astype(F32), rsel, csel, precision=lax.Precision.HIGHEST)
    tab = jnp.where(inwin[None, None, :, None, :], tab, NAT_MASKED)
    return tab.reshape(rpb.shape[0], NAT_ROWS, GRID_W, NAT_WIN)


def nat_attention(q, k, v, kc, vc, rpb, qn_g, kn_g, interpret=False):
    B, L, D = q.shape
    Lc = kc.shape[1]
    rows = L // GRID_W
    P = D // PAIR
    rpb_blk = NAT_ROWS_PER_BLOCK
    assert rows >= NAT_ROWS and rows % rpb_blk == 0 and L % 512 == 0
    bias = _nat_bias_table(rpb)
    g2 = lambda g: jnp.tile(g, 2).reshape(1, PAIR)
    full = pl.BlockSpec((1, L, PAIR), lambda b, p, i: (b, 0, p))
    cfull = pl.BlockSpec((1, Lc, PAIR), lambda b, p, i: (b, 0, p))
    qblk = pl.BlockSpec((1, rpb_blk * GRID_W, PAIR), lambda b, p, i: (b, i, p))
    gspec = pl.BlockSpec((1, PAIR), lambda b, p, i: (0, 0))
    return pl.pallas_call(
        functools.partial(_nat_body, rows=rows),
        grid=(B, P, rows // rpb_blk),
        in_specs=[qblk, full, full, cfull, cfull,
                  pl.BlockSpec((2, NAT_ROWS, GRID_W, NAT_WIN), lambda b, p, i: (p, 0, 0, 0)), gspec, gspec],
        out_specs=qblk,
        out_shape=jax.ShapeDtypeStruct((B, L, D), F32),
        scratch_shapes=[pltpu.VMEM((L, PAIR), BF16), pltpu.VMEM((L, PAIR), BF16),
                        pltpu.VMEM((Lc, PAIR), BF16), pltpu.VMEM((Lc, PAIR), BF16)],
        compiler_params=pltpu.CompilerParams(dimension_semantics=("parallel", "parallel", "arbitrary"),
                                             vmem_limit_bytes=48 * 1024 * 1024),
        name="nat_attention",
        interpret=interpret,
    )(q, k, v, kc, vc, bias, g2(qn_g), g2(kn_g))


GLA_SUB = 16
GLA_BLOCK = 256
GLA_KPAIR = 2 * GLA_DK
GLA_VPAIR = 2 * GLA_DV
GLA_UNROLL = 4


def _gla_body(q_ref, k_ref, v_ref, g_ref, s0_ref, o_ref, sT_ref, st_scr, *, reverse):
    C = GLA_SUB
    nsub = GLA_BLOCK // C
    npair = st_scr.shape[0]
    blk = pl.program_id(1)

    @pl.when(blk == 0)
    def _():
        st_scr[...] = s0_ref[0]

    ti = lax.broadcasted_iota(jnp.int32, (C, C), 0)
    si = lax.broadcasted_iota(jnp.int32, (C, C), 1)
    tri = ((si >= ti) if reverse else (si <= ti)).astype(F32)
    lane_lo = lax.broadcasted_iota(jnp.int32, (C, GLA_KPAIR), 1) < GLA_DK
    row_id = lax.broadcasted_iota(jnp.int32, (C, GLA_KPAIR), 0)
    vrow = lax.broadcasted_iota(jnp.int32, (2 * C, GLA_VPAIR), 0)
    vcol = lax.broadcasted_iota(jnp.int32, (2 * C, GLA_VPAIR), 1)
    v_same_head = (vrow // C) == (vcol // GLA_DV)
    srow = lax.broadcasted_iota(jnp.int32, (GLA_VPAIR, GLA_KPAIR), 0)
    scol = lax.broadcasted_iota(jnp.int32, (GLA_VPAIR, GLA_KPAIR), 1)
    s_same_head = (srow // GLA_DV) == (scol // GLA_DK)

    pairs = range(npair)
    klanes = [slice(p * GLA_KPAIR, (p + 1) * GLA_KPAIR) for p in pairs]
    vlanes = [slice(p * GLA_VPAIR, (p + 1) * GLA_VPAIR) for p in pairs]

    def sub_chunks(it, carry):
        chains = [(u, p) for u in range(GLA_UNROLL) for p in pairs]
        sl, q, k, g, v = {}, {}, {}, {}, {}
        for u in range(GLA_UNROLL):
            i = it * GLA_UNROLL + u
            ci = (nsub - 1 - i) if reverse else i
            sl[u] = pl.ds(pl.multiple_of(ci * C, C), C)
            for p in pairs:
                q[u, p], k[u, p], g[u, p] = (r[0, sl[u], klanes[p]] for r in (q_ref, k_ref, g_ref))
                v[u, p] = v_ref[0, sl[u], vlanes[p]]
        b = {c: jnp.dot(tri, g[c], preferred_element_type=F32, precision=lax.Precision.HIGHEST) for c in chains}
        b_end = {c: jnp.sum(g[c], axis=0, keepdims=True) for c in chains}
        a_lo = {c: jnp.zeros((C, C), F32) for c in chains}
        a_hi = {c: jnp.zeros((C, C), F32) for c in chains}
        for j in range(C):
            seen = (row_id <= j) if reverse else (row_id >= j)
            for c in chains:
                decay = jnp.exp(jnp.where(seen, b[c] - b[c][j:j + 1, :], 0.0))
                f = jnp.where(seen, q[c] * k[c][j:j + 1, :] * decay, 0.0)
                r_lo = jnp.sum(jnp.where(lane_lo, f, 0.0), axis=-1, keepdims=True)
                r_hi = jnp.sum(jnp.where(lane_lo, 0.0, f), axis=-1, keepdims=True)
                a_lo[c] = jnp.where(si == j, r_lo, a_lo[c])
                a_hi[c] = jnp.where(si == j, r_hi, a_hi[c])
        v_bd = {c: jnp.where(v_same_head, jnp.concatenate([v[c], v[c]], axis=0), 0.0) for c in chains}
        o_in = {c: _mm(jnp.concatenate([a_lo[c], a_hi[c]], axis=1), v_bd[c]) for c in chains}
        kv = {c: jnp.where(s_same_head, _mm_tn(v[c], k[c] * jnp.exp(b_end[c] - b[c])), 0.0) for c in chains}
        qd = {c: q[c] * jnp.exp(b[c]) for c in chains}
        e_end = {c: jnp.exp(b_end[c]) for c in chains}
        st = [st_scr[p] for p in pairs]
        for u in range(GLA_UNROLL):
            for p in pairs:
                o_ref[0, sl[u], vlanes[p]] = o_in[u, p] + _mm_nt(qd[u, p], st[p])
                st[p] = st[p] * e_end[u, p] + kv[u, p]
        for p in pairs:
            st_scr[p] = st[p]
        return carry

    lax.fori_loop(0, nsub // GLA_UNROLL, sub_chunks, 0)

    @pl.when(blk == pl.num_programs(1) - 1)
    def _():
        sT_ref[0] = st_scr[...]


def gla_scan(q, k, v, logg, st0, *, reverse, interpret=False):
    B, L, Dk = q.shape
    Dv = v.shape[-1]
    T = GLA_BLOCK
    assert L % T == 0 and Dk % GLA_KPAIR == 0
    n, P = L // T, Dk // GLA_KPAIR
    bidx = (lambda c: n - 1 - c) if reverse else (lambda c: c)
    kspec = pl.BlockSpec((1, T, Dk), lambda b, c: (b, bidx(c), 0))
    vspec = pl.BlockSpec((1, T, Dv), lambda b, c: (b, bidx(c), 0))
    sspec = pl.BlockSpec((1, P, GLA_VPAIR, GLA_KPAIR), lambda b, c: (b, 0, 0, 0))
    return pl.pallas_call(
        functools.partial(_gla_body, reverse=reverse),
        grid=(B, n),
        in_specs=[kspec, kspec, vspec, kspec, sspec],
        out_specs=[vspec, sspec],
        out_shape=[jax.ShapeDtypeStruct((B, L, Dv), F32),
                   jax.ShapeDtypeStruct((B, P, GLA_VPAIR, GLA_KPAIR), F32)],
        scratch_shapes=[pltpu.VMEM((P, GLA_VPAIR, GLA_KPAIR), F32)],
        compiler_params=pltpu.CompilerParams(dimension_semantics=("parallel", "arbitrary")),
        name="gla_scan_rev" if reverse else "gla_scan_fwd",
        interpret=interpret,
    )(q, k, v, logg, st0)


MOE_F_TILE = 512


def _expert_ffn_body(x_ref, g_ref, w1_ref, w3_ref, w2_ref, o_ref, acc_ref):
    f = pl.program_id(3)

    @pl.when(f == 0)
    def _():
        acc_ref[...] = jnp.zeros_like(acc_ref)

    x = x_ref[0, 0].astype(BF16)
    h1 = jnp.dot(x, w1_ref[0].astype(BF16), preferred_element_type=F32)
    h3 = jnp.dot(x, w3_ref[0].astype(BF16), preferred_element_type=F32)
    hid = (h1 * jax.nn.sigmoid(h1) * h3).astype(BF16)
    acc_ref[...] += jnp.dot(hid, w2_ref[0].astype(BF16), preferred_element_type=F32)

    @pl.when(f == pl.num_programs(3) - 1)
    def _():
        o_ref[0, 0] = acc_ref[...] * g_ref[0, 0]


def expert_ffn(xin, gate, w1, w3, w2, layer, interpret=False):
    B, E, cap, D = xin.shape
    F = w1.shape[-1]
    tm = min(cap, 1024)
    tf = F if tm <= 256 else MOE_F_TILE
    assert cap % tm == 0 and F % tf == 0
    return pl.pallas_call(
        _expert_ffn_body,
        grid=(E, B, cap // tm, F // tf),
        in_specs=[
            pl.BlockSpec((1, 1, tm, D), lambda e, b, i, f: (b, e, i, 0)),
            pl.BlockSpec((1, 1, tm, 1), lambda e, b, i, f: (b, e, i, 0)),
            pl.BlockSpec((None, 1, D, tf), lambda e, b, i, f: (layer, e, 0, f)),
            pl.BlockSpec((None, 1, D, tf), lambda e, b, i, f: (layer, e, 0, f)),
            pl.BlockSpec((None, 1, tf, D), lambda e, b, i, f: (layer, e, f, 0)),
        ],
        out_specs=pl.BlockSpec((1, 1, tm, D), lambda e, b, i, f: (b, e, i, 0)),
        out_shape=jax.ShapeDtypeStruct((B, E, cap, D), F32),
        scratch_shapes=[pltpu.VMEM((tm, D), F32)],
        compiler_params=pltpu.CompilerParams(
            dimension_semantics=("parallel", "parallel", "parallel", "arbitrary"),
            vmem_limit_bytes=56 * 1024 * 1024),
        name="expert_ffn",
        interpret=interpret,
    )(xin, gate[..., None], w1, w3, w2)


def _split(p, sizes):
    return jnp.split(p, np.cumsum(sizes)[:-1].tolist(), axis=-1)


def rmsnorm(x, g, eps=NORM_EPS):
    xf = x.astype(F32)
    y = xf * lax.rsqrt(jnp.mean(xf * xf, axis=-1, keepdims=True) + eps)
    return (y * g.astype(F32)).astype(x.dtype)


def modulate(x, g, shift, scale):
    return rmsnorm(x, g) * (1 + scale) + shift


def conv3(u, w):
    up = jnp.pad(u, ((0, 0), (1, 1), (0, 0)))
    return up[:, :-2] * w[0] + up[:, 1:-1] * w[1] + up[:, 2:] * w[2]


def rope_2d(x):
    L, d = x.shape[1], x.shape[-1]
    half, nf = d // 2, d // 4
    t = jnp.arange(L)
    inv = ROPE_BASE ** (-jnp.arange(nf, dtype=F32) / nf)

    def rot(u, pos):
        ang = pos.astype(F32)[:, None] * inv[None, :]
        cos, sin = jnp.cos(ang)[None, :, None, :], jnp.sin(ang)[None, :, None, :]
        u1, u2 = u[..., :nf].astype(F32), u[..., nf:].astype(F32)
        return jnp.concatenate([u1 * cos - u2 * sin, u1 * sin + u2 * cos], axis=-1)

    return jnp.concatenate([rot(x[..., :half], t // GRID_W), rot(x[..., half:], t % GRID_W)], axis=-1).astype(x.dtype)


def ec_moe(h, router, w1, w3, w2, layer):
    B, T, D = h.shape
    cap = EC_CAPACITY_FACTOR * T // N_EXPERTS
    aff = jax.nn.softmax((h @ router).astype(F32), axis=-1)
    gate, idx = lax.top_k(jnp.swapaxes(aff, 1, 2), cap)
    xin = jax.vmap(lambda hb, ib: hb[ib])(h, idx)
    if B * cap <= 1024:
        merge = lambda t: jnp.swapaxes(t, 0, 1).reshape((1, N_EXPERTS, B * cap) + t.shape[3:])
        y = expert_ffn(merge(xin), merge(gate), w1, w3, w2, layer)
        y = jnp.swapaxes(y.reshape(N_EXPERTS, B, cap, D), 0, 1)
    else:
        y = expert_ffn(xin, gate, w1, w3, w2, layer)
    out = jax.vmap(lambda ib, yb: jnp.zeros((T, D), yb.dtype).at[ib.reshape(-1)].add(yb.reshape(-1, D)))(idx, y)
    return out.astype(h.dtype)


def rwkv_readout(y, r, k, v, xg, r_k, g_up, ln_g, ln_b):
    B, L = r.shape[:2]
    hs = lambda t: t.reshape(B, L, RWKV_HEADS, RWKV_HEAD)
    yf = hs(y.astype(F32))
    mu = jnp.mean(yf, axis=-1, keepdims=True)
    var = jnp.mean(jnp.square(yf - mu), axis=-1, keepdims=True)
    yn = ((yf - mu) * lax.rsqrt(var + RWKV_LN_EPS)).reshape(B, L, D_RWKV) * ln_g + ln_b
    bonus = (jnp.sum(hs(r) * hs(k) * r_k, axis=-1, keepdims=True) * hs(v)).reshape(B, L, D_RWKV)
    g = jax.nn.sigmoid(xg) @ g_up
    return ((yn + bonus) * g).astype(r.dtype)


EVEN_TAIL = EVEN_SPLIT[6:]


def even_mixer(proj_lat, proj_ctx, conv_w, k_k, k_a, r_k, w0, w_up, a0, a_up, g_up, ln_g, ln_b, need_ctx):
    B = proj_lat[0].shape[0]
    p_lat = list(proj_lat[:6]) + _split(proj_lat[6], EVEN_TAIL)
    p_ctx = list(proj_ctx[:6]) + _split(proj_ctx[6], EVEN_TAIL)

    def conv_branch(u, gate_b, gate_c):
        return gate_b * conv3(gate_c * u, conv_w)

    def unit_key(k):
        Bq, L = k.shape[:2]
        kkf = (k * k_k).astype(F32).reshape(Bq, L, RWKV_HEADS, RWKV_HEAD)
        return (kkf * lax.rsqrt(jnp.sum(kkf * kkf, axis=-1, keepdims=True) + 1e-12)).reshape(Bq, L, D_RWKV)

    def dir_gates(xw, xa, d):
        w_raw = (w0[d] + jnp.tanh(xw) @ w_up[d]).astype(F32)
        return -jnp.exp(-jax.nn.softplus(-w_raw) - 0.5), jax.nn.sigmoid(a0[d] + xa @ a_up[d])

    kk_ctx, kk_lat = unit_key(p_ctx[4]), unit_key(p_lat[4])
    st0 = jnp.zeros((B, RWKV_HEADS // 2, PAIR, PAIR), F32)
    y_lat, y_ctx = 0.0, 0.0
    for d, rev in ((0, False), (1, True)):
        yc, st_c = rwkv_chunked(*p_ctx[3:6], kk_ctx, *dir_gates(*p_ctx[6:8], d), k_a, st0, reverse=rev)
        yl, _ = rwkv_chunked(*p_lat[3:6], kk_lat, *dir_gates(*p_lat[6:8], d), k_a, st_c, reverse=rev)
        y_lat = y_lat + yl
        if need_ctx:
            y_ctx = y_ctx + yc
    cat_lat = jnp.concatenate([conv_branch(*p_lat[0:3]),
                               rwkv_readout(y_lat, *p_lat[3:6], p_lat[8], r_k, g_up, ln_g, ln_b)], axis=-1)
    cat_ctx = None
    if need_ctx:
        cat_ctx = jnp.concatenate([conv_branch(*p_ctx[0:3]),
                                   rwkv_readout(y_ctx, *p_ctx[3:6], p_ctx[8], r_k, g_up, ln_g, ln_b)], axis=-1)
    return cat_lat, cat_ctx


def ctx_attention(q, k, v):
    s = jnp.einsum('bhqd,bhkd->bhqk', q, k).astype(F32) * (q.shape[-1] ** -0.5)
    p = jax.nn.softmax(s, axis=-1).astype(v.dtype)
    return jnp.einsum('bhqk,bhkd->bhqd', p, v)


def gla_log_gate(ga, a_up_d, a_b_d):
    B, L = ga.shape[:2]
    lg = jax.nn.log_sigmoid((ga @ a_up_d + a_b_d).astype(F32)) / GLA_GATE_TEMP
    return lg.reshape(B, L, GLA_HEADS, GLA_DK)


def gla_readout(o, gr, ln_g):
    B, L = gr.shape[:2]
    return (rmsnorm(o, ln_g).reshape(B, L, D_GLA_V) * jax.nn.silu(gr)).astype(gr.dtype)


def odd_mixer(proj_lat, proj_ctx, qn_g, kn_g, rpb, a_up, a_b, gla_ln_g, need_ctx):
    nq, nk, nv, gq, gk, gv, gr, ga = proj_lat
    cnq, cnk, cnv, cgq, cgk, cgv, cgr, cga = proj_ctx
    B, L = nq.shape[:2]
    Lc = cnq.shape[1]

    def nat_heads(t, g=None):
        t = t.reshape(t.shape[0], t.shape[1], NAT_HEADS, NAT_HEAD)
        if g is not None:
            t = rmsnorm(t, g)
        return jnp.swapaxes(t, 1, 2)

    nat_lat = nat_attention(nq, nk, nv, cnk, cnv, rpb, qn_g, kn_g)

    gh = lambda t, d: t.reshape(t.shape[0], t.shape[1], GLA_HEADS, d)
    qscale = GLA_DK ** -0.5
    q = (rope_2d(gh(gq, GLA_DK)) * qscale).reshape(B, L, D_GLA_K)
    k = rope_2d(gh(gk, GLA_DK)).reshape(B, L, D_GLA_K)
    qc = cgq * qscale
    st0 = jnp.zeros((B, GLA_HEADS // 2, GLA_VPAIR, GLA_KPAIR), F32)
    o_lat, o_ctx = 0.0, 0.0
    for d in range(2):
        lg_c = gla_log_gate(cga, a_up[d], a_b[d]).reshape(B, Lc, D_GLA_K)
        lg_l = gla_log_gate(ga, a_up[d], a_b[d]).reshape(B, L, D_GLA_K)
        oc, st_c = gla_scan(qc, cgk, cgv, lg_c, st0, reverse=(d == 1))
        ol, _ = gla_scan(q, k, gv, lg_l, st_c, reverse=(d == 1))
        o_lat = o_lat + ol
        if need_ctx:
            o_ctx = o_ctx + oc
    o_lat = gh(o_lat, GLA_DV)
    if need_ctx:
        o_ctx = gh(o_ctx, GLA_DV)
    cat_lat = jnp.concatenate([nat_lat, gla_readout(o_lat, gr, gla_ln_g)], axis=-1)
    cat_ctx = None
    if need_ctx:
        kc, vc = nat_heads(cnk, kn_g), nat_heads(cnv)
        nat_ctx = jnp.swapaxes(ctx_attention(nat_heads(cnq, qn_g), kc, vc), 1, 2).reshape(B, Lc, D_NAT)
        cat_ctx = jnp.concatenate([nat_ctx, gla_readout(o_ctx, cgr, gla_ln_g)], axis=-1)
    return cat_lat, cat_ctx


def kernel(x, c, ctx, c_ctx, ada_w, ada_b, norm1_g, norm2_g, ev_w_in, ev_w_out, conv_w, rw_k_k, rw_k_a, rw_r_k, rw_w0, rw_w_up, rw_a0, rw_a_up, rw_g_up, rw_ln_g, rw_ln_b, od_w_in, od_w_out, nat_qn_g, nat_kn_g, nat_rpb, gla_a_up, gla_a_b, gla_ln_g, moe_router, moe_w1, moe_w3, moe_w2):
    depth = ada_w.shape[0]
    ctx_s = ctx
    silu_c = jax.nn.silu(c)
    silu_cc = jax.nn.silu(c_ctx)
    for l in range(depth):
        last = l == depth - 1
        j = l // 2
        sh1, sc1, gt1, sh2, sc2, gt2 = _split((silu_c @ ada_w[l] + ada_b[l])[:, None, :], [D_MODEL] * 6)
        csh1, csc1, cgt1, csh2, csc2, cgt2 = _split(silu_cc @ ada_w[l] + ada_b[l], [D_MODEL] * 6)
        even = l % 2 == 0
        w_in = ev_w_in[j] if even else od_w_in[j]
        sizes = EVEN_SPLIT[:6] + [sum(EVEN_TAIL)] if even else ODD_SPLIT
        proj_lat = in_proj_split(x, norm1_g[l], sh1, sc1, w_in, sizes)
        proj_ctx = in_proj_split(ctx_s, norm1_g[l], csh1, csc1, w_in, sizes)
        if even:
            cat_lat, cat_ctx = even_mixer(proj_lat, proj_ctx, conv_w[j], rw_k_k[j], rw_k_a[j], rw_r_k[j],
                                          rw_w0[j], rw_w_up[j], rw_a0[j], rw_a_up[j], rw_g_up[j], rw_ln_g[j],
                                          rw_ln_b[j], not last)
            w_out = ev_w_out[j]
        else:
            cat_lat, cat_ctx = odd_mixer(proj_lat, proj_ctx, nat_qn_g[j], nat_kn_g[j], nat_rpb[j],
                                         gla_a_up[j], gla_a_b[j], gla_ln_g[j], not last)
            w_out = od_w_out[j]
        experts = (moe_w1, moe_w3, moe_w2, l)
        x = out_proj_residual(cat_lat, w_out, x, gt1)
        x = x + gt2 * ec_moe(modulate(x, norm2_g[l], sh2, sc2), moe_router[l], *experts)
        if not last:
            ctx_s = out_proj_residual(cat_ctx, w_out, ctx_s, cgt1)
            ctx_s = ctx_s + cgt2 * ec_moe(modulate(ctx_s, norm2_g[l], csh2, csc2), moe_router[l], *experts)
    return x
```

```python
import functools

import jax
import jax.numpy as jnp
import numpy as np
from jax import lax
from jax.experimental import pallas as pl
from jax.experimental.pallas import tpu as pltpu

D_MODEL = 1024
GRID_W = 64
NORM_EPS = 1e-6
F32 = jnp.float32
BF16 = jnp.bfloat16

D_CONV = 512
D_RWKV = 512
RWKV_HEAD = 64
RWKV_HEADS = D_RWKV // RWKV_HEAD
RWKV_DECAY_RANK = 64
RWKV_ICLR_RANK = 64
RWKV_GATE_RANK = 128
RWKV_LN_EPS = 64e-5
D_NAT = 512
NAT_HEAD = 64
NAT_HEADS = D_NAT // NAT_HEAD
NAT_ROWS = 8
NAT_COLS = 16
GLA_HEADS = 4
GLA_DK = 64
GLA_DV = 128
D_GLA_K = GLA_HEADS * GLA_DK
D_GLA_V = GLA_HEADS * GLA_DV
GLA_GATE_RANK = 16
GLA_GATE_TEMP = 16.0
GLA_CHUNK = 64
ROPE_BASE = 10000.0
N_EXPERTS = 16
EC_CAPACITY_FACTOR = 2

EVEN_SPLIT = [D_CONV, D_CONV, D_CONV, D_RWKV, D_RWKV, D_RWKV, RWKV_DECAY_RANK, RWKV_ICLR_RANK, RWKV_GATE_RANK]
ODD_SPLIT = [D_NAT, D_NAT, D_NAT, D_GLA_K, D_GLA_K, D_GLA_V, D_GLA_V, GLA_GATE_RANK]


def _out_proj_body(a_ref, w_ref, x_ref, g_ref, o_ref):
    acc = jnp.dot(a_ref[0].astype(BF16), w_ref[...], preferred_element_type=F32)
    o_ref[0] = x_ref[0] + g_ref[0] * acc


def out_proj_residual(a, w, x, gate, block_rows=512):
    B, L, K = a.shape
    N = w.shape[1]
    tm = min(block_rows, L)
    assert L % tm == 0
    gate = jnp.broadcast_to(gate.reshape(-1, 1, N), (B, 1, N))
    return pl.pallas_call(
        _out_proj_body,
        grid=(B, L // tm),
        in_specs=[
            pl.BlockSpec((1, tm, K), lambda b, i: (b, i, 0)),
            pl.BlockSpec((K, N), lambda b, i: (0, 0)),
            pl.BlockSpec((1, tm, N), lambda b, i: (b, i, 0)),
            pl.BlockSpec((1, 1, N), lambda b, i: (b, 0, 0)),
        ],
        out_specs=pl.BlockSpec((1, tm, N), lambda b, i: (b, i, 0)),
        out_shape=jax.ShapeDtypeStruct((B, L, N), F32),
        compiler_params=pltpu.CompilerParams(dimension_semantics=("parallel", "parallel")),
        name="out_proj_residual",
    )(a, w.astype(BF16), x, gate)


def _in_proj_body(x_ref, g_ref, shift_ref, scale_ref, w_ref, *o_refs, offsets):
    xf = x_ref[0]
    y = xf * lax.rsqrt(jnp.mean(xf * xf, axis=-1, keepdims=True) + NORM_EPS) * g_ref[...]
    h = (y * (1.0 + scale_ref[0]) + shift_ref[0]).astype(BF16)
    for o_ref, off in zip(o_refs, offsets):
        o_ref[0] = jnp.dot(h, w_ref[:, off:off + o_ref.shape[-1]], preferred_element_type=F32)


def in_proj_split(x, g, shift, scale, w, sizes, block_rows=512):
    B, L, D = x.shape
    N = w.shape[1]
    assert sum(sizes) == N
    tm = min(block_rows, L)
    assert L % tm == 0
    offsets = tuple(int(o) for o in np.cumsum([0] + list(sizes[:-1])))
    per_sample = lambda t: jnp.broadcast_to(t.reshape(-1, 1, D), (B, 1, D))
    vec = pl.BlockSpec((1, 1, D), lambda b, i: (b, 0, 0))
    return pl.pallas_call(
        functools.partial(_in_proj_body, offsets=offsets),
        grid=(B, L // tm),
        in_specs=[pl.BlockSpec((1, tm, D), lambda b, i: (b, i, 0)), pl.BlockSpec((1, D), lambda b, i: (0, 0)),
                  vec, vec, pl.BlockSpec((D, N), lambda b, i: (0, 0))],
        out_specs=[pl.BlockSpec((1, tm, s), lambda b, i: (b, i, 0)) for s in sizes],
        out_shape=[jax.ShapeDtypeStruct((B, L, s), F32) for s in sizes],
        compiler_params=pltpu.CompilerParams(dimension_semantics=("parallel", "parallel"),
                                             vmem_limit_bytes=48 * 1024 * 1024),
        name="in_proj_split",
    )(x, g.reshape(1, D), per_sample(shift), per_sample(scale), w.astype(BF16))


RWKV_CHUNK = 64
RWKV_CHUNKS_PER_STEP = 4
PAIR = 2 * RWKV_HEAD


def _mm(a, b):
    return jnp.dot(a.astype(BF16), b.astype(BF16), preferred_element_type=F32)


def _mm_nt(a, b):
    return lax.dot_general(a.astype(BF16), b.astype(BF16), (((1,), (1,)), ((), ())), preferred_element_type=F32)


def _mm_tn(a, b):
    return lax.dot_general(a.astype(BF16), b.astype(BF16), (((0,), (0,)), ((), ())), preferred_element_type=F32)


def _rwkv_chunk_pairs(ins, sts, slots, *, reverse):
    C = RWKV_CHUNK
    row = lax.broadcasted_iota(jnp.int32, (PAIR, PAIR), 0)
    col = lax.broadcasted_iota(jnp.int32, (PAIR, PAIR), 1)
    same_head = (row // RWKV_HEAD) == (col // RWKV_HEAD)
    t_i, s_i = row % C, col % C
    before = (s_i > t_i) if reverse else (s_i < t_i)
    upto = before | (s_i == t_i)

    ct = lax.broadcasted_iota(jnp.int32, (C, C), 0)
    cs = lax.broadcasted_iota(jnp.int32, (C, C), 1)
    tri = ((cs >= ct) if reverse else (cs <= ct)).astype(F32)
    eye = jnp.where(row == col, 1.0, 0.0)

    def expand(x):
        return jnp.where(same_head, jnp.concatenate([x, x], axis=0), 0.0)

    def prepare(r, k, v, kk, lw, a, ka):
        keff = k * (1.0 + (a - 1.0) * ka)
        b = kk * a
        cum_in = jnp.dot(tri, lw, preferred_element_type=F32, precision=lax.Precision.HIGHEST)
        cum_ex = cum_in - lw
        tot = jnp.sum(lw, axis=0, keepdims=True)
        e_neg = jnp.exp(-cum_in)
        e_rem = jnp.exp(tot - cum_in)
        return dict(at2=expand(-kk * jnp.exp(cum_ex)), rt2=expand(r * jnp.exp(cum_in)),
                    bh2=expand(b * e_neg), kh2=expand(keff * e_neg), bp2=expand(b * e_rem),
                    kp2=expand(keff * e_rem), v2=expand(v), dtot=eye * jnp.exp(tot))

    ps = [prepare(*args) for args in ins]
    pps = [_mm_nt(jnp.concatenate([p["at2"], p["rt2"]], axis=0), jnp.concatenate([p["bh2"], p["kh2"]], axis=0))
           for p in ps]
    a_ab = [jnp.where(before, pp[:PAIR, :PAIR], 0.0) for pp in pps]
    a_ak = [jnp.where(before, pp[:PAIR, PAIR:], 0.0) for pp in pps]
    a_rb = [jnp.where(upto, pp[PAIR:, :PAIR], 0.0) for pp in pps]
    a_rk = [jnp.where(upto, pp[PAIR:, PAIR:], 0.0) for pp in pps]

    tinv = [eye + n for n in a_ab]
    npow = a_ab
    for _ in range(int(np.log2(C)) - 1):
        npow = [_mm(n, n) for n in npow]
        tinv = [t + _mm(t, n) for t, n in zip(tinv, npow)]

    av = [_mm(jnp.concatenate([ak, rk], axis=0), p["v2"]) for ak, rk, p in zip(a_ak, a_rk, ps)]
    x = [_mm(t, jnp.concatenate([p["at2"], w[:PAIR]], axis=1)) for t, p, w in zip(tinv, ps, av)]
    z = [_mm(rb, xx) for rb, xx in zip(a_rb, x)]
    bx = [_mm_tn(p["bp2"], xx) for p, xx in zip(ps, x)]
    kv = [_mm_tn(p["kp2"], p["v2"]) for p in ps]
    qe = [p["rt2"] + zz[:, :PAIR] for p, zz in zip(ps, z)]
    yloc = [zz[:, PAIR:] + w[PAIR:] for zz, w in zip(z, av)]
    mt = [p["dtot"] + b[:, :PAIR] for p, b in zip(ps, bx)]
    gt = [b[:, PAIR:] + g for b, g in zip(bx, kv)]
    sts = list(sts)
    ys = []
    for i, s in enumerate(slots):
        y2 = _mm(qe[i], sts[s]) + yloc[i]
        sts[s] = _mm(mt[i], sts[s]) + gt[i]
        ys.append(y2[:C] + y2[C:])
    return ys, sts


def _rwkv_chunk_body(r_ref, k_ref, v_ref, kk_ref, lw_ref, a_ref, ka_ref, s0_ref, y_ref, sT_ref, st_scr, *, reverse):
    c = pl.program_id(1)

    @pl.when(c == 0)
    def _():
        st_scr[...] = s0_ref[0]

    C = RWKV_CHUNK
    npair = st_scr.shape[0]
    nchunk = r_ref.shape[1] // C
    order = range(nchunk - 1, -1, -1) if reverse else range(nchunk)
    chains = [(slice(u * C, (u + 1) * C), slice(p * PAIR, (p + 1) * PAIR), p) for u in order for p in range(npair)]
    ins = [(r_ref[0, rs, ls], k_ref[0, rs, ls], v_ref[0, rs, ls], kk_ref[0, rs, ls], lw_ref[0, rs, ls],
            a_ref[0, rs, ls], ka_ref[:, ls]) for rs, ls, _ in chains]
    ys, sts = _rwkv_chunk_pairs(ins, [st_scr[p] for p in range(npair)], [p for _, _, p in chains], reverse=reverse)
    for (rs, ls, _), y in zip(chains, ys):
        y_ref[0, rs, ls] = y
    for p in range(npair):
        st_scr[p] = sts[p]

    @pl.when(c == pl.num_programs(1) - 1)
    def _():
        sT_ref[0] = st_scr[...]


def rwkv_state_pack(S):
    B, H = S.shape[:2]
    St = jnp.swapaxes(S, -1, -2).reshape(B, H // 2, 2, RWKV_HEAD, RWKV_HEAD)
    z = jnp.zeros_like(St[:, :, 0])
    top = jnp.concatenate([St[:, :, 0], z], axis=-1)
    bot = jnp.concatenate([z, St[:, :, 1]], axis=-1)
    return jnp.concatenate([top, bot], axis=-2)


def rwkv_state_unpack(St):
    B, P = St.shape[:2]
    h0 = St[:, :, :RWKV_HEAD, :RWKV_HEAD]
    h1 = St[:, :, RWKV_HEAD:, RWKV_HEAD:]
    return jnp.swapaxes(jnp.stack([h0, h1], axis=2).reshape(B, 2 * P, RWKV_HEAD, RWKV_HEAD), -1, -2)


def rwkv_chunked(r, k, v, kk, lw, a, k_a, st0, *, reverse):
    B, L, D = r.shape
    T = RWKV_CHUNK * RWKV_CHUNKS_PER_STEP
    assert L % T == 0 and D % PAIR == 0
    n, P = L // T, D // PAIR
    cidx = (lambda c: n - 1 - c) if reverse else (lambda c: c)
    seq = pl.BlockSpec((1, T, D), lambda b, c: (b, cidx(c), 0))
    state = pl.BlockSpec((1, P, PAIR, PAIR), lambda b, c: (b, 0, 0, 0))
    return pl.pallas_call(
        functools.partial(_rwkv_chunk_body, reverse=reverse),
        grid=(B, n),
        in_specs=[seq] * 6 + [pl.BlockSpec((1, D), lambda b, c: (0, 0)), state],
        out_specs=[seq, state],
        out_shape=[jax.ShapeDtypeStruct((B, L, D), F32), jax.ShapeDtypeStruct((B, P, PAIR, PAIR), F32)],
        scratch_shapes=[pltpu.VMEM((P, PAIR, PAIR), F32)],
        compiler_params=pltpu.CompilerParams(dimension_semantics=("parallel", "arbitrary")),
        name="rwkv_chunked_rev" if reverse else "rwkv_chunked_fwd",
    )(r, k, v, kk, lw, a, k_a.reshape(1, D), st0)


NAT_WIN = NAT_ROWS * GRID_W
NAT_ROWS_PER_BLOCK = 16
NAT_MASKED = -1e30


def _pair_rmsnorm(x, g, lane_lo):
    sq = x * x
    s_lo = jnp.sum(jnp.where(lane_lo, sq, 0.0), axis=-1, keepdims=True)
    s_hi = jnp.sum(sq, axis=-1, keepdims=True) - s_lo
    ms = jnp.where(lane_lo, s_lo, s_hi) * (1.0 / NAT_HEAD)
    return x * lax.rsqrt(ms + NORM_EPS) * g


def _nat_body(q_ref, k_ref, v_ref, kc_ref, vc_ref, bias_ref, qg_ref, kg_ref, o_ref, kn_scr, vb_scr, kcn_scr, vcb_scr,
              *, rows):
    rb = pl.program_id(2)
    L = k_ref.shape[1]
    norm_rows = 512
    lane_lo_n = lax.broadcasted_iota(jnp.int32, (norm_rows, PAIR), 1) < NAT_HEAD

    @pl.when(rb == 0)
    def _():
        def norm_block(i, carry):
            sl = pl.ds(pl.multiple_of(i * norm_rows, norm_rows), norm_rows)
            kn_scr[sl, :] = _pair_rmsnorm(k_ref[0, sl, :], kg_ref[...], lane_lo_n).astype(BF16)
            vb_scr[sl, :] = v_ref[0, sl, :].astype(BF16)
            return carry
        lax.fori_loop(0, L // norm_rows, norm_block, 0)
        lane_lo_c = lax.broadcasted_iota(jnp.int32, kc_ref.shape[1:], 1) < NAT_HEAD
        kcn_scr[...] = _pair_rmsnorm(kc_ref[0], kg_ref[...], lane_lo_c).astype(BF16)
        vcb_scr[...] = vc_ref[0].astype(BF16)

    lane_lo = lax.broadcasted_iota(jnp.int32, (GRID_W, PAIR), 1) < NAT_HEAD
    scale = NAT_HEAD ** -0.5

    rows_per_iter = 8
    nt = (((1,), (1,)), ((), ()))

    def row_group(it, carry):
        qsl, kwin, vwin, qh, bias = [], [], [], [], []
        for u in range(rows_per_iter):
            j = it * rows_per_iter + u
            r = rb * NAT_ROWS_PER_BLOCK + j
            rs = jnp.clip(r - NAT_ROWS // 2, 0, rows - NAT_ROWS)
            sl = pl.ds(pl.multiple_of(j * GRID_W, GRID_W), GRID_W)
            wsl = pl.ds(pl.multiple_of(rs * GRID_W, GRID_W), NAT_WIN)
            qn = _pair_rmsnorm(q_ref[0, sl, :], qg_ref[...], lane_lo) * scale
            for h in range(2):
                qsl.append(sl)
                kwin.append(kn_scr[wsl, :])
                vwin.append(vb_scr[wsl, :])
                qh.append(jnp.where(lane_lo if h == 0 else ~lane_lo, qn, 0.0).astype(BF16))
                bias.append(bias_ref[h, r - rs])
        chains = range(2 * rows_per_iter)
        s_loc = [lax.dot_general(qh[i], kwin[i], nt, preferred_element_type=F32) + bias[i] for i in chains]
        s_ctx = [lax.dot_general(qh[i], kcn_scr[...], nt, preferred_element_type=F32) for i in chains]
        m = [jnp.maximum(jnp.max(s_loc[i], axis=-1, keepdims=True), jnp.max(s_ctx[i], axis=-1, keepdims=True))
             for i in chains]
        p_loc = [jnp.exp(s_loc[i] - m[i]) for i in chains]
        p_ctx = [jnp.exp(s_ctx[i] - m[i]) for i in chains]
        den = [jnp.sum(p_loc[i], axis=-1, keepdims=True) + jnp.sum(p_ctx[i], axis=-1, keepdims=True) for i in chains]
        o = [(jnp.dot(p_loc[i].astype(BF16), vwin[i], preferred_element_type=F32)
              + jnp.dot(p_ctx[i].astype(BF16), vcb_scr[...], preferred_element_type=F32)) / den[i] for i in chains]
        for u in range(rows_per_iter):
            o_ref[0, qsl[2 * u], :] = jnp.where(lane_lo, o[2 * u], o[2 * u + 1])
        return carry

    lax.fori_loop(0, NAT_ROWS_PER_BLOCK // rows_per_iter, row_group, 0)


def _nat_bias_table(rpb):
    col = jnp.arange(GRID_W)
    cstart = jnp.clip(col - NAT_COLS // 2, 0, GRID_W - NAT_COLS)
    delta = jnp.arange(NAT_ROWS)
    wrow = jnp.arange(NAT_ROWS)
    ridx = wrow[None, :] - delta[:, None] + (NAT_ROWS - 1)
    cidx = col[None, :] - col[:, None] + (NAT_COLS - 1)
    inwin = (col[None, :] >= cstart[:, None]) & (col[None, :] < cstart[:, None] + NAT_COLS)
    rsel = (ridx[:, :, None] == jnp.arange(2 * NAT_ROWS - 1)).astype(F32)
    csel = ((cidx[:, :, None] == jnp.arange(2 * NAT_COLS - 1)) & inwin[:, :, None]).astype(F32)
    tab = jnp.einsum('hab,dia,ckb->hdcik', rpb.astype(F32), rsel, csel, precision=lax.Precision.HIGHEST)
    tab = jnp.where(inwin[None, None, :, None, :], tab, NAT_MASKED)
    return tab.reshape(rpb.shape[0], NAT_ROWS, GRID_W, NAT_WIN)


def nat_attention(q, k, v, kc, vc, rpb, qn_g, kn_g):
    B, L, D = q.shape
    Lc = kc.shape[1]
    rows = L // GRID_W
    P = D // PAIR
    rpb_blk = NAT_ROWS_PER_BLOCK
    assert rows >= NAT_ROWS and rows % rpb_blk == 0 and L % 512 == 0
    bias = _nat_bias_table(rpb)
    g2 = lambda g: jnp.tile(g, 2).reshape(1, PAIR)
    full = pl.BlockSpec((1, L, PAIR), lambda b, p, i: (b, 0, p))
    cfull = pl.BlockSpec((1, Lc, PAIR), lambda b, p, i: (b, 0, p))
    qblk = pl.BlockSpec((1, rpb_blk * GRID_W, PAIR), lambda b, p, i: (b, i, p))
    gspec = pl.BlockSpec((1, PAIR), lambda b, p, i: (0, 0))
    return pl.pallas_call(
        functools.partial(_nat_body, rows=rows),
        grid=(B, P, rows // rpb_blk),
        in_specs=[qblk, full, full, cfull, cfull,
                  pl.BlockSpec((2, NAT_ROWS, GRID_W, NAT_WIN), lambda b, p, i: (p, 0, 0, 0)), gspec, gspec],
        out_specs=qblk,
        out_shape=jax.ShapeDtypeStruct((B, L, D), F32),
        scratch_shapes=[pltpu.VMEM((L, PAIR), BF16), pltpu.VMEM((L, PAIR), BF16),
                        pltpu.VMEM((Lc, PAIR), BF16), pltpu.VMEM((Lc, PAIR), BF16)],
        compiler_params=pltpu.CompilerParams(dimension_semantics=("parallel", "parallel", "arbitrary"),
                                             vmem_limit_bytes=48 * 1024 * 1024),
        name="nat_attention",
    )(q, k, v, kc, vc, bias, g2(qn_g), g2(kn_g))


GLA_SUB = 16
GLA_BLOCK = 256
GLA_KPAIR = 2 * GLA_DK
GLA_VPAIR = 2 * GLA_DV
GLA_UNROLL = 4


def _gla_body(q_ref, k_ref, v_ref, g_ref, s0_ref, o_ref, sT_ref, st_scr, *, reverse):
    C = GLA_SUB
    nsub = GLA_BLOCK // C
    npair = st_scr.shape[0]
    blk = pl.program_id(1)

    @pl.when(blk == 0)
    def _():
        st_scr[...] = s0_ref[0]

    ti = lax.broadcasted_iota(jnp.int32, (C, C), 0)
    si = lax.broadcasted_iota(jnp.int32, (C, C), 1)
    tri = ((si >= ti) if reverse else (si <= ti)).astype(F32)
    lane_lo = lax.broadcasted_iota(jnp.int32, (C, GLA_KPAIR), 1) < GLA_DK
    row_id = lax.broadcasted_iota(jnp.int32, (C, GLA_KPAIR), 0)
    vrow = lax.broadcasted_iota(jnp.int32, (2 * C, GLA_VPAIR), 0)
    vcol = lax.broadcasted_iota(jnp.int32, (2 * C, GLA_VPAIR), 1)
    v_same_head = (vrow // C) == (vcol // GLA_DV)
    srow = lax.broadcasted_iota(jnp.int32, (GLA_VPAIR, GLA_KPAIR), 0)
    scol = lax.broadcasted_iota(jnp.int32, (GLA_VPAIR, GLA_KPAIR), 1)
    s_same_head = (srow // GLA_DV) == (scol // GLA_DK)

    pairs = range(npair)
    klanes = [slice(p * GLA_KPAIR, (p + 1) * GLA_KPAIR) for p in pairs]
    vlanes = [slice(p * GLA_VPAIR, (p + 1) * GLA_VPAIR) for p in pairs]

    def sub_chunks(it, carry):
        chains = [(u, p) for u in range(GLA_UNROLL) for p in pairs]
        sl, q, k, g, v = {}, {}, {}, {}, {}
        for u in range(GLA_UNROLL):
            i = it * GLA_UNROLL + u
            ci = (nsub - 1 - i) if reverse else i
            sl[u] = pl.ds(pl.multiple_of(ci * C, C), C)
            for p in pairs:
                q[u, p], k[u, p], g[u, p] = (r[0, sl[u], klanes[p]] for r in (q_ref, k_ref, g_ref))
                v[u, p] = v_ref[0, sl[u], vlanes[p]]
        b = {c: jnp.dot(tri, g[c], preferred_element_type=F32, precision=lax.Precision.HIGHEST) for c in chains}
        b_end = {c: jnp.sum(g[c], axis=0, keepdims=True) for c in chains}
        a_lo = {c: jnp.zeros((C, C), F32) for c in chains}
        a_hi = {c: jnp.zeros((C, C), F32) for c in chains}
        for j in range(C):
            seen = (row_id <= j) if reverse else (row_id >= j)
            for c in chains:
                decay = jnp.exp(jnp.where(seen, b[c] - b[c][j:j + 1, :], 0.0))
                f = jnp.where(seen, q[c] * k[c][j:j + 1, :] * decay, 0.0)
                r_lo = jnp.sum(jnp.where(lane_lo, f, 0.0), axis=-1, keepdims=True)
                r_hi = jnp.sum(jnp.where(lane_lo, 0.0, f), axis=-1, keepdims=True)
                a_lo[c] = jnp.where(si == j, r_lo, a_lo[c])
                a_hi[c] = jnp.where(si == j, r_hi, a_hi[c])
        v_bd = {c: jnp.where(v_same_head, jnp.concatenate([v[c], v[c]], axis=0), 0.0) for c in chains}
        o_in = {c: _mm(jnp.concatenate([a_lo[c], a_hi[c]], axis=1), v_bd[c]) for c in chains}
        kv = {c: jnp.where(s_same_head, _mm_tn(v[c], k[c] * jnp.exp(b_end[c] - b[c])), 0.0) for c in chains}
        qd = {c: q[c] * jnp.exp(b[c]) for c in chains}
        e_end = {c: jnp.exp(b_end[c]) for c in chains}
        st = [st_scr[p] for p in pairs]
        for u in range(GLA_UNROLL):
            for p in pairs:
                o_ref[0, sl[u], vlanes[p]] = o_in[u, p] + _mm_nt(qd[u, p], st[p])
                st[p] = st[p] * e_end[u, p] + kv[u, p]
        for p in pairs:
            st_scr[p] = st[p]
        return carry

    lax.fori_loop(0, nsub // GLA_UNROLL, sub_chunks, 0)

    @pl.when(blk == pl.num_programs(1) - 1)
    def _():
        sT_ref[0] = st_scr[...]


def gla_scan(q, k, v, logg, st0, *, reverse):
    B, L, Dk = q.shape
    Dv = v.shape[-1]
    T = GLA_BLOCK
    assert L % T == 0 and Dk % GLA_KPAIR == 0
    n, P = L // T, Dk // GLA_KPAIR
    bidx = (lambda c: n - 1 - c) if reverse else (lambda c: c)
    kspec = pl.BlockSpec((1, T, Dk), lambda b, c: (b, bidx(c), 0))
    vspec = pl.BlockSpec((1, T, Dv), lambda b, c: (b, bidx(c), 0))
    sspec = pl.BlockSpec((1, P, GLA_VPAIR, GLA_KPAIR), lambda b, c: (b, 0, 0, 0))
    return pl.pallas_call(
        functools.partial(_gla_body, reverse=reverse),
        grid=(B, n),
        in_specs=[kspec, kspec, vspec, kspec, sspec],
        out_specs=[vspec, sspec],
        out_shape=[jax.ShapeDtypeStruct((B, L, Dv), F32),
                   jax.ShapeDtypeStruct((B, P, GLA_VPAIR, GLA_KPAIR), F32)],
        scratch_shapes=[pltpu.VMEM((P, GLA_VPAIR, GLA_KPAIR), F32)],
        compiler_params=pltpu.CompilerParams(dimension_semantics=("parallel", "arbitrary")),
        name="gla_scan_rev" if reverse else "gla_scan_fwd",
    )(q, k, v, logg, st0)


MOE_F_TILE = 512


def _expert_ffn_body(x_ref, g_ref, w1_ref, w3_ref, w2_ref, o_ref, acc_ref):
    f = pl.program_id(3)

    @pl.when(f == 0)
    def _():
        acc_ref[...] = jnp.zeros_like(acc_ref)

    x = x_ref[0, 0].astype(BF16)
    h1 = jnp.dot(x, w1_ref[0].astype(BF16), preferred_element_type=F32)
    h3 = jnp.dot(x, w3_ref[0].astype(BF16), preferred_element_type=F32)
    hid = (h1 * jax.nn.sigmoid(h1) * h3).astype(BF16)
    acc_ref[...] += jnp.dot(hid, w2_ref[0].astype(BF16), preferred_element_type=F32)

    @pl.when(f == pl.num_programs(3) - 1)
    def _():
        o_ref[0, 0] = acc_ref[...] * g_ref[0, 0]


def expert_ffn(xin, gate, w1, w3, w2, layer):
    B, E, cap, D = xin.shape
    F = w1.shape[-1]
    tm = min(cap, 1024)
    tf = F if tm <= 256 else MOE_F_TILE
    assert cap % tm == 0 and F % tf == 0
    return pl.pallas_call(
        _expert_ffn_body,
        grid=(E, B, cap // tm, F // tf),
        in_specs=[
            pl.BlockSpec((1, 1, tm, D), lambda e, b, i, f: (b, e, i, 0)),
            pl.BlockSpec((1, 1, tm, 1), lambda e, b, i, f: (b, e, i, 0)),
            pl.BlockSpec((None, 1, D, tf), lambda e, b, i, f: (layer, e, 0, f)),
            pl.BlockSpec((None, 1, D, tf), lambda e, b, i, f: (layer, e, 0, f)),
            pl.BlockSpec((None, 1, tf, D), lambda e, b, i, f: (layer, e, f, 0)),
        ],
        out_specs=pl.BlockSpec((1, 1, tm, D), lambda e, b, i, f: (b, e, i, 0)),
        out_shape=jax.ShapeDtypeStruct((B, E, cap, D), F32),
        scratch_shapes=[pltpu.VMEM((tm, D), F32)],
        compiler_params=pltpu.CompilerParams(
            dimension_semantics=("parallel", "parallel", "parallel", "arbitrary"),
            vmem_limit_bytes=56 * 1024 * 1024),
        name="expert_ffn",
    )(xin, gate[..., None], w1, w3, w2)


def _split(p, sizes):
    return jnp.split(p, np.cumsum(sizes)[:-1].tolist(), axis=-1)


def rmsnorm(x, g, eps=NORM_EPS):
    xf = x.astype(F32)
    y = xf * lax.rsqrt(jnp.mean(xf * xf, axis=-1, keepdims=True) + eps)
    return (y * g.astype(F32)).astype(x.dtype)


def modulate(x, g, shift, scale):
    return rmsnorm(x, g) * (1 + scale) + shift


def conv3(u, w):
    up = jnp.pad(u, ((0, 0), (1, 1), (0, 0)))
    return up[:, :-2] * w[0] + up[:, 1:-1] * w[1] + up[:, 2:] * w[2]


def rope_2d(x):
    L, d = x.shape[1], x.shape[-1]
    half, nf = d // 2, d // 4
    t = jnp.arange(L)
    inv = ROPE_BASE ** (-jnp.arange(nf, dtype=F32) / nf)

    def rot(u, pos):
        ang = pos.astype(F32)[:, None] * inv[None, :]
        cos, sin = jnp.cos(ang)[None, :, None, :], jnp.sin(ang)[None, :, None, :]
        u1, u2 = u[..., :nf].astype(F32), u[..., nf:].astype(F32)
        return jnp.concatenate([u1 * cos - u2 * sin, u1 * sin + u2 * cos], axis=-1)

    return jnp.concatenate([rot(x[..., :half], t // GRID_W), rot(x[..., half:], t % GRID_W)], axis=-1).astype(x.dtype)


def ec_moe(h, router, w1, w3, w2, layer):
    B, T, D = h.shape
    cap = EC_CAPACITY_FACTOR * T // N_EXPERTS
    aff = jax.nn.softmax((h @ router).astype(F32), axis=-1)
    gate, idx = lax.top_k(jnp.swapaxes(aff, 1, 2), cap)
    xin = jax.vmap(lambda hb, ib: hb[ib])(h, idx)
    if B * cap <= 1024:
        merge = lambda t: jnp.swapaxes(t, 0, 1).reshape((1, N_EXPERTS, B * cap) + t.shape[3:])
        y = expert_ffn(merge(xin), merge(gate), w1, w3, w2, layer)
        y = jnp.swapaxes(y.reshape(N_EXPERTS, B, cap, D), 0, 1)
    else:
        y = expert_ffn(xin, gate, w1, w3, w2, layer)
    out = jax.vmap(lambda ib, yb: jnp.zeros((T, D), yb.dtype).at[ib.reshape(-1)].add(yb.reshape(-1, D)))(idx, y)
    return out.astype(h.dtype)


def rwkv_readout(y, r, k, v, xg, r_k, g_up, ln_g, ln_b):
    B, L = r.shape[:2]
    hs = lambda t: t.reshape(B, L, RWKV_HEADS, RWKV_HEAD)
    yf = hs(y.astype(F32))
    mu = jnp.mean(yf, axis=-1, keepdims=True)
    var = jnp.mean(jnp.square(yf - mu), axis=-1, keepdims=True)
    yn = ((yf - mu) * lax.rsqrt(var + RWKV_LN_EPS)).reshape(B, L, D_RWKV) * ln_g + ln_b
    bonus = (jnp.sum(hs(r) * hs(k) * r_k, axis=-1, keepdims=True) * hs(v)).reshape(B, L, D_RWKV)
    g = jax.nn.sigmoid(xg) @ g_up
    return ((yn + bonus) * g).astype(r.dtype)


EVEN_TAIL = EVEN_SPLIT[6:]


def even_mixer(proj_lat, proj_ctx, conv_w, k_k, k_a, r_k, w0, w_up, a0, a_up, g_up, ln_g, ln_b, need_ctx):
    B = proj_lat[0].shape[0]
    p_lat = list(proj_lat[:6]) + _split(proj_lat[6], EVEN_TAIL)
    p_ctx = list(proj_ctx[:6]) + _split(proj_ctx[6], EVEN_TAIL)

    def conv_branch(u, gate_b, gate_c):
        return gate_b * conv3(gate_c * u, conv_w)

    def unit_key(k):
        Bq, L = k.shape[:2]
        kkf = (k * k_k).astype(F32).reshape(Bq, L, RWKV_HEADS, RWKV_HEAD)
        return (kkf * lax.rsqrt(jnp.sum(kkf * kkf, axis=-1, keepdims=True) + 1e-12)).reshape(Bq, L, D_RWKV)

    def dir_gates(xw, xa, d):
        w_raw = (w0[d] + jnp.tanh(xw) @ w_up[d]).astype(F32)
        return -jnp.exp(-jax.nn.softplus(-w_raw) - 0.5), jax.nn.sigmoid(a0[d] + xa @ a_up[d])

    kk_ctx, kk_lat = unit_key(p_ctx[4]), unit_key(p_lat[4])
    st0 = jnp.zeros((B, RWKV_HEADS // 2, PAIR, PAIR), F32)
    y_lat, y_ctx = 0.0, 0.0
    for d, rev in ((0, False), (1, True)):
        yc, st_c = rwkv_chunked(*p_ctx[3:6], kk_ctx, *dir_gates(*p_ctx[6:8], d), k_a, st0, reverse=rev)
        yl, _ = rwkv_chunked(*p_lat[3:6], kk_lat, *dir_gates(*p_lat[6:8], d), k_a, st_c, reverse=rev)
        y_lat = y_lat + yl
        if need_ctx:
            y_ctx = y_ctx + yc
    cat_lat = jnp.concatenate([conv_branch(*p_lat[0:3]),
                               rwkv_readout(y_lat, *p_lat[3:6], p_lat[8], r_k, g_up, ln_g, ln_b)], axis=-1)
    cat_ctx = None
    if need_ctx:
        cat_ctx = jnp.concatenate([conv_branch(*p_ctx[0:3]),
                                   rwkv_readout(y_ctx, *p_ctx[3:6], p_ctx[8], r_k, g_up, ln_g, ln_b)], axis=-1)
    return cat_lat, cat_ctx


def ctx_attention(q, k, v):
    s = jnp.einsum('bhqd,bhkd->bhqk', q, k).astype(F32) * (q.shape[-1] ** -0.5)
    p = jax.nn.softmax(s, axis=-1).astype(v.dtype)
    return jnp.einsum('bhqk,bhkd->bhqd', p, v)


def gla_log_gate(ga, a_up_d, a_b_d):
    B, L = ga.shape[:2]
    lg = jax.nn.log_sigmoid((ga @ a_up_d + a_b_d).astype(F32)) / GLA_GATE_TEMP
    return lg.reshape(B, L, GLA_HEADS, GLA_DK)


def gla_readout(o, gr, ln_g):
    B, L = gr.shape[:2]
    return (rmsnorm(o, ln_g).reshape(B, L, D_GLA_V) * jax.nn.silu(gr)).astype(gr.dtype)


def odd_mixer(proj_lat, proj_ctx, qn_g, kn_g, rpb, a_up, a_b, gla_ln_g, need_ctx):
    nq, nk, nv, gq, gk, gv, gr, ga = proj_lat
    cnq, cnk, cnv, cgq, cgk, cgv, cgr, cga = proj_ctx
    B, L = nq.shape[:2]
    Lc = cnq.shape[1]

    def nat_heads(t, g=None):
        t = t.reshape(t.shape[0], t.shape[1], NAT_HEADS, NAT_HEAD)
        if g is not None:
            t = rmsnorm(t, g)
        return jnp.swapaxes(t, 1, 2)

    nat_lat = nat_attention(nq, nk, nv, cnk, cnv, rpb, qn_g, kn_g)

    gh = lambda t, d: t.reshape(t.shape[0], t.shape[1], GLA_HEADS, d)
    qscale = GLA_DK ** -0.5
    q = (rope_2d(gh(gq, GLA_DK)) * qscale).reshape(B, L, D_GLA_K)
    k = rope_2d(gh(gk, GLA_DK)).reshape(B, L, D_GLA_K)
    qc = cgq * qscale
    st0 = jnp.zeros((B, GLA_HEADS // 2, GLA_VPAIR, GLA_KPAIR), F32)
    o_lat, o_ctx = 0.0, 0.0
    for d in range(2):
        lg_c = gla_log_gate(cga, a_up[d], a_b[d]).reshape(B, Lc, D_GLA_K)
        lg_l = gla_log_gate(ga, a_up[d], a_b[d]).reshape(B, L, D_GLA_K)
        oc, st_c = gla_scan(qc, cgk, cgv, lg_c, st0, reverse=(d == 1))
        ol, _ = gla_scan(q, k, gv, lg_l, st_c, reverse=(d == 1))
        o_lat = o_lat + ol
        if need_ctx:
            o_ctx = o_ctx + oc
    o_lat = gh(o_lat, GLA_DV)
    if need_ctx:
        o_ctx = gh(o_ctx, GLA_DV)
    cat_lat = jnp.concatenate([nat_lat, gla_readout(o_lat, gr, gla_ln_g)], axis=-1)
    cat_ctx = None
    if need_ctx:
        kc, vc = nat_heads(cnk, kn_g), nat_heads(cnv)
        nat_ctx = jnp.swapaxes(ctx_attention(nat_heads(cnq, qn_g), kc, vc), 1, 2).reshape(B, Lc, D_NAT)
        cat_ctx = jnp.concatenate([nat_ctx, gla_readout(o_ctx, cgr, gla_ln_g)], axis=-1)
    return cat_lat, cat_ctx


def kernel(x, c, ctx, c_ctx, ada_w, ada_b, norm1_g, norm2_g, ev_w_in, ev_w_out, conv_w, rw_k_k, rw_k_a, rw_r_k, rw_w0, rw_w_up, rw_a0, rw_a_up, rw_g_up, rw_ln_g, rw_ln_b, od_w_in, od_w_out, nat_qn_g, nat_kn_g, nat_rpb, gla_a_up, gla_a_b, gla_ln_g, moe_router, moe_w1, moe_w3, moe_w2):
    depth = ada_w.shape[0]
    ctx_s = ctx
    silu_c = jax.nn.silu(c)
    silu_cc = jax.nn.silu(c_ctx)
    for l in range(depth):
        last = l == depth - 1
        j = l // 2
        sh1, sc1, gt1, sh2, sc2, gt2 = _split((silu_c @ ada_w[l] + ada_b[l])[:, None, :], [D_MODEL] * 6)
        csh1, csc1, cgt1, csh2, csc2, cgt2 = _split(silu_cc @ ada_w[l] + ada_b[l], [D_MODEL] * 6)
        even = l % 2 == 0
        w_in = ev_w_in[j] if even else od_w_in[j]
        sizes = EVEN_SPLIT[:6] + [sum(EVEN_TAIL)] if even else ODD_SPLIT
        proj_lat = in_proj_split(x, norm1_g[l], sh1, sc1, w_in, sizes)
        proj_ctx = in_proj_split(ctx_s, norm1_g[l], csh1, csc1, w_in, sizes)
        if even:
            cat_lat, cat_ctx = even_mixer(proj_lat, proj_ctx, conv_w[j], rw_k_k[j], rw_k_a[j], rw_r_k[j],
                                          rw_w0[j], rw_w_up[j], rw_a0[j], rw_a_up[j], rw_g_up[j], rw_ln_g[j],
                                          rw_ln_b[j], not last)
            w_out = ev_w_out[j]
        else:
            cat_lat, cat_ctx = odd_mixer(proj_lat, proj_ctx, nat_qn_g[j], nat_kn_g[j], nat_rpb[j],
                                         gla_a_up[j], gla_a_b[j], gla_ln_g[j], not last)
            w_out = od_w_out[j]
        experts = (moe_w1, moe_w3, moe_w2, l)
        x = out_proj_residual(cat_lat, w_out, x, gt1)
        x = x + gt2 * ec_moe(modulate(x, norm2_g[l], sh2, sc2), moe_router[l], *experts)
        if not last:
            ctx_s = out_proj_residual(cat_ctx, w_out, ctx_s, cgt1)
            ctx_s = ctx_s + cgt2 * ec_moe(modulate(ctx_s, norm2_g[l], csh2, csc2), moe_router[l], *experts)
    return x
```

```python
import functools

import jax
import jax.numpy as jnp
import numpy as np
from jax import lax
from jax.experimental import pallas as pl
from jax.experimental.pallas import tpu as pltpu

D_MODEL = 1024
GRID_W = 64
NORM_EPS = 1e-6
F32 = jnp.float32
BF16 = jnp.bfloat16

D_CONV = 512
D_RWKV = 512
RWKV_HEAD = 64
RWKV_HEADS = D_RWKV // RWKV_HEAD
RWKV_DECAY_RANK = 64
RWKV_ICLR_RANK = 64
RWKV_GATE_RANK = 128
RWKV_LN_EPS = 64e-5
D_NAT = 512
NAT_HEAD = 64
NAT_HEADS = D_NAT // NAT_HEAD
NAT_ROWS = 8
NAT_COLS = 16
GLA_HEADS = 4
GLA_DK = 64
GLA_DV = 128
D_GLA_K = GLA_HEADS * GLA_DK
D_GLA_V = GLA_HEADS * GLA_DV
GLA_GATE_RANK = 16
GLA_GATE_TEMP = 16.0
ROPE_BASE = 10000.0
N_EXPERTS = 16
EC_CAPACITY_FACTOR = 2

EVEN_SPLIT = [D_CONV, D_CONV, D_CONV, D_RWKV, D_RWKV, D_RWKV, RWKV_DECAY_RANK, RWKV_ICLR_RANK, RWKV_GATE_RANK]
ODD_SPLIT = [D_NAT, D_NAT, D_NAT, D_GLA_K, D_GLA_K, D_GLA_V, D_GLA_V, GLA_GATE_RANK]


def _out_proj_body(a_ref, w_ref, x_ref, g_ref, o_ref):
    acc = jnp.dot(a_ref[0].astype(BF16), w_ref[...], preferred_element_type=F32)
    o_ref[0] = x_ref[0] + g_ref[0] * acc


def out_proj_residual(a, w, x, gate, block_rows=512):
    B, L, K = a.shape
    N = w.shape[1]
    tm = min(block_rows, L)
    assert L % tm == 0
    gate = jnp.broadcast_to(gate.reshape(-1, 1, N), (B, 1, N))
    return pl.pallas_call(
        _out_proj_body,
        grid=(B, L // tm),
        in_specs=[
            pl.BlockSpec((1, tm, K), lambda b, i: (b, i, 0)),
            pl.BlockSpec((K, N), lambda b, i: (0, 0)),
            pl.BlockSpec((1, tm, N), lambda b, i: (b, i, 0)),
            pl.BlockSpec((1, 1, N), lambda b, i: (b, 0, 0)),
        ],
        out_specs=pl.BlockSpec((1, tm, N), lambda b, i: (b, i, 0)),
        out_shape=jax.ShapeDtypeStruct((B, L, N), F32),
        compiler_params=pltpu.CompilerParams(dimension_semantics=("parallel", "parallel")),
        name="out_proj_residual",
    )(a, w.astype(BF16), x, gate)


def _in_proj_body(x_ref, g_ref, shift_ref, scale_ref, w_ref, *o_refs, offsets):
    xf = x_ref[0]
    y = xf * lax.rsqrt(jnp.mean(xf * xf, axis=-1, keepdims=True) + NORM_EPS) * g_ref[...]
    h = (y * (1.0 + scale_ref[0]) + shift_ref[0]).astype(BF16)
    for o_ref, off in zip(o_refs, offsets):
        o_ref[0] = jnp.dot(h, w_ref[:, off:off + o_ref.shape[-1]], preferred_element_type=F32)


def in_proj_split(x, g, shift, scale, w, sizes, block_rows=512):
    B, L, D = x.shape
    N = w.shape[1]
    assert sum(sizes) == N
    tm = min(block_rows, L)
    assert L % tm == 0
    offsets = tuple(int(o) for o in np.cumsum([0] + list(sizes[:-1])))
    per_sample = lambda t: jnp.broadcast_to(t.reshape(-1, 1, D), (B, 1, D))
    vec = pl.BlockSpec((1, 1, D), lambda b, i: (b, 0, 0))
    return pl.pallas_call(
        functools.partial(_in_proj_body, offsets=offsets),
        grid=(B, L // tm),
        in_specs=[pl.BlockSpec((1, tm, D), lambda b, i: (b, i, 0)), pl.BlockSpec((1, D), lambda b, i: (0, 0)),
                  vec, vec, pl.BlockSpec((D, N), lambda b, i: (0, 0))],
        out_specs=[pl.BlockSpec((1, tm, s), lambda b, i: (b, i, 0)) for s in sizes],
        out_shape=[jax.ShapeDtypeStruct((B, L, s), F32) for s in sizes],
        compiler_params=pltpu.CompilerParams(dimension_semantics=("parallel", "parallel"),
                                             vmem_limit_bytes=48 * 1024 * 1024),
        name="in_proj_split",
    )(x, g.reshape(1, D), per_sample(shift), per_sample(scale), w.astype(BF16))


RWKV_CHUNK = 64
RWKV_CHUNKS_PER_STEP = 4
PAIR = 2 * RWKV_HEAD


def _mm(a, b):
    return jnp.dot(a.astype(BF16), b.astype(BF16), preferred_element_type=F32)


def _mm_nt(a, b):
    return lax.dot_general(a.astype(BF16), b.astype(BF16), (((1,), (1,)), ((), ())), preferred_element_type=F32)


def _mm_tn(a, b):
    return lax.dot_general(a.astype(BF16), b.astype(BF16), (((0,), (0,)), ((), ())), preferred_element_type=F32)


def _rwkv_chunk_pairs(ins, sts, slots, *, reverse):
    C = RWKV_CHUNK
    row = lax.broadcasted_iota(jnp.int32, (PAIR, PAIR), 0)
    col = lax.broadcasted_iota(jnp.int32, (PAIR, PAIR), 1)
    same_head = (row // RWKV_HEAD) == (col // RWKV_HEAD)
    t_i, s_i = row % C, col % C
    before = (s_i > t_i) if reverse else (s_i < t_i)
    upto = before | (s_i == t_i)

    ct = lax.broadcasted_iota(jnp.int32, (C, C), 0)
    cs = lax.broadcasted_iota(jnp.int32, (C, C), 1)
    tri = ((cs >= ct) if reverse else (cs <= ct)).astype(F32)
    eye = jnp.where(row == col, 1.0, 0.0)

    def expand(x):
        return jnp.where(same_head, jnp.concatenate([x, x], axis=0), 0.0)

    def prepare(r, k, v, kk, lw, a, ka):
        keff = k * (1.0 + (a - 1.0) * ka)
        b = kk * a
        cum_in = jnp.dot(tri, lw, preferred_element_type=F32, precision=lax.Precision.HIGHEST)
        cum_ex = cum_in - lw
        tot = jnp.sum(lw, axis=0, keepdims=True)
        e_neg = jnp.exp(-cum_in)
        e_rem = jnp.exp(tot - cum_in)
        return dict(at2=expand(-kk * jnp.exp(cum_ex)), rt2=expand(r * jnp.exp(cum_in)),
                    bh2=expand(b * e_neg), kh2=expand(keff * e_neg), bp2=expand(b * e_rem),
                    kp2=expand(keff * e_rem), v2=expand(v), dtot=eye * jnp.exp(tot))

    ps = [prepare(*args) for args in ins]
    pps = [_mm_nt(jnp.concatenate([p["at2"], p["rt2"]], axis=0), jnp.concatenate([p["bh2"], p["kh2"]], axis=0))
           for p in ps]
    a_ab = [jnp.where(before, pp[:PAIR, :PAIR], 0.0) for pp in pps]
    a_ak = [jnp.where(before, pp[:PAIR, PAIR:], 0.0) for pp in pps]
    a_rb = [jnp.where(upto, pp[PAIR:, :PAIR], 0.0) for pp in pps]
    a_rk = [jnp.where(upto, pp[PAIR:, PAIR:], 0.0) for pp in pps]

    tinv = [eye + n for n in a_ab]
    npow = a_ab
    for _ in range(int(np.log2(C)) - 1):
        npow = [_mm(n, n) for n in npow]
        tinv = [t + _mm(t, n) for t, n in zip(tinv, npow)]

    av = [_mm(jnp.concatenate([ak, rk], axis=0), p["v2"]) for ak, rk, p in zip(a_ak, a_rk, ps)]
    x = [_mm(t, jnp.concatenate([p["at2"], w[:PAIR]], axis=1)) for t, p, w in zip(tinv, ps, av)]
    z = [_mm(rb, xx) for rb, xx in zip(a_rb, x)]
    bx = [_mm_tn(p["bp2"], xx) for p, xx in zip(ps, x)]
    kv = [_mm_tn(p["kp2"], p["v2"]) for p in ps]
    qe = [p["rt2"] + zz[:, :PAIR] for p, zz in zip(ps, z)]
    yloc = [zz[:, PAIR:] + w[PAIR:] for zz, w in zip(z, av)]
    mt = [p["dtot"] + b[:, :PAIR] for p, b in zip(ps, bx)]
    gt = [b[:, PAIR:] + g for b, g in zip(bx, kv)]
    sts = list(sts)
    ys = []
    for i, s in enumerate(slots):
        y2 = _mm(qe[i], sts[s]) + yloc[i]
        sts[s] = _mm(mt[i], sts[s]) + gt[i]
        ys.append(y2[:C] + y2[C:])
    return ys, sts


def _rwkv_chunk_body(r_ref, k_ref, v_ref, kk_ref, lw_ref, a_ref, ka_ref, s0_ref, y_ref, sT_ref, st_scr, *, reverse):
    c = pl.program_id(1)

    @pl.when(c == 0)
    def _():
        st_scr[...] = s0_ref[0]

    C = RWKV_CHUNK
    npair = st_scr.shape[0]
    nchunk = r_ref.shape[1] // C
    order = range(nchunk - 1, -1, -1) if reverse else range(nchunk)
    chains = [(slice(u * C, (u + 1) * C), slice(p * PAIR, (p + 1) * PAIR), p) for u in order for p in range(npair)]
    ins = [(r_ref[0, rs, ls], k_ref[0, rs, ls], v_ref[0, rs, ls], kk_ref[0, rs, ls], lw_ref[0, rs, ls],
            a_ref[0, rs, ls], ka_ref[:, ls]) for rs, ls, _ in chains]
    ys, sts = _rwkv_chunk_pairs(ins, [st_scr[p] for p in range(npair)], [p for _, _, p in chains], reverse=reverse)
    for (rs, ls, _), y in zip(chains, ys):
        y_ref[0, rs, ls] = y
    for p in range(npair):
        st_scr[p] = sts[p]

    @pl.when(c == pl.num_programs(1) - 1)
    def _():
        sT_ref[0] = st_scr[...]


def rwkv_chunked(r, k, v, kk, lw, a, k_a, st0, *, reverse):
    B, L, D = r.shape
    T = RWKV_CHUNK * RWKV_CHUNKS_PER_STEP
    assert L % T == 0 and D % PAIR == 0
    n, P = L // T, D // PAIR
    cidx = (lambda c: n - 1 - c) if reverse else (lambda c: c)
    seq = pl.BlockSpec((1, T, D), lambda b, c: (b, cidx(c), 0))
    state = pl.BlockSpec((1, P, PAIR, PAIR), lambda b, c: (b, 0, 0, 0))
    return pl.pallas_call(
        functools.partial(_rwkv_chunk_body, reverse=reverse),
        grid=(B, n),
        in_specs=[seq] * 6 + [pl.BlockSpec((1, D), lambda b, c: (0, 0)), state],
        out_specs=[seq, state],
        out_shape=[jax.ShapeDtypeStruct((B, L, D), F32), jax.ShapeDtypeStruct((B, P, PAIR, PAIR), F32)],
        scratch_shapes=[pltpu.VMEM((P, PAIR, PAIR), F32)],
        compiler_params=pltpu.CompilerParams(dimension_semantics=("parallel", "arbitrary")),
        name="rwkv_chunked_rev" if reverse else "rwkv_chunked_fwd",
    )(r, k, v, kk, lw, a, k_a.reshape(1, D), st0)


NAT_WIN = NAT_ROWS * GRID_W
NAT_ROWS_PER_BLOCK = 16
NAT_MASKED = -1e30


def _pair_rmsnorm(x, g, lane_lo):
    sq = x * x
    s_lo = jnp.sum(jnp.where(lane_lo, sq, 0.0), axis=-1, keepdims=True)
    s_hi = jnp.sum(sq, axis=-1, keepdims=True) - s_lo
    ms = jnp.where(lane_lo, s_lo, s_hi) * (1.0 / NAT_HEAD)
    return x * lax.rsqrt(ms + NORM_EPS) * g


def _nat_body(q_ref, k_ref, v_ref, kc_ref, vc_ref, bias_ref, qg_ref, kg_ref, o_ref, kn_scr, vb_scr, kcn_scr, vcb_scr,
              *, rows):
    rb = pl.program_id(2)
    L = k_ref.shape[1]
    norm_rows = 512
    lane_lo_n = lax.broadcasted_iota(jnp.int32, (norm_rows, PAIR), 1) < NAT_HEAD

    @pl.when(rb == 0)
    def _():
        def norm_block(i, carry):
            sl = pl.ds(pl.multiple_of(i * norm_rows, norm_rows), norm_rows)
            kn_scr[sl, :] = _pair_rmsnorm(k_ref[0, sl, :], kg_ref[...], lane_lo_n).astype(BF16)
            vb_scr[sl, :] = v_ref[0, sl, :].astype(BF16)
            return carry
        lax.fori_loop(0, L // norm_rows, norm_block, 0)
        lane_lo_c = lax.broadcasted_iota(jnp.int32, kc_ref.shape[1:], 1) < NAT_HEAD
        kcn_scr[...] = _pair_rmsnorm(kc_ref[0], kg_ref[...], lane_lo_c).astype(BF16)
        vcb_scr[...] = vc_ref[0].astype(BF16)

    lane_lo = lax.broadcasted_iota(jnp.int32, (GRID_W, PAIR), 1) < NAT_HEAD
    scale = NAT_HEAD ** -0.5

    rows_per_iter = 8
    nt = (((1,), (1,)), ((), ()))

    def row_group(it, carry):
        qsl, kwin, vwin, qh, bias = [], [], [], [], []
        for u in range(rows_per_iter):
            j = it * rows_per_iter + u
            r = rb * NAT_ROWS_PER_BLOCK + j
            rs = jnp.clip(r - NAT_ROWS // 2, 0, rows - NAT_ROWS)
            sl = pl.ds(pl.multiple_of(j * GRID_W, GRID_W), GRID_W)
            wsl = pl.ds(pl.multiple_of(rs * GRID_W, GRID_W), NAT_WIN)
            qn = _pair_rmsnorm(q_ref[0, sl, :], qg_ref[...], lane_lo) * scale
            for h in range(2):
                qsl.append(sl)
                kwin.append(kn_scr[wsl, :])
                vwin.append(vb_scr[wsl, :])
                qh.append(jnp.where(lane_lo if h == 0 else ~lane_lo, qn, 0.0).astype(BF16))
                bias.append(bias_ref[h, r - rs])
        chains = range(2 * rows_per_iter)
        s_loc = [lax.dot_general(qh[i], kwin[i], nt, preferred_element_type=F32) + bias[i] for i in chains]
        s_ctx = [lax.dot_general(qh[i], kcn_scr[...], nt, preferred_element_type=F32) for i in chains]
        m = [jnp.maximum(jnp.max(s_loc[i], axis=-1, keepdims=True), jnp.max(s_ctx[i], axis=-1, keepdims=True))
             for i in chains]
        p_loc = [jnp.exp(s_loc[i] - m[i]) for i in chains]
        p_ctx = [jnp.exp(s_ctx[i] - m[i]) for i in chains]
        den = [jnp.sum(p_loc[i], axis=-1, keepdims=True) + jnp.sum(p_ctx[i], axis=-1, keepdims=True) for i in chains]
        o = [(jnp.dot(p_loc[i].astype(BF16), vwin[i], preferred_element_type=F32)
              + jnp.dot(p_ctx[i].astype(BF16), vcb_scr[...], preferred_element_type=F32)) / den[i] for i in chains]
        for u in range(rows_per_iter):
            o_ref[0, qsl[2 * u], :] = jnp.where(lane_lo, o[2 * u], o[2 * u + 1])
        return carry

    lax.fori_loop(0, NAT_ROWS_PER_BLOCK // rows_per_iter, row_group, 0)


def _nat_bias_table(rpb):
    col = jnp.arange(GRID_W)
    cstart = jnp.clip(col - NAT_COLS // 2, 0, GRID_W - NAT_COLS)
    delta = jnp.arange(NAT_ROWS)
    wrow = jnp.arange(NAT_ROWS)
    ridx = wrow[None, :] - delta[:, None] + (NAT_ROWS - 1)
    cidx = col[None, :] - col[:, None] + (NAT_COLS - 1)
    inwin = (col[None, :] >= cstart[:, None]) & (col[None, :] < cstart[:, None] + NAT_COLS)
    rsel = (ridx[:, :, None] == jnp.arange(2 * NAT_ROWS - 1)).astype(F32)
    csel = ((cidx[:, :, None] == jnp.arange(2 * NAT_COLS - 1)) & inwin[:, :, None]).astype(F32)
    tab = jnp.einsum('hab,dia,ckb->hdcik', rpb.astype(F32), rsel, csel, precision=lax.Precision.HIGHEST)
    tab = jnp.where(inwin[None, None, :, None, :], tab, NAT_MASKED)
    return tab.reshape(rpb.shape[0], NAT_ROWS, GRID_W, NAT_WIN)


def nat_attention(q, k, v, kc, vc, rpb, qn_g, kn_g):
    B, L, D = q.shape
    Lc = kc.shape[1]
    rows = L // GRID_W
    P = D // PAIR
    rpb_blk = NAT_ROWS_PER_BLOCK
    assert rows >= NAT_ROWS and rows % rpb_blk == 0 and L % 512 == 0
    bias = _nat_bias_table(rpb)
    g2 = lambda g: jnp.tile(g, 2).reshape(1, PAIR)
    full = pl.BlockSpec((1, L, PAIR), lambda b, p, i: (b, 0, p))
    cfull = pl.BlockSpec((1, Lc, PAIR), lambda b, p, i: (b, 0, p))
    qblk = pl.BlockSpec((1, rpb_blk * GRID_W, PAIR), lambda b, p, i: (b, i, p))
    gspec = pl.BlockSpec((1, PAIR), lambda b, p, i: (0, 0))
    return pl.pallas_call(
        functools.partial(_nat_body, rows=rows),
        grid=(B, P, rows // rpb_blk),
        in_specs=[qblk, full, full, cfull, cfull,
                  pl.BlockSpec((2, NAT_ROWS, GRID_W, NAT_WIN), lambda b, p, i: (p, 0, 0, 0)), gspec, gspec],
        out_specs=qblk,
        out_shape=jax.ShapeDtypeStruct((B, L, D), F32),
        scratch_shapes=[pltpu.VMEM((L, PAIR), BF16), pltpu.VMEM((L, PAIR), BF16),
                        pltpu.VMEM((Lc, PAIR), BF16), pltpu.VMEM((Lc, PAIR), BF16)],
        compiler_params=pltpu.CompilerParams(dimension_semantics=("parallel", "parallel", "arbitrary"),
                                             vmem_limit_bytes=48 * 1024 * 1024),
        name="nat_attention",
    )(q, k, v, kc, vc, bias, g2(qn_g), g2(kn_g))


GLA_SUB = 16
GLA_BLOCK = 256
GLA_KPAIR = 2 * GLA_DK
GLA_VPAIR = 2 * GLA_DV
GLA_UNROLL = 8


def _gla_body(q_ref, k_ref, v_ref, g_ref, s0_ref, o_ref, sT_ref, st_scr, *, reverse):
    C = GLA_SUB
    nsub = GLA_BLOCK // C
    npair = st_scr.shape[0]
    blk = pl.program_id(1)

    @pl.when(blk == 0)
    def _():
        st_scr[...] = s0_ref[0]

    ti = lax.broadcasted_iota(jnp.int32, (C, C), 0)
    si = lax.broadcasted_iota(jnp.int32, (C, C), 1)
    tri = ((si >= ti) if reverse else (si <= ti)).astype(F32)
    lane_lo = lax.broadcasted_iota(jnp.int32, (C, GLA_KPAIR), 1) < GLA_DK
    row_id = lax.broadcasted_iota(jnp.int32, (C, GLA_KPAIR), 0)
    vrow = lax.broadcasted_iota(jnp.int32, (2 * C, GLA_VPAIR), 0)
    vcol = lax.broadcasted_iota(jnp.int32, (2 * C, GLA_VPAIR), 1)
    v_same_head = (vrow // C) == (vcol // GLA_DV)
    srow = lax.broadcasted_iota(jnp.int32, (GLA_VPAIR, GLA_KPAIR), 0)
    scol = lax.broadcasted_iota(jnp.int32, (GLA_VPAIR, GLA_KPAIR), 1)
    s_same_head = (srow // GLA_DV) == (scol // GLA_DK)

    pairs = range(npair)
    klanes = [slice(p * GLA_KPAIR, (p + 1) * GLA_KPAIR) for p in pairs]
    vlanes = [slice(p * GLA_VPAIR, (p + 1) * GLA_VPAIR) for p in pairs]

    def sub_chunks(it, carry):
        chains = [(u, p) for u in range(GLA_UNROLL) for p in pairs]
        sl, q, k, g, v = {}, {}, {}, {}, {}
        for u in range(GLA_UNROLL):
            i = it * GLA_UNROLL + u
            ci = (nsub - 1 - i) if reverse else i
            sl[u] = pl.ds(pl.multiple_of(ci * C, C), C)
            for p in pairs:
                q[u, p], k[u, p], g[u, p] = (r[0, sl[u], klanes[p]] for r in (q_ref, k_ref, g_ref))
                v[u, p] = v_ref[0, sl[u], vlanes[p]]
        b = {c: jnp.dot(tri, g[c], preferred_element_type=F32, precision=lax.Precision.HIGHEST) for c in chains}
        b_end = {c: jnp.sum(g[c], axis=0, keepdims=True) for c in chains}
        a_lo = {c: jnp.zeros((C, C), F32) for c in chains}
        a_hi = {c: jnp.zeros((C, C), F32) for c in chains}
        for j in range(C):
            seen = (row_id <= j) if reverse else (row_id >= j)
            for c in chains:
                decay = jnp.exp(jnp.where(seen, b[c] - b[c][j:j + 1, :], 0.0))
                f = jnp.where(seen, q[c] * k[c][j:j + 1, :] * decay, 0.0)
                r_lo = jnp.sum(jnp.where(lane_lo, f, 0.0), axis=-1, keepdims=True)
                r_hi = jnp.sum(jnp.where(lane_lo, 0.0, f), axis=-1, keepdims=True)
                a_lo[c] = jnp.where(si == j, r_lo, a_lo[c])
                a_hi[c] = jnp.where(si == j, r_hi, a_hi[c])
        v_bd = {c: jnp.where(v_same_head, jnp.concatenate([v[c], v[c]], axis=0), 0.0) for c in chains}
        o_in = {c: _mm(jnp.concatenate([a_lo[c], a_hi[c]], axis=1), v_bd[c]) for c in chains}
        kv = {c: jnp.where(s_same_head, _mm_tn(v[c], k[c] * jnp.exp(b_end[c] - b[c])), 0.0) for c in chains}
        qd = {c: q[c] * jnp.exp(b[c]) for c in chains}
        e_end = {c: jnp.exp(b_end[c]) for c in chains}
        st = [st_scr[p] for p in pairs]
        for u in range(GLA_UNROLL):
            for p in pairs:
                o_ref[0, sl[u], vlanes[p]] = o_in[u, p] + _mm_nt(qd[u, p], st[p])
                st[p] = st[p] * e_end[u, p] + kv[u, p]
        for p in pairs:
            st_scr[p] = st[p]
        return carry

    lax.fori_loop(0, nsub // GLA_UNROLL, sub_chunks, 0)

    @pl.when(blk == pl.num_programs(1) - 1)
    def _():
        sT_ref[0] = st_scr[...]


def gla_scan(q, k, v, logg, st0, *, reverse):
    B, L, Dk = q.shape
    Dv = v.shape[-1]
    T = GLA_BLOCK
    assert L % T == 0 and Dk % GLA_KPAIR == 0
    n, P = L // T, Dk // GLA_KPAIR
    bidx = (lambda c: n - 1 - c) if reverse else (lambda c: c)
    kspec = pl.BlockSpec((1, T, Dk), lambda b, c: (b, bidx(c), 0))
    vspec = pl.BlockSpec((1, T, Dv), lambda b, c: (b, bidx(c), 0))
    sspec = pl.BlockSpec((1, P, GLA_VPAIR, GLA_KPAIR), lambda b, c: (b, 0, 0, 0))
    return pl.pallas_call(
        functools.partial(_gla_body, reverse=reverse),
        grid=(B, n),
        in_specs=[kspec, kspec, vspec, kspec, sspec],
        out_specs=[vspec, sspec],
        out_shape=[jax.ShapeDtypeStruct((B, L, Dv), F32),
                   jax.ShapeDtypeStruct((B, P, GLA_VPAIR, GLA_KPAIR), F32)],
        scratch_shapes=[pltpu.VMEM((P, GLA_VPAIR, GLA_KPAIR), F32)],
        compiler_params=pltpu.CompilerParams(dimension_semantics=("parallel", "arbitrary")),
        name="gla_scan_rev" if reverse else "gla_scan_fwd",
    )(q, k, v, logg, st0)


MOE_F_TILE = 512


def _expert_ffn_body(x_ref, g_ref, w1_ref, w3_ref, w2_ref, o_ref, acc_ref):
    f = pl.program_id(3)

    @pl.when(f == 0)
    def _():
        acc_ref[...] = jnp.zeros_like(acc_ref)

    x = x_ref[0, 0].astype(BF16)
    h1 = jnp.dot(x, w1_ref[0].astype(BF16), preferred_element_type=F32)
    h3 = jnp.dot(x, w3_ref[0].astype(BF16), preferred_element_type=F32)
    hid = (h1 * jax.nn.sigmoid(h1) * h3).astype(BF16)
    acc_ref[...] += jnp.dot(hid, w2_ref[0].astype(BF16), preferred_element_type=F32)

    @pl.when(f == pl.num_programs(3) - 1)
    def _():
        o_ref[0, 0] = acc_ref[...] * g_ref[0, 0]


def expert_ffn(xin, gate, w1, w3, w2, layer):
    B, E, cap, D = xin.shape
    F = w1.shape[-1]
    tm = min(cap, 1024)
    tf = F if tm <= 256 else MOE_F_TILE
    assert cap % tm == 0 and F % tf == 0
    return pl.pallas_call(
        _expert_ffn_body,
        grid=(E, B, cap // tm, F // tf),
        in_specs=[
            pl.BlockSpec((1, 1, tm, D), lambda e, b, i, f: (b, e, i, 0)),
            pl.BlockSpec((1, 1, tm, 1), lambda e, b, i, f: (b, e, i, 0)),
            pl.BlockSpec((None, 1, D, tf), lambda e, b, i, f: (layer, e, 0, f)),
            pl.BlockSpec((None, 1, D, tf), lambda e, b, i, f: (layer, e, 0, f)),
            pl.BlockSpec((None, 1, tf, D), lambda e, b, i, f: (layer, e, f, 0)),
        ],
        out_specs=pl.BlockSpec((1, 1, tm, D), lambda e, b, i, f: (b, e, i, 0)),
        out_shape=jax.ShapeDtypeStruct((B, E, cap, D), F32),
        scratch_shapes=[pltpu.VMEM((tm, D), F32)],
        compiler_params=pltpu.CompilerParams(
            dimension_semantics=("parallel", "parallel", "parallel", "arbitrary"),
            vmem_limit_bytes=56 * 1024 * 1024),
        name="expert_ffn",
    )(xin, gate[..., None], w1, w3, w2)


def _split(p, sizes):
    return jnp.split(p, np.cumsum(sizes)[:-1].tolist(), axis=-1)


def rmsnorm(x, g, eps=NORM_EPS):
    xf = x.astype(F32)
    y = xf * lax.rsqrt(jnp.mean(xf * xf, axis=-1, keepdims=True) + eps)
    return (y * g.astype(F32)).astype(x.dtype)


def modulate(x, g, shift, scale):
    return rmsnorm(x, g) * (1 + scale) + shift


def conv3(u, w):
    up = jnp.pad(u, ((0, 0), (1, 1), (0, 0)))
    return up[:, :-2] * w[0] + up[:, 1:-1] * w[1] + up[:, 2:] * w[2]


def rope_2d(x):
    L, d = x.shape[1], x.shape[-1]
    half, nf = d // 2, d // 4
    t = jnp.arange(L)
    inv = ROPE_BASE ** (-jnp.arange(nf, dtype=F32) / nf)

    def rot(u, pos):
        ang = pos.astype(F32)[:, None] * inv[None, :]
        cos, sin = jnp.cos(ang)[None, :, None, :], jnp.sin(ang)[None, :, None, :]
        u1, u2 = u[..., :nf].astype(F32), u[..., nf:].astype(F32)
        return jnp.concatenate([u1 * cos - u2 * sin, u1 * sin + u2 * cos], axis=-1)

    return jnp.concatenate([rot(x[..., :half], t // GRID_W), rot(x[..., half:], t % GRID_W)], axis=-1).astype(x.dtype)


def ec_moe(h, router, w1, w3, w2, layer):
    B, T, D = h.shape
    cap = EC_CAPACITY_FACTOR * T // N_EXPERTS
    aff = jax.nn.softmax((h @ router).astype(F32), axis=-1)
    gate, idx = lax.top_k(jnp.swapaxes(aff, 1, 2), cap)
    xin = jax.vmap(lambda hb, ib: hb[ib])(h, idx)
    if B * cap <= 1024:
        merge = lambda t: jnp.swapaxes(t, 0, 1).reshape((1, N_EXPERTS, B * cap) + t.shape[3:])
        y = expert_ffn(merge(xin), merge(gate), w1, w3, w2, layer)
        y = jnp.swapaxes(y.reshape(N_EXPERTS, B, cap, D), 0, 1)
    else:
        y = expert_ffn(xin, gate, w1, w3, w2, layer)
    out = jax.vmap(lambda ib, yb: jnp.zeros((T, D), yb.dtype).at[ib.reshape(-1)].add(yb.reshape(-1, D)))(idx, y)
    return out.astype(h.dtype)


def rwkv_readout(y, r, k, v, xg, r_k, g_up, ln_g, ln_b):
    B, L = r.shape[:2]
    hs = lambda t: t.reshape(B, L, RWKV_HEADS, RWKV_HEAD)
    yf = hs(y.astype(F32))
    mu = jnp.mean(yf, axis=-1, keepdims=True)
    var = jnp.mean(jnp.square(yf - mu), axis=-1, keepdims=True)
    yn = ((yf - mu) * lax.rsqrt(var + RWKV_LN_EPS)).reshape(B, L, D_RWKV) * ln_g + ln_b
    bonus = (jnp.sum(hs(r) * hs(k) * r_k, axis=-1, keepdims=True) * hs(v)).reshape(B, L, D_RWKV)
    g = jax.nn.sigmoid(xg) @ g_up
    return ((yn + bonus) * g).astype(r.dtype)


EVEN_TAIL = EVEN_SPLIT[6:]


def even_mixer(proj_lat, proj_ctx, conv_w, k_k, k_a, r_k, w0, w_up, a0, a_up, g_up, ln_g, ln_b, need_ctx):
    B = proj_lat[0].shape[0]
    p_lat = list(proj_lat[:6]) + _split(proj_lat[6], EVEN_TAIL)
    p_ctx = list(proj_ctx[:6]) + _split(proj_ctx[6], EVEN_TAIL)

    def conv_branch(u, gate_b, gate_c):
        return gate_b * conv3(gate_c * u, conv_w)

    def unit_key(k):
        Bq, L = k.shape[:2]
        kkf = (k * k_k).astype(F32).reshape(Bq, L, RWKV_HEADS, RWKV_HEAD)
        return (kkf * lax.rsqrt(jnp.sum(kkf * kkf, axis=-1, keepdims=True) + 1e-12)).reshape(Bq, L, D_RWKV)

    def dir_gates(xw, xa, d):
        w_raw = (w0[d] + jnp.tanh(xw) @ w_up[d]).astype(F32)
        return -jnp.exp(-jax.nn.softplus(-w_raw) - 0.5), jax.nn.sigmoid(a0[d] + xa @ a_up[d])

    kk_ctx, kk_lat = unit_key(p_ctx[4]), unit_key(p_lat[4])
    st0 = jnp.zeros((B, RWKV_HEADS // 2, PAIR, PAIR), F32)
    y_lat, y_ctx = 0.0, 0.0
    for d, rev in ((0, False), (1, True)):
        yc, st_c = rwkv_chunked(*p_ctx[3:6], kk_ctx, *dir_gates(*p_ctx[6:8], d), k_a, st0, reverse=rev)
        yl, _ = rwkv_chunked(*p_lat[3:6], kk_lat, *dir_gates(*p_lat[6:8], d), k_a, st_c, reverse=rev)
        y_lat = y_lat + yl
        if need_ctx:
            y_ctx = y_ctx + yc
    cat_lat = jnp.concatenate([conv_branch(*p_lat[0:3]),
                               rwkv_readout(y_lat, *p_lat[3:6], p_lat[8], r_k, g_up, ln_g, ln_b)], axis=-1)
    cat_ctx = None
    if need_ctx:
        cat_ctx = jnp.concatenate([conv_branch(*p_ctx[0:3]),
                                   rwkv_readout(y_ctx, *p_ctx[3:6], p_ctx[8], r_k, g_up, ln_g, ln_b)], axis=-1)
    return cat_lat, cat_ctx


def ctx_attention(q, k, v):
    s = jnp.einsum('bhqd,bhkd->bhqk', q, k).astype(F32) * (q.shape[-1] ** -0.5)
    p = jax.nn.softmax(s, axis=-1).astype(v.dtype)
    return jnp.einsum('bhqk,bhkd->bhqd', p, v)


def gla_log_gate(ga, a_up_d, a_b_d):
    B, L = ga.shape[:2]
    lg = jax.nn.log_sigmoid((ga @ a_up_d + a_b_d).astype(F32)) / GLA_GATE_TEMP
    return lg.reshape(B, L, GLA_HEADS, GLA_DK)


def gla_readout(o, gr, ln_g):
    B, L = gr.shape[:2]
    return (rmsnorm(o, ln_g).reshape(B, L, D_GLA_V) * jax.nn.silu(gr)).astype(gr.dtype)


def odd_mixer(proj_lat, proj_ctx, qn_g, kn_g, rpb, a_up, a_b, gla_ln_g, need_ctx):
    nq, nk, nv, gq, gk, gv, gr, ga = proj_lat
    cnq, cnk, cnv, cgq, cgk, cgv, cgr, cga = proj_ctx
    B, L = nq.shape[:2]
    Lc = cnq.shape[1]

    def nat_heads(t, g=None):
        t = t.reshape(t.shape[0], t.shape[1], NAT_HEADS, NAT_HEAD)
        if g is not None:
            t = rmsnorm(t, g)
        return jnp.swapaxes(t, 1, 2)

    nat_lat = nat_attention(nq, nk, nv, cnk, cnv, rpb, qn_g, kn_g)

    gh = lambda t, d: t.reshape(t.shape[0], t.shape[1], GLA_HEADS, d)
    qscale = GLA_DK ** -0.5
    q = (rope_2d(gh(gq, GLA_DK)) * qscale).reshape(B, L, D_GLA_K)
    k = rope_2d(gh(gk, GLA_DK)).reshape(B, L, D_GLA_K)
    qc = cgq * qscale
    st0 = jnp.zeros((B, GLA_HEADS // 2, GLA_VPAIR, GLA_KPAIR), F32)
    o_lat, o_ctx = 0.0, 0.0
    for d in range(2):
        lg_c = gla_log_gate(cga, a_up[d], a_b[d]).reshape(B, Lc, D_GLA_K)
        lg_l = gla_log_gate(ga, a_up[d], a_b[d]).reshape(B, L, D_GLA_K)
        oc, st_c = gla_scan(qc, cgk, cgv, lg_c, st0, reverse=(d == 1))
        ol, _ = gla_scan(q, k, gv, lg_l, st_c, reverse=(d == 1))
        o_lat = o_lat + ol
        if need_ctx:
            o_ctx = o_ctx + oc
    o_lat = gh(o_lat, GLA_DV)
    if need_ctx:
        o_ctx = gh(o_ctx, GLA_DV)
    cat_lat = jnp.concatenate([nat_lat, gla_readout(o_lat, gr, gla_ln_g)], axis=-1)
    cat_ctx = None
    if need_ctx:
        kc, vc = nat_heads(cnk, kn_g), nat_heads(cnv)
        nat_ctx = jnp.swapaxes(ctx_attention(nat_heads(cnq, qn_g), kc, vc), 1, 2).reshape(B, Lc, D_NAT)
        cat_ctx = jnp.concatenate([nat_ctx, gla_readout(o_ctx, cgr, gla_ln_g)], axis=-1)
    return cat_lat, cat_ctx


def kernel(x, c, ctx, c_ctx, ada_w, ada_b, norm1_g, norm2_g, ev_w_in, ev_w_out, conv_w, rw_k_k, rw_k_a, rw_r_k, rw_w0, rw_w_up, rw_a0, rw_a_up, rw_g_up, rw_ln_g, rw_ln_b, od_w_in, od_w_out, nat_qn_g, nat_kn_g, nat_rpb, gla_a_up, gla_a_b, gla_ln_g, moe_router, moe_w1, moe_w3, moe_w2):
    depth = ada_w.shape[0]
    ctx_s = ctx
    silu_c = jax.nn.silu(c)
    silu_cc = jax.nn.silu(c_ctx)
    for l in range(depth):
        last = l == depth - 1
        j = l // 2
        sh1, sc1, gt1, sh2, sc2, gt2 = _split((silu_c @ ada_w[l] + ada_b[l])[:, None, :], [D_MODEL] * 6)
        csh1, csc1, cgt1, csh2, csc2, cgt2 = _split(silu_cc @ ada_w[l] + ada_b[l], [D_MODEL] * 6)
        even = l % 2 == 0
        w_in = ev_w_in[j] if even else od_w_in[j]
        sizes = EVEN_SPLIT[:6] + [sum(EVEN_TAIL)] if even else ODD_SPLIT
        proj_lat = in_proj_split(x, norm1_g[l], sh1, sc1, w_in, sizes)
        proj_ctx = in_proj_split(ctx_s, norm1_g[l], csh1, csc1, w_in, sizes)
        if even:
            cat_lat, cat_ctx = even_mixer(proj_lat, proj_ctx, conv_w[j], rw_k_k[j], rw_k_a[j], rw_r_k[j],
                                          rw_w0[j], rw_w_up[j], rw_a0[j], rw_a_up[j], rw_g_up[j], rw_ln_g[j],
                                          rw_ln_b[j], not last)
            w_out = ev_w_out[j]
        else:
            cat_lat, cat_ctx = odd_mixer(proj_lat, proj_ctx, nat_qn_g[j], nat_kn_g[j], nat_rpb[j],
                                         gla_a_up[j], gla_a_b[j], gla_ln_g[j], not last)
            w_out = od_w_out[j]
        experts = (moe_w1, moe_w3, moe_w2, l)
        x = out_proj_residual(cat_lat, w_out, x, gt1)
        x = x + gt2 * ec_moe(modulate(x, norm2_g[l], sh2, sc2), moe_router[l], *experts)
        if not last:
            ctx_s = out_proj_residual(cat_ctx, w_out, ctx_s, cgt1)
            ctx_s = ctx_s + cgt2 * ec_moe(modulate(ctx_s, norm2_g[l], csh2, csc2), moe_router[l], *experts)
    return x
```

```python
import functools

import jax
import jax.numpy as jnp
import numpy as np
from jax import lax
from jax.experimental import pallas as pl
from jax.experimental.pallas import tpu as pltpu

D_MODEL = 1024
GRID_W = 64
NORM_EPS = 1e-6
F32 = jnp.float32
BF16 = jnp.bfloat16

D_CONV = 512
D_RWKV = 512
RWKV_HEAD = 64
RWKV_HEADS = D_RWKV // RWKV_HEAD
RWKV_DECAY_RANK = 64
RWKV_ICLR_RANK = 64
RWKV_GATE_RANK = 128
RWKV_LN_EPS = 64e-5
D_NAT = 512
NAT_HEAD = 64
NAT_HEADS = D_NAT // NAT_HEAD
NAT_ROWS = 8
NAT_COLS = 16
GLA_HEADS = 4
GLA_DK = 64
GLA_DV = 128
D_GLA_K = GLA_HEADS * GLA_DK
D_GLA_V = GLA_HEADS * GLA_DV
GLA_GATE_RANK = 16
GLA_GATE_TEMP = 16.0
ROPE_BASE = 10000.0
N_EXPERTS = 16
EC_CAPACITY_FACTOR = 2

EVEN_SPLIT = [D_CONV, D_CONV, D_CONV, D_RWKV, D_RWKV, D_RWKV, RWKV_DECAY_RANK, RWKV_ICLR_RANK, RWKV_GATE_RANK]
ODD_SPLIT = [D_NAT, D_NAT, D_NAT, D_GLA_K, D_GLA_K, D_GLA_V, D_GLA_V, GLA_GATE_RANK]


def _out_proj_body(a_ref, w_ref, x_ref, g_ref, o_ref):
    acc = jnp.dot(a_ref[0].astype(BF16), w_ref[...], preferred_element_type=F32)
    o_ref[0] = x_ref[0] + g_ref[0] * acc


def out_proj_residual(a, w, x, gate, block_rows=512):
    B, L, K = a.shape
    N = w.shape[1]
    tm = min(block_rows, L)
    assert L % tm == 0
    gate = jnp.broadcast_to(gate.reshape(-1, 1, N), (B, 1, N))
    return pl.pallas_call(
        _out_proj_body,
        grid=(B, L // tm),
        in_specs=[
            pl.BlockSpec((1, tm, K), lambda b, i: (b, i, 0)),
            pl.BlockSpec((K, N), lambda b, i: (0, 0)),
            pl.BlockSpec((1, tm, N), lambda b, i: (b, i, 0)),
            pl.BlockSpec((1, 1, N), lambda b, i: (b, 0, 0)),
        ],
        out_specs=pl.BlockSpec((1, tm, N), lambda b, i: (b, i, 0)),
        out_shape=jax.ShapeDtypeStruct((B, L, N), F32),
        compiler_params=pltpu.CompilerParams(dimension_semantics=("parallel", "parallel")),
        name="out_proj_residual",
    )(a, w.astype(BF16), x, gate)


def _in_proj_body(x_ref, g_ref, shift_ref, scale_ref, w_ref, *o_refs, offsets):
    xf = x_ref[0]
    y = xf * lax.rsqrt(jnp.mean(xf * xf, axis=-1, keepdims=True) + NORM_EPS) * g_ref[...]
    h = (y * (1.0 + scale_ref[0]) + shift_ref[0]).astype(BF16)
    for o_ref, off in zip(o_refs, offsets):
        o_ref[0] = jnp.dot(h, w_ref[:, off:off + o_ref.shape[-1]], preferred_element_type=F32)


def in_proj_split(x, g, shift, scale, w, sizes, block_rows=512):
    B, L, D = x.shape
    N = w.shape[1]
    assert sum(sizes) == N
    tm = min(block_rows, L)
    assert L % tm == 0
    offsets = tuple(int(o) for o in np.cumsum([0] + list(sizes[:-1])))
    per_sample = lambda t: jnp.broadcast_to(t.reshape(-1, 1, D), (B, 1, D))
    vec = pl.BlockSpec((1, 1, D), lambda b, i: (b, 0, 0))
    return pl.pallas_call(
        functools.partial(_in_proj_body, offsets=offsets),
        grid=(B, L // tm),
        in_specs=[pl.BlockSpec((1, tm, D), lambda b, i: (b, i, 0)), pl.BlockSpec((1, D), lambda b, i: (0, 0)),
                  vec, vec, pl.BlockSpec((D, N), lambda b, i: (0, 0))],
        out_specs=[pl.BlockSpec((1, tm, s), lambda b, i: (b, i, 0)) for s in sizes],
        out_shape=[jax.ShapeDtypeStruct((B, L, s), F32) for s in sizes],
        compiler_params=pltpu.CompilerParams(dimension_semantics=("parallel", "parallel"),
                                             vmem_limit_bytes=48 * 1024 * 1024),
        name="in_proj_split",
    )(x, g.reshape(1, D), per_sample(shift), per_sample(scale), w.astype(BF16))


RWKV_CHUNK = 64
RWKV_CHUNKS_PER_STEP = 4
PAIR = 2 * RWKV_HEAD


def _mm(a, b):
    return jnp.dot(a.astype(BF16), b.astype(BF16), preferred_element_type=F32)


def _mm_nt(a, b):
    return lax.dot_general(a.astype(BF16), b.astype(BF16), (((1,), (1,)), ((), ())), preferred_element_type=F32)


def _mm_tn(a, b):
    return lax.dot_general(a.astype(BF16), b.astype(BF16), (((0,), (0,)), ((), ())), preferred_element_type=F32)


def _rwkv_chunk_pairs(ins, sts, slots, *, reverse):
    C = RWKV_CHUNK
    row = lax.broadcasted_iota(jnp.int32, (PAIR, PAIR), 0)
    col = lax.broadcasted_iota(jnp.int32, (PAIR, PAIR), 1)
    same_head = (row // RWKV_HEAD) == (col // RWKV_HEAD)
    t_i, s_i = row % C, col % C
    before = (s_i > t_i) if reverse else (s_i < t_i)
    upto = before | (s_i == t_i)

    ct = lax.broadcasted_iota(jnp.int32, (C, C), 0)
    cs = lax.broadcasted_iota(jnp.int32, (C, C), 1)
    tri = ((cs >= ct) if reverse else (cs <= ct)).astype(F32)
    eye = jnp.where(row == col, 1.0, 0.0)

    def expand(x):
        return jnp.where(same_head, jnp.concatenate([x, x], axis=0), 0.0)

    def prepare(r, k, v, kk, lw, a, ka):
        keff = k * (1.0 + (a - 1.0) * ka)
        b = kk * a
        cum_in = jnp.dot(tri, lw, preferred_element_type=F32, precision=lax.Precision.HIGHEST)
        cum_ex = cum_in - lw
        tot = jnp.sum(lw, axis=0, keepdims=True)
        e_neg = jnp.exp(-cum_in)
        e_rem = jnp.exp(tot - cum_in)
        return dict(at2=expand(-kk * jnp.exp(cum_ex)), rt2=expand(r * jnp.exp(cum_in)),
                    bh2=expand(b * e_neg), kh2=expand(keff * e_neg), bp2=expand(b * e_rem),
                    kp2=expand(keff * e_rem), v2=expand(v), dtot=eye * jnp.exp(tot))

    ps = [prepare(*args) for args in ins]
    pps = [_mm_nt(jnp.concatenate([p["at2"], p["rt2"]], axis=0), jnp.concatenate([p["bh2"], p["kh2"]], axis=0))
           for p in ps]
    a_ab = [jnp.where(before, pp[:PAIR, :PAIR], 0.0) for pp in pps]
    a_ak = [jnp.where(before, pp[:PAIR, PAIR:], 0.0) for pp in pps]
    a_rb = [jnp.where(upto, pp[PAIR:, :PAIR], 0.0) for pp in pps]
    a_rk = [jnp.where(upto, pp[PAIR:, PAIR:], 0.0) for pp in pps]

    tinv = [eye + n for n in a_ab]
    npow = a_ab
    for _ in range(int(np.log2(C)) - 1):
        npow = [_mm(n, n) for n in npow]
        tinv = [t + _mm(t, n) for t, n in zip(tinv, npow)]

    av = [_mm(jnp.concatenate([ak, rk], axis=0), p["v2"]) for ak, rk, p in zip(a_ak, a_rk, ps)]
    x = [_mm(t, jnp.concatenate([p["at2"], w[:PAIR]], axis=1)) for t, p, w in zip(tinv, ps, av)]
    z = [_mm(rb, xx) for rb, xx in zip(a_rb, x)]
    bx = [_mm_tn(p["bp2"], xx) for p, xx in zip(ps, x)]
    kv = [_mm_tn(p["kp2"], p["v2"]) for p in ps]
    qe = [p["rt2"] + zz[:, :PAIR] for p, zz in zip(ps, z)]
    yloc = [zz[:, PAIR:] + w[PAIR:] for zz, w in zip(z, av)]
    mt = [p["dtot"] + b[:, :PAIR] for p, b in zip(ps, bx)]
    gt = [b[:, PAIR:] + g for b, g in zip(bx, kv)]
    sts = list(sts)
    ys = []
    for i, s in enumerate(slots):
        y2 = _mm(qe[i], sts[s]) + yloc[i]
        sts[s] = _mm(mt[i], sts[s]) + gt[i]
        ys.append(y2[:C] + y2[C:])
    return ys, sts


def _rwkv_chunk_body(r_ref, k_ref, v_ref, kk_ref, lw_ref, a_ref, ka_ref, s0_ref, y_ref, sT_ref, st_scr, *, reverse):
    c = pl.program_id(1)

    @pl.when(c == 0)
    def _():
        st_scr[...] = s0_ref[0]

    C = RWKV_CHUNK
    npair = st_scr.shape[0]
    nchunk = r_ref.shape[1] // C
    order = range(nchunk - 1, -1, -1) if reverse else range(nchunk)
    chains = [(slice(u * C, (u + 1) * C), slice(p * PAIR, (p + 1) * PAIR), p) for u in order for p in range(npair)]
    ins = [(r_ref[0, rs, ls], k_ref[0, rs, ls], v_ref[0, rs, ls], kk_ref[0, rs, ls], lw_ref[0, rs, ls],
            a_ref[0, rs, ls], ka_ref[:, ls]) for rs, ls, _ in chains]
    ys, sts = _rwkv_chunk_pairs(ins, [st_scr[p] for p in range(npair)], [p for _, _, p in chains], reverse=reverse)
    for (rs, ls, _), y in zip(chains, ys):
        y_ref[0, rs, ls] = y
    for p in range(npair):
        st_scr[p] = sts[p]

    @pl.when(c == pl.num_programs(1) - 1)
    def _():
        sT_ref[0] = st_scr[...]


def rwkv_chunked(r, k, v, kk, lw, a, k_a, st0, *, reverse):
    B, L, D = r.shape
    T = RWKV_CHUNK * RWKV_CHUNKS_PER_STEP
    assert L % T == 0 and D % PAIR == 0
    n, P = L // T, D // PAIR
    cidx = (lambda c: n - 1 - c) if reverse else (lambda c: c)
    seq = pl.BlockSpec((1, T, D), lambda b, c: (b, cidx(c), 0))
    state = pl.BlockSpec((1, P, PAIR, PAIR), lambda b, c: (b, 0, 0, 0))
    return pl.pallas_call(
        functools.partial(_rwkv_chunk_body, reverse=reverse),
        grid=(B, n),
        in_specs=[seq] * 6 + [pl.BlockSpec((1, D), lambda b, c: (0, 0)), state],
        out_specs=[seq, state],
        out_shape=[jax.ShapeDtypeStruct((B, L, D), F32), jax.ShapeDtypeStruct((B, P, PAIR, PAIR), F32)],
        scratch_shapes=[pltpu.VMEM((P, PAIR, PAIR), F32)],
        compiler_params=pltpu.CompilerParams(dimension_semantics=("parallel", "arbitrary")),
        name="rwkv_chunked_rev" if reverse else "rwkv_chunked_fwd",
    )(r, k, v, kk, lw, a, k_a.reshape(1, D), st0)


NAT_WIN = NAT_ROWS * GRID_W
NAT_ROWS_PER_BLOCK = 16
NAT_MASKED = -1e30


def _pair_rmsnorm(x, g, lane_lo):
    sq = x * x
    s_lo = jnp.sum(jnp.where(lane_lo, sq, 0.0), axis=-1, keepdims=True)
    s_hi = jnp.sum(sq, axis=-1, keepdims=True) - s_lo
    ms = jnp.where(lane_lo, s_lo, s_hi) * (1.0 / NAT_HEAD)
    return x * lax.rsqrt(ms + NORM_EPS) * g


def _nat_body(q_ref, k_ref, v_ref, kc_ref, vc_ref, bias_ref, qg_ref, kg_ref, o_ref, kn_scr, vb_scr, kcn_scr, vcb_scr,
              *, rows):
    rb = pl.program_id(2)
    L = k_ref.shape[1]
    norm_rows = 512
    lane_lo_n = lax.broadcasted_iota(jnp.int32, (norm_rows, PAIR), 1) < NAT_HEAD

    @pl.when(rb == 0)
    def _():
        def norm_block(i, carry):
            sl = pl.ds(pl.multiple_of(i * norm_rows, norm_rows), norm_rows)
            kn_scr[sl, :] = _pair_rmsnorm(k_ref[0, sl, :], kg_ref[...], lane_lo_n).astype(BF16)
            vb_scr[sl, :] = v_ref[0, sl, :].astype(BF16)
            return carry
        lax.fori_loop(0, L // norm_rows, norm_block, 0)
        lane_lo_c = lax.broadcasted_iota(jnp.int32, kc_ref.shape[1:], 1) < NAT_HEAD
        kcn_scr[...] = _pair_rmsnorm(kc_ref[0], kg_ref[...], lane_lo_c).astype(BF16)
        vcb_scr[...] = vc_ref[0].astype(BF16)

    lane_lo = lax.broadcasted_iota(jnp.int32, (GRID_W, PAIR), 1) < NAT_HEAD
    scale = NAT_HEAD ** -0.5

    rows_per_iter = 8
    nt = (((1,), (1,)), ((), ()))

    def row_group(it, carry):
        qsl, kwin, vwin, qh, bias = [], [], [], [], []
        for u in range(rows_per_iter):
            j = it * rows_per_iter + u
            r = rb * NAT_ROWS_PER_BLOCK + j
            rs = jnp.clip(r - NAT_ROWS // 2, 0, rows - NAT_ROWS)
            sl = pl.ds(pl.multiple_of(j * GRID_W, GRID_W), GRID_W)
            wsl = pl.ds(pl.multiple_of(rs * GRID_W, GRID_W), NAT_WIN)
            qn = _pair_rmsnorm(q_ref[0, sl, :], qg_ref[...], lane_lo) * scale
            for h in range(2):
                qsl.append(sl)
                kwin.append(kn_scr[wsl, :])
                vwin.append(vb_scr[wsl, :])
                qh.append(jnp.where(lane_lo if h == 0 else ~lane_lo, qn, 0.0).astype(BF16))
                bias.append(bias_ref[h, r - rs])
        chains = range(2 * rows_per_iter)
        s_loc = [lax.dot_general(qh[i], kwin[i], nt, preferred_element_type=F32) + bias[i] for i in chains]
        s_ctx = [lax.dot_general(qh[i], kcn_scr[...], nt, preferred_element_type=F32) for i in chains]
        m = [jnp.maximum(jnp.max(s_loc[i], axis=-1, keepdims=True), jnp.max(s_ctx[i], axis=-1, keepdims=True))
             for i in chains]
        p_loc = [jnp.exp(s_loc[i] - m[i]) for i in chains]
        p_ctx = [jnp.exp(s_ctx[i] - m[i]) for i in chains]
        den = [jnp.sum(p_loc[i], axis=-1, keepdims=True) + jnp.sum(p_ctx[i], axis=-1, keepdims=True) for i in chains]
        o = [(jnp.dot(p_loc[i].astype(BF16), vwin[i], preferred_element_type=F32)
              + jnp.dot(p_ctx[i].astype(BF16), vcb_scr[...], preferred_element_type=F32)) / den[i] for i in chains]
        for u in range(rows_per_iter):
            o_ref[0, qsl[2 * u], :] = jnp.where(lane_lo, o[2 * u], o[2 * u + 1])
        return carry

    lax.fori_loop(0, NAT_ROWS_PER_BLOCK // rows_per_iter, row_group, 0)


def _nat_bias_table(rpb):
    col = jnp.arange(GRID_W)
    cstart = jnp.clip(col - NAT_COLS // 2, 0, GRID_W - NAT_COLS)
    delta = jnp.arange(NAT_ROWS)
    wrow = jnp.arange(NAT_ROWS)
    ridx = wrow[None, :] - delta[:, None] + (NAT_ROWS - 1)
    cidx = col[None, :] - col[:, None] + (NAT_COLS - 1)
    inwin = (col[None, :] >= cstart[:, None]) & (col[None, :] < cstart[:, None] + NAT_COLS)
    rsel = (ridx[:, :, None] == jnp.arange(2 * NAT_ROWS - 1)).astype(F32)
    csel = ((cidx[:, :, None] == jnp.arange(2 * NAT_COLS - 1)) & inwin[:, :, None]).astype(F32)
    tab = jnp.einsum('hab,dia,ckb->hdcik', rpb.astype(F32), rsel, csel, precision=lax.Precision.HIGHEST)
    tab = jnp.where(inwin[None, None, :, None, :], tab, NAT_MASKED)
    return tab.reshape(rpb.shape[0], NAT_ROWS, GRID_W, NAT_WIN)


def nat_attention(q, k, v, kc, vc, rpb, qn_g, kn_g):
    B, L, D = q.shape
    Lc = kc.shape[1]
    rows = L // GRID_W
    P = D // PAIR
    rpb_blk = NAT_ROWS_PER_BLOCK
    assert rows >= NAT_ROWS and rows % rpb_blk == 0 and L % 512 == 0
    bias = _nat_bias_table(rpb)
    g2 = lambda g: jnp.tile(g, 2).reshape(1, PAIR)
    full = pl.BlockSpec((1, L, PAIR), lambda b, p, i: (b, 0, p))
    cfull = pl.BlockSpec((1, Lc, PAIR), lambda b, p, i: (b, 0, p))
    qblk = pl.BlockSpec((1, rpb_blk * GRID_W, PAIR), lambda b, p, i: (b, i, p))
    gspec = pl.BlockSpec((1, PAIR), lambda b, p, i: (0, 0))
    return pl.pallas_call(
        functools.partial(_nat_body, rows=rows),
        grid=(B, P, rows // rpb_blk),
        in_specs=[qblk, full, full, cfull, cfull,
                  pl.BlockSpec((2, NAT_ROWS, GRID_W, NAT_WIN), lambda b, p, i: (p, 0, 0, 0)), gspec, gspec],
        out_specs=qblk,
        out_shape=jax.ShapeDtypeStruct((B, L, D), F32),
        scratch_shapes=[pltpu.VMEM((L, PAIR), BF16), pltpu.VMEM((L, PAIR), BF16),
                        pltpu.VMEM((Lc, PAIR), BF16), pltpu.VMEM((Lc, PAIR), BF16)],
        compiler_params=pltpu.CompilerParams(dimension_semantics=("parallel", "parallel", "arbitrary"),
                                             vmem_limit_bytes=48 * 1024 * 1024),
        name="nat_attention",
    )(q, k, v, kc, vc, bias, g2(qn_g), g2(kn_g))


GLA_SUB = 16
GLA_BLOCK = 256
GLA_KPAIR = 2 * GLA_DK
GLA_VPAIR = 2 * GLA_DV
GLA_UNROLL = 8


def _gla_body(q_ref, k_ref, v_ref, g_ref, s0_ref, o_ref, sT_ref, st_scr, *, reverse):
    C = GLA_SUB
    nsub = GLA_BLOCK // C
    npair = st_scr.shape[0]
    blk = pl.program_id(1)

    @pl.when(blk == 0)
    def _():
        st_scr[...] = s0_ref[0]

    ti = lax.broadcasted_iota(jnp.int32, (C, C), 0)
    si = lax.broadcasted_iota(jnp.int32, (C, C), 1)
    tri = ((si >= ti) if reverse else (si <= ti)).astype(F32)
    lane_lo = lax.broadcasted_iota(jnp.int32, (C, GLA_KPAIR), 1) < GLA_DK
    row_id = lax.broadcasted_iota(jnp.int32, (C, GLA_KPAIR), 0)
    vrow = lax.broadcasted_iota(jnp.int32, (2 * C, GLA_VPAIR), 0)
    vcol = lax.broadcasted_iota(jnp.int32, (2 * C, GLA_VPAIR), 1)
    v_same_head = (vrow // C) == (vcol // GLA_DV)
    srow = lax.broadcasted_iota(jnp.int32, (GLA_VPAIR, GLA_KPAIR), 0)
    scol = lax.broadcasted_iota(jnp.int32, (GLA_VPAIR, GLA_KPAIR), 1)
    s_same_head = (srow // GLA_DV) == (scol // GLA_DK)

    pairs = range(npair)
    klanes = [slice(p * GLA_KPAIR, (p + 1) * GLA_KPAIR) for p in pairs]
    vlanes = [slice(p * GLA_VPAIR, (p + 1) * GLA_VPAIR) for p in pairs]

    def sub_chunks(it, carry):
        chains = [(u, p) for u in range(GLA_UNROLL) for p in pairs]
        sl, q, k, g, v = {}, {}, {}, {}, {}
        for u in range(GLA_UNROLL):
            i = it * GLA_UNROLL + u
            ci = (nsub - 1 - i) if reverse else i
            sl[u] = pl.ds(pl.multiple_of(ci * C, C), C)
            for p in pairs:
                q[u, p], k[u, p], g[u, p] = (r[0, sl[u], klanes[p]] for r in (q_ref, k_ref, g_ref))
                v[u, p] = v_ref[0, sl[u], vlanes[p]]
        b = {c: jnp.dot(tri, g[c], preferred_element_type=F32, precision=lax.Precision.HIGHEST) for c in chains}
        b_end = {c: jnp.sum(g[c], axis=0, keepdims=True) for c in chains}
        a_lo = {c: jnp.zeros((C, C), F32) for c in chains}
        a_hi = {c: jnp.zeros((C, C), F32) for c in chains}
        for j in range(C):
            seen = (row_id <= j) if reverse else (row_id >= j)
            for c in chains:
                decay = jnp.exp(jnp.where(seen, b[c] - b[c][j:j + 1, :], 0.0))
                f = jnp.where(seen, q[c] * k[c][j:j + 1, :] * decay, 0.0)
                r_lo = jnp.sum(jnp.where(lane_lo, f, 0.0), axis=-1, keepdims=True)
                r_hi = jnp.sum(jnp.where(lane_lo, 0.0, f), axis=-1, keepdims=True)
                a_lo[c] = jnp.where(si == j, r_lo, a_lo[c])
                a_hi[c] = jnp.where(si == j, r_hi, a_hi[c])
        v_bd = {c: jnp.where(v_same_head, jnp.concatenate([v[c], v[c]], axis=0), 0.0) for c in chains}
        o_in = {c: _mm(jnp.concatenate([a_lo[c], a_hi[c]], axis=1), v_bd[c]) for c in chains}
        kv = {c: jnp.where(s_same_head, _mm_tn(v[c], k[c] * jnp.exp(b_end[c] - b[c])), 0.0) for c in chains}
        qd = {c: q[c] * jnp.exp(b[c]) for c in chains}
        e_end = {c: jnp.exp(b_end[c]) for c in chains}
        st = [st_scr[p] for p in pairs]
        for u in range(GLA_UNROLL):
            for p in pairs:
                o_ref[0, sl[u], vlanes[p]] = o_in[u, p] + _mm_nt(qd[u, p], st[p])
                st[p] = st[p] * e_end[u, p] + kv[u, p]
        for p in pairs:
            st_scr[p] = st[p]
        return carry

    lax.fori_loop(0, nsub // GLA_UNROLL, sub_chunks, 0)

    @pl.when(blk == pl.num_programs(1) - 1)
    def _():
        sT_ref[0] = st_scr[...]


def gla_scan(q, k, v, logg, st0, *, reverse):
    B, L, Dk = q.shape
    Dv = v.shape[-1]
    T = GLA_BLOCK
    assert L % T == 0 and Dk % GLA_KPAIR == 0
    n, P = L // T, Dk // GLA_KPAIR
    bidx = (lambda c: n - 1 - c) if reverse else (lambda c: c)
    kspec = pl.BlockSpec((1, T, Dk), lambda b, c: (b, bidx(c), 0))
    vspec = pl.BlockSpec((1, T, Dv), lambda b, c: (b, bidx(c), 0))
    sspec = pl.BlockSpec((1, P, GLA_VPAIR, GLA_KPAIR), lambda b, c: (b, 0, 0, 0))
    return pl.pallas_call(
        functools.partial(_gla_body, reverse=reverse),
        grid=(B, n),
        in_specs=[kspec, kspec, vspec, kspec, sspec],
        out_specs=[vspec, sspec],
        out_shape=[jax.ShapeDtypeStruct((B, L, Dv), F32),
                   jax.ShapeDtypeStruct((B, P, GLA_VPAIR, GLA_KPAIR), F32)],
        scratch_shapes=[pltpu.VMEM((P, GLA_VPAIR, GLA_KPAIR), F32)],
        compiler_params=pltpu.CompilerParams(dimension_semantics=("parallel", "arbitrary")),
        name="gla_scan_rev" if reverse else "gla_scan_fwd",
    )(q, k, v, logg, st0)


MOE_F_TILE = 512


def _expert_ffn_body(x_ref, g_ref, w1_ref, w3_ref, w2_ref, o_ref, acc_ref):
    f = pl.program_id(3)

    @pl.when(f == 0)
    def _():
        acc_ref[...] = jnp.zeros_like(acc_ref)

    x = x_ref[0, 0].astype(BF16)
    h1 = jnp.dot(x, w1_ref[0].astype(BF16), preferred_element_type=F32)
    h3 = jnp.dot(x, w3_ref[0].astype(BF16), preferred_element_type=F32)
    hid = (h1 * jax.nn.sigmoid(h1) * h3).astype(BF16)
    acc_ref[...] += jnp.dot(hid, w2_ref[0].astype(BF16), preferred_element_type=F32)

    @pl.when(f == pl.num_programs(3) - 1)
    def _():
        o_ref[0, 0] = acc_ref[...] * g_ref[0, 0]


def expert_ffn(xin, gate, w1, w3, w2, layer):
    B, E, cap, D = xin.shape
    F = w1.shape[-1]
    tm = min(cap, 1024)
    tf = F if tm <= 256 else MOE_F_TILE
    assert cap % tm == 0 and F % tf == 0
    return pl.pallas_call(
        _expert_ffn_body,
        grid=(E, B, cap // tm, F // tf),
        in_specs=[
            pl.BlockSpec((1, 1, tm, D), lambda e, b, i, f: (b, e, i, 0)),
            pl.BlockSpec((1, 1, tm, 1), lambda e, b, i, f: (b, e, i, 0)),
            pl.BlockSpec((None, 1, D, tf), lambda e, b, i, f: (layer, e, 0, f)),
            pl.BlockSpec((None, 1, D, tf), lambda e, b, i, f: (layer, e, 0, f)),
            pl.BlockSpec((None, 1, tf, D), lambda e, b, i, f: (layer, e, f, 0)),
        ],
        out_specs=pl.BlockSpec((1, 1, tm, D), lambda e, b, i, f: (b, e, i, 0)),
        out_shape=jax.ShapeDtypeStruct((B, E, cap, D), F32),
        scratch_shapes=[pltpu.VMEM((tm, D), F32)],
        compiler_params=pltpu.CompilerParams(
            dimension_semantics=("parallel", "parallel", "parallel", "arbitrary"),
            vmem_limit_bytes=56 * 1024 * 1024),
        name="expert_ffn",
    )(xin, gate[..., None], w1, w3, w2)


def _split(p, sizes):
    return jnp.split(p, np.cumsum(sizes)[:-1].tolist(), axis=-1)


def rmsnorm(x, g, eps=NORM_EPS):
    xf = x.astype(F32)
    y = xf * lax.rsqrt(jnp.mean(xf * xf, axis=-1, keepdims=True) + eps)
    return (y * g.astype(F32)).astype(x.dtype)


def modulate(x, g, shift, scale):
    return rmsnorm(x, g) * (1 + scale) + shift


def conv3(u, w):
    up = jnp.pad(u, ((0, 0), (1, 1), (0, 0)))
    return up[:, :-2] * w[0] + up[:, 1:-1] * w[1] + up[:, 2:] * w[2]


def rope_2d(x):
    L, d = x.shape[1], x.shape[-1]
    half, nf = d // 2, d // 4
    t = jnp.arange(L)
    inv = ROPE_BASE ** (-jnp.arange(nf, dtype=F32) / nf)

    def rot(u, pos):
        ang = pos.astype(F32)[:, None] * inv[None, :]
        cos, sin = jnp.cos(ang)[None, :, None, :], jnp.sin(ang)[None, :, None, :]
        u1, u2 = u[..., :nf].astype(F32), u[..., nf:].astype(F32)
        return jnp.concatenate([u1 * cos - u2 * sin, u1 * sin + u2 * cos], axis=-1)

    return jnp.concatenate([rot(x[..., :half], t // GRID_W), rot(x[..., half:], t % GRID_W)], axis=-1).astype(x.dtype)


def ec_moe(h, router, w1, w3, w2, layer):
    B, T, D = h.shape
    cap = EC_CAPACITY_FACTOR * T // N_EXPERTS
    aff = jax.nn.softmax((h @ router).astype(F32), axis=-1)
    gate, idx = lax.top_k(jnp.swapaxes(aff, 1, 2), cap)
    xin = jax.vmap(lambda hb, ib: hb[ib])(h, idx)
    if B * cap <= 1024:
        merge = lambda t: jnp.swapaxes(t, 0, 1).reshape((1, N_EXPERTS, B * cap) + t.shape[3:])
        y = expert_ffn(merge(xin), merge(gate), w1, w3, w2, layer)
        y = jnp.swapaxes(y.reshape(N_EXPERTS, B, cap, D), 0, 1)
    else:
        y = expert_ffn(xin, gate, w1, w3, w2, layer)
    out = jnp.zeros((B, T, D), y.dtype)
    for e in range(N_EXPERTS):
        out = jax.vmap(lambda ob, ib, yb: ob.at[ib].add(yb, unique_indices=True))(out, idx[:, e], y[:, e])
    return out.astype(h.dtype)


def rwkv_readout(y, r, k, v, xg, r_k, g_up, ln_g, ln_b):
    B, L = r.shape[:2]
    hs = lambda t: t.reshape(B, L, RWKV_HEADS, RWKV_HEAD)
    yf = hs(y.astype(F32))
    mu = jnp.mean(yf, axis=-1, keepdims=True)
    var = jnp.mean(jnp.square(yf - mu), axis=-1, keepdims=True)
    yn = ((yf - mu) * lax.rsqrt(var + RWKV_LN_EPS)).reshape(B, L, D_RWKV) * ln_g + ln_b
    bonus = (jnp.sum(hs(r) * hs(k) * r_k, axis=-1, keepdims=True) * hs(v)).reshape(B, L, D_RWKV)
    g = jax.nn.sigmoid(xg) @ g_up
    return ((yn + bonus) * g).astype(r.dtype)


EVEN_TAIL = EVEN_SPLIT[6:]


def even_mixer(proj_lat, proj_ctx, conv_w, k_k, k_a, r_k, w0, w_up, a0, a_up, g_up, ln_g, ln_b, need_ctx):
    B = proj_lat[0].shape[0]
    p_lat = list(proj_lat[:6]) + _split(proj_lat[6], EVEN_TAIL)
    p_ctx = list(proj_ctx[:6]) + _split(proj_ctx[6], EVEN_TAIL)

    def conv_branch(u, gate_b, gate_c):
        return gate_b * conv3(gate_c * u, conv_w)

    def unit_key(k):
        Bq, L = k.shape[:2]
        kkf = (k * k_k).astype(F32).reshape(Bq, L, RWKV_HEADS, RWKV_HEAD)
        return (kkf * lax.rsqrt(jnp.sum(kkf * kkf, axis=-1, keepdims=True) + 1e-12)).reshape(Bq, L, D_RWKV)

    def dir_gates(xw, xa, d):
        w_raw = (w0[d] + jnp.tanh(xw) @ w_up[d]).astype(F32)
        return -jnp.exp(-jax.nn.softplus(-w_raw) - 0.5), jax.nn.sigmoid(a0[d] + xa @ a_up[d])

    kk_ctx, kk_lat = unit_key(p_ctx[4]), unit_key(p_lat[4])
    st0 = jnp.zeros((B, RWKV_HEADS // 2, PAIR, PAIR), F32)
    y_lat, y_ctx = 0.0, 0.0
    for d, rev in ((0, False), (1, True)):
        yc, st_c = rwkv_chunked(*p_ctx[3:6], kk_ctx, *dir_gates(*p_ctx[6:8], d), k_a, st0, reverse=rev)
        yl, _ = rwkv_chunked(*p_lat[3:6], kk_lat, *dir_gates(*p_lat[6:8], d), k_a, st_c, reverse=rev)
        y_lat = y_lat + yl
        if need_ctx:
            y_ctx = y_ctx + yc
    cat_lat = jnp.concatenate([conv_branch(*p_lat[0:3]),
                               rwkv_readout(y_lat, *p_lat[3:6], p_lat[8], r_k, g_up, ln_g, ln_b)], axis=-1)
    cat_ctx = None
    if need_ctx:
        cat_ctx = jnp.concatenate([conv_branch(*p_ctx[0:3]),
                                   rwkv_readout(y_ctx, *p_ctx[3:6], p_ctx[8], r_k, g_up, ln_g, ln_b)], axis=-1)
    return cat_lat, cat_ctx


def ctx_attention(q, k, v):
    s = jnp.einsum('bhqd,bhkd->bhqk', q, k).astype(F32) * (q.shape[-1] ** -0.5)
    p = jax.nn.softmax(s, axis=-1).astype(v.dtype)
    return jnp.einsum('bhqk,bhkd->bhqd', p, v)


def gla_log_gate(ga, a_up_d, a_b_d):
    B, L = ga.shape[:2]
    lg = jax.nn.log_sigmoid((ga @ a_up_d + a_b_d).astype(F32)) / GLA_GATE_TEMP
    return lg.reshape(B, L, GLA_HEADS, GLA_DK)


def gla_readout(o, gr, ln_g):
    B, L = gr.shape[:2]
    return (rmsnorm(o, ln_g).reshape(B, L, D_GLA_V) * jax.nn.silu(gr)).astype(gr.dtype)


def odd_mixer(proj_lat, proj_ctx, qn_g, kn_g, rpb, a_up, a_b, gla_ln_g, need_ctx):
    nq, nk, nv, gq, gk, gv, gr, ga = proj_lat
    cnq, cnk, cnv, cgq, cgk, cgv, cgr, cga = proj_ctx
    B, L = nq.shape[:2]
    Lc = cnq.shape[1]

    def nat_heads(t, g=None):
        t = t.reshape(t.shape[0], t.shape[1], NAT_HEADS, NAT_HEAD)
        if g is not None:
            t = rmsnorm(t, g)
        return jnp.swapaxes(t, 1, 2)

    nat_lat = nat_attention(nq, nk, nv, cnk, cnv, rpb, qn_g, kn_g)

    gh = lambda t, d: t.reshape(t.shape[0], t.shape[1], GLA_HEADS, d)
    qscale = GLA_DK ** -0.5
    q = (rope_2d(gh(gq, GLA_DK)) * qscale).reshape(B, L, D_GLA_K)
    k = rope_2d(gh(gk, GLA_DK)).reshape(B, L, D_GLA_K)
    qc = cgq * qscale
    st0 = jnp.zeros((B, GLA_HEADS // 2, GLA_VPAIR, GLA_KPAIR), F32)
    o_lat, o_ctx = 0.0, 0.0
    for d in range(2):
        lg_c = gla_log_gate(cga, a_up[d], a_b[d]).reshape(B, Lc, D_GLA_K)
        lg_l = gla_log_gate(ga, a_up[d], a_b[d]).reshape(B, L, D_GLA_K)
        oc, st_c = gla_scan(qc, cgk, cgv, lg_c, st0, reverse=(d == 1))
        ol, _ = gla_scan(q, k, gv, lg_l, st_c, reverse=(d == 1))
        o_lat = o_lat + ol
        if need_ctx:
            o_ctx = o_ctx + oc
    o_lat = gh(o_lat, GLA_DV)
    if need_ctx:
        o_ctx = gh(o_ctx, GLA_DV)
    cat_lat = jnp.concatenate([nat_lat, gla_readout(o_lat, gr, gla_ln_g)], axis=-1)
    cat_ctx = None
    if need_ctx:
        kc, vc = nat_heads(cnk, kn_g), nat_heads(cnv)
        nat_ctx = jnp.swapaxes(ctx_attention(nat_heads(cnq, qn_g), kc, vc), 1, 2).reshape(B, Lc, D_NAT)
        cat_ctx = jnp.concatenate([nat_ctx, gla_readout(o_ctx, cgr, gla_ln_g)], axis=-1)
    return cat_lat, cat_ctx


def kernel(x, c, ctx, c_ctx, ada_w, ada_b, norm1_g, norm2_g, ev_w_in, ev_w_out, conv_w, rw_k_k, rw_k_a, rw_r_k, rw_w0, rw_w_up, rw_a0, rw_a_up, rw_g_up, rw_ln_g, rw_ln_b, od_w_in, od_w_out, nat_qn_g, nat_kn_g, nat_rpb, gla_a_up, gla_a_b, gla_ln_g, moe_router, moe_w1, moe_w3, moe_w2):
    depth = ada_w.shape[0]
    ctx_s = ctx
    silu_c = jax.nn.silu(c)
    silu_cc = jax.nn.silu(c_ctx)
    for l in range(depth):
        last = l == depth - 1
        j = l // 2
        sh1, sc1, gt1, sh2, sc2, gt2 = _split((silu_c @ ada_w[l] + ada_b[l])[:, None, :], [D_MODEL] * 6)
        csh1, csc1, cgt1, csh2, csc2, cgt2 = _split(silu_cc @ ada_w[l] + ada_b[l], [D_MODEL] * 6)
        even = l % 2 == 0
        w_in = ev_w_in[j] if even else od_w_in[j]
        sizes = EVEN_SPLIT[:6] + [sum(EVEN_TAIL)] if even else ODD_SPLIT
        proj_lat = in_proj_split(x, norm1_g[l], sh1, sc1, w_in, sizes)
        proj_ctx = in_proj_split(ctx_s, norm1_g[l], csh1, csc1, w_in, sizes)
        if even:
            cat_lat, cat_ctx = even_mixer(proj_lat, proj_ctx, conv_w[j], rw_k_k[j], rw_k_a[j], rw_r_k[j],
                                          rw_w0[j], rw_w_up[j], rw_a0[j], rw_a_up[j], rw_g_up[j], rw_ln_g[j],
                                          rw_ln_b[j], not last)
            w_out = ev_w_out[j]
        else:
            cat_lat, cat_ctx = odd_mixer(proj_lat, proj_ctx, nat_qn_g[j], nat_kn_g[j], nat_rpb[j],
                                         gla_a_up[j], gla_a_b[j], gla_ln_g[j], not last)
            w_out = od_w_out[j]
        experts = (moe_w1, moe_w3, moe_w2, l)
        x = out_proj_residual(cat_lat, w_out, x, gt1)
        x = x + gt2 * ec_moe(modulate(x, norm2_g[l], sh2, sc2), moe_router[l], *experts)
        if not last:
            ctx_s = out_proj_residual(cat_ctx, w_out, ctx_s, cgt1)
            ctx_s = ctx_s + cgt2 * ec_moe(modulate(ctx_s, norm2_g[l], csh2, csc2), moe_router[l], *experts)
    return x
```
